```python
import jax
import jax.numpy as jnp
from jax import lax
import numpy as np

D_MODEL = 1024
BATCH = 32
SEQ = 2048
DEPTH = 1

CTX_LEN = 256
GRID_W = 64

POOL_WINDOWS = (2, 4, 8, 16)
POOL_GROUPS = len(POOL_WINDOWS)
POOL_GROUP_DIM = 128
POOL_WIDTH = POOL_GROUPS * POOL_GROUP_DIM

GLA_HEADS = 4
GLA_DK = D_MODEL // 2
GLA_DV = D_MODEL
GLA_HK = GLA_DK // GLA_HEADS
GLA_HV = GLA_DV // GLA_HEADS
GLA_GATE_RANK = 16
GLA_GATE_NORMALIZER = 16.0
GLA_CHUNK = 64

N_BRANCHES = 2
IN_WIDTHS = (POOL_WIDTH, GLA_DK, GLA_DK, GLA_DV, GLA_DV, N_BRANCHES * D_MODEL, 2 * GLA_GATE_RANK)
IN_SPLITS = tuple(int(s) for s in np.cumsum(IN_WIDTHS)[:-1])
D_IN = int(sum(IN_WIDTHS))

N_GROUPS = 4
EXPERTS_PER_GROUP = 8
N_EXPERTS = N_GROUPS * EXPERTS_PER_GROUP
TOP_K_IN_GROUP = 2
D_EXPERT = D_MODEL // 2

MOD_CHUNKS = 6
EPS = 1e-6

kernel_name = 'hybrid_pool_gla_hmoe_dit_block'


def _rmsnorm(x, g):
    xf = x.astype(jnp.float32)
    y = xf * lax.rsqrt(jnp.mean(xf * xf, axis=-1, keepdims=True) + EPS)
    return (y * g.astype(jnp.float32)).astype(x.dtype)


def _adaln(cond, w_mod, b_mod):
    return jnp.split(jnp.matmul(jax.nn.silu(cond), w_mod) + b_mod, MOD_CHUNKS, axis=-1)


def _heads(t, hd):
    b, l, _ = t.shape
    return t.reshape(b, l, GLA_HEADS, hd).transpose(0, 2, 1, 3)


def _flip(t):
    return jnp.flip(t, axis=2)


def _project(h, w_in, a2_f, ab_f, a2_b, ab_b):
    p = jnp.einsum('bld,de->ble', h, w_in)
    u, q, k, v, g, gates, lr = jnp.split(p, IN_SPLITS, axis=-1)
    lr_f, lr_b = jnp.split(lr, 2, axis=-1)
    la_f = jax.nn.log_sigmoid(jnp.matmul(lr_f, a2_f).astype(jnp.float32) + ab_f.astype(jnp.float32)) / GLA_GATE_NORMALIZER
    la_b = jax.nn.log_sigmoid(jnp.matmul(lr_b, a2_b).astype(jnp.float32) + ab_b.astype(jnp.float32)) / GLA_GATE_NORMALIZER
    return (u, _heads(q, GLA_HK) * (GLA_HK ** -0.5), _heads(k, GLA_HK), _heads(v, GLA_HV),
            _heads(la_f, GLA_HK), _heads(la_b, GLA_HK), g, gates)


def _centred_pool_minus_self(u):
    n, l, _ = u.shape
    uf = u.astype(jnp.float32)
    cs = jnp.concatenate([jnp.zeros((n, 1, POOL_WIDTH), jnp.float32), jnp.cumsum(uf, axis=1)], axis=1)
    t = np.arange(l)
    means = []
    for gi, w in enumerate(POOL_WINDOWS):
        lo = np.clip(t - w // 2, 0, l)
        hi = np.clip(t + w // 2, 0, l)
        cg = cs[..., gi * POOL_GROUP_DIM:(gi + 1) * POOL_GROUP_DIM]
        cnt = jnp.asarray((hi - lo)[:, None], jnp.float32)
        means.append((cg[:, hi] - cg[:, lo]) / cnt)
    return (jnp.concatenate(means, axis=-1) - uf).astype(u.dtype)


def _gla_chunked(q, k, v, log_a, s0):
    b, h, l, _ = q.shape
    dv = v.shape[-1]
    n = l // GLA_CHUNK

    def chunks(t):
        return t.astype(jnp.float32).reshape(b, h, n, GLA_CHUNK, t.shape[-1])

    qc, kc, vc, la = chunks(q), chunks(k), chunks(v), chunks(log_a)
    cum = jnp.cumsum(la, axis=3)
    last = cum[:, :, :, -1:, :]
    q_dec = qc * jnp.exp(cum)
    k_inv = kc * jnp.exp(-cum)
    k_rem = kc * jnp.exp(last - cum)
    lower = np.tril(np.ones((GLA_CHUNK, GLA_CHUNK), dtype=bool))
    scores = jnp.where(lower, jnp.einsum('bhnid,bhnjd->bhnij', q_dec, k_inv), 0.0)
    o_intra = jnp.einsum('bhnij,bhnjv->bhniv', scores, vc)
    decay = jnp.exp(last[:, :, :, 0, :])

    def step(state, inp):
        q_n, k_n, v_n, d_n = inp
        o_n = jnp.einsum('bhid,bhdv->bhiv', q_n, state)
        state = d_n[..., None] * state + jnp.einsum('bhid,bhiv->bhdv', k_n, v_n)
        return state, o_n

    xs = tuple(jnp.moveaxis(t, 2, 0) for t in (q_dec, k_rem, vc, decay))
    s_final, o_inter = lax.scan(step, s0, xs)
    o = o_intra + jnp.moveaxis(o_inter, 0, 2)
    return o.reshape(b, h, l, dv).astype(v.dtype), s_final


def _gla_final_state(k, v, log_a):
    cum = jnp.cumsum(log_a, axis=2)
    k_rem = k.astype(jnp.float32) * jnp.exp(cum[:, :, -1:, :] - cum)
    return jnp.einsum('bhtd,bhtv->bhdv', k_rem, v.astype(jnp.float32))


def _gla_bidirectional(q, k, v, la_f, la_b, s0_f, s0_b):
    o_f, _ = _gla_chunked(q, k, v, la_f, s0_f)
    o_b, _ = _gla_chunked(_flip(q), _flip(k), _flip(v), _flip(la_b), s0_b)
    return o_f + _flip(o_b)


def _mixer_out(pooled, o, g, gates, pool_w, pool_scale, gla_onorm_g, w_pool_br, w_gla_br, w_o):
    b, l, _ = pooled.shape
    y_pool = jnp.einsum('blgc,gce->blge', pooled.reshape(b, l, POOL_GROUPS, POOL_GROUP_DIM), pool_w)
    y_pool = jnp.matmul(y_pool.reshape(b, l, POOL_WIDTH) * pool_scale, w_pool_br)
    o = _rmsnorm(o, gla_onorm_g).transpose(0, 2, 1, 3).reshape(b, l, GLA_DV)
    y_gla = jnp.matmul(o * jax.nn.silu(g), w_gla_br)
    gate_pool, gate_gla = jnp.split(jax.nn.sigmoid(gates), N_BRANCHES, axis=-1)
    return jnp.matmul(gate_pool * y_pool + gate_gla * y_gla, w_o)


def _hier_moe(h, rg_w, rg_b, re_w, re_b, w1, w3, w2):
    b, l, d = h.shape
    t = h.reshape(b * l, d)
    grp_logits = (jnp.matmul(t, rg_w) + rg_b).astype(jnp.float32)
    grp_top, grp_idx = lax.top_k(grp_logits, 1)
    p_grp = jnp.exp(grp_top - jax.nn.logsumexp(grp_logits, axis=-1, keepdims=True))
    exp_logits = (jnp.matmul(t, re_w) + re_b).astype(jnp.float32).reshape(-1, N_GROUPS, EXPERTS_PER_GROUP)
    in_grp = jnp.einsum('tg,tge->te', jax.nn.one_hot(grp_idx[:, 0], N_GROUPS, dtype=jnp.float32), exp_logits)
    top_v, top_i = lax.top_k(in_grp, TOP_K_IN_GROUP)
    w = jax.nn.softmax(top_v, axis=-1) * p_grp
    expert_idx = grp_idx * EXPERTS_PER_GROUP + top_i
    combine = jnp.einsum('tk,tke->te', w, jax.nn.one_hot(expert_idx, N_EXPERTS, dtype=jnp.float32)).astype(h.dtype)
    out = jnp.zeros_like(t)
    for e in range(N_EXPERTS):
        hid = jax.nn.silu(jnp.matmul(t, w1[e])) * jnp.matmul(t, w3[e])
        out = out + combine[:, e:e + 1] * jnp.matmul(hid, w2[e])
    return out.reshape(b, l, d)


def setup_inputs(seed: int = 0) -> dict:
    key = jax.random.key(seed)
    ks = jax.random.split(key, 32)
    d = D_MODEL

    def nrm(k, shape, scale):
        return jax.random.normal(k, shape, jnp.float32) * scale

    return {
        'x': nrm(ks[0], (BATCH, SEQ, d), 1.0),
        'c': nrm(ks[1], (BATCH, d), 1.0),
        'ctx': nrm(ks[2], (BATCH, CTX_LEN, d), 1.0),
        'c_ctx': nrm(ks[3], (d,), 1.0),
        'w_mod': nrm(ks[4], (DEPTH, d, MOD_CHUNKS * d), 0.5 * d ** -0.5),
        'b_mod': nrm(ks[5], (DEPTH, MOD_CHUNKS * d), 0.02),
        'norm1_g': 1.0 + nrm(ks[6], (DEPTH, d), 0.05),
        'norm2_g': 1.0 + nrm(ks[7], (DEPTH, d), 0.05),
        'w_in': nrm(ks[8], (DEPTH, d, D_IN), d ** -0.5),
        'gla_a2_f': nrm(ks[9], (DEPTH, GLA_GATE_RANK, GLA_DK), GLA_GATE_RANK ** -0.5),
        'gla_ab_f': nrm(ks[10], (DEPTH, GLA_DK), 0.5),
        'gla_a2_b': nrm(ks[11], (DEPTH, GLA_GATE_RANK, GLA_DK), GLA_GATE_RANK ** -0.5),
        'gla_ab_b': nrm(ks[12], (DEPTH, GLA_DK), 0.5),
        'gla_onorm_g': 1.0 + nrm(ks[13], (DEPTH, GLA_HV), 0.05),
        'pool_w': nrm(ks[14], (DEPTH, POOL_GROUPS, POOL_GROUP_DIM, POOL_GROUP_DIM), POOL_GROUP_DIM ** -0.5),
        'pool_scale': 1.0 + nrm(ks[15], (DEPTH, POOL_WIDTH), 0.1),
        'w_pool_br': nrm(ks[16], (DEPTH, POOL_WIDTH, d), POOL_WIDTH ** -0.5),
        'w_gla_br': nrm(ks[17], (DEPTH, GLA_DV, d), GLA_DV ** -0.5),
        'w_o': nrm(ks[18], (DEPTH, d, d), d ** -0.5),
        'router_grp_w': nrm(ks[19], (DEPTH, d, N_GROUPS), d ** -0.5),
        'router_grp_b': nrm(ks[20], (DEPTH, N_GROUPS), 0.01),
        'router_exp_w': nrm(ks[21], (DEPTH, d, N_EXPERTS), d ** -0.5),
        'router_exp_b': nrm(ks[22], (DEPTH, N_EXPERTS), 0.01),
        'moe_w1': nrm(ks[23], (DEPTH, N_EXPERTS, d, D_EXPERT), d ** -0.5),
        'moe_w3': nrm(ks[24], (DEPTH, N_EXPERTS, d, D_EXPERT), d ** -0.5),
        'moe_w2': nrm(ks[25], (DEPTH, N_EXPERTS, D_EXPERT, d), D_EXPERT ** -0.5),
        'final_norm_g': 1.0 + nrm(ks[26], (d,), 0.05),
    }


def reference(x, c, ctx, c_ctx, w_mod, b_mod, norm1_g, norm2_g, w_in, gla_a2_f, gla_ab_f, gla_a2_b, gla_ab_b,
              gla_onorm_g, pool_w, pool_scale, w_pool_br, w_gla_br, w_o, router_grp_w, router_grp_b,
              router_exp_w, router_exp_b, moe_w1, moe_w3, moe_w2, final_norm_g):
    b, seq, d = x.shape
    rows = seq // GRID_W
    for i in range(DEPTH):
        proj_params = (w_in[i], gla_a2_f[i], gla_ab_f[i], gla_a2_b[i], gla_ab_b[i])
        mix_params = (pool_w[i], pool_scale[i], gla_onorm_g[i], w_pool_br[i], w_gla_br[i], w_o[i])
        moe_params = (router_grp_w[i], router_grp_b[i], router_exp_w[i], router_exp_b[i],
                      moe_w1[i], moe_w3[i], moe_w2[i])
        sh1, sc1, g1, sh2, sc2, g2 = [m[:, None, :] for m in _adaln(c, w_mod[i], b_mod[i])]
        csh1, csc1, cg1, csh2, csc2, cg2 = _adaln(c_ctx, w_mod[i], b_mod[i])

        hc = _rmsnorm(ctx, norm1_g[i]) * (1 + csc1) + csh1
        uc, qc, kc, vc, lfc, lbc, gc, gtc = _project(hc, *proj_params)
        st_f = _gla_final_state(kc, vc, lfc)
        st_b = _gla_final_state(_flip(kc), _flip(vc), _flip(lbc))

        hx = _rmsnorm(x, norm1_g[i]) * (1 + sc1) + sh1
        ux, qx, kx, vx, lfx, lbx, gx, gtx = _project(hx, *proj_params)
        pooled_x = _centred_pool_minus_self(ux.reshape(b * rows, GRID_W, POOL_WIDTH)).reshape(b, seq, POOL_WIDTH)
        ox = _gla_bidirectional(qx, kx, vx, lfx, lbx, st_f, st_b)
        x = x + g1 * _mixer_out(pooled_x, ox, gx, gtx, *mix_params)
        x = x + g2 * _hier_moe(_rmsnorm(x, norm2_g[i]) * (1 + sc2) + sh2, *moe_params)

        if i + 1 < DEPTH:
            zero_state = jnp.zeros_like(st_f)
            oc = _gla_bidirectional(qc, kc, vc, lfc, lbc, zero_state, zero_state)
            ctx = ctx + cg1 * _mixer_out(_centred_pool_minus_self(uc), oc, gc, gtc, *mix_params)
            ctx = ctx + cg2 * _hier_moe(_rmsnorm(ctx, norm2_g[i]) * (1 + csc2) + csh2, *moe_params)
    return _rmsnorm(x, final_norm_g)
```

```python
import functools

import jax
import jax.numpy as jnp
import numpy as np
from jax import lax
from jax.experimental import pallas as pl
from jax.experimental.pallas import tpu as pltpu

f32 = jnp.float32
bf16 = jnp.bfloat16
i32 = jnp.int32
u32 = jnp.uint32

D_MODEL = 1024
GRID_W = 64
POOL_WINDOWS = (2, 4, 8, 16)
POOL_GROUP_DIM = 128
POOL_WIDTH = len(POOL_WINDOWS) * POOL_GROUP_DIM
GLA_HEADS = 4
GLA_DK = 512
GLA_DV = 1024
GLA_HK = GLA_DK // GLA_HEADS
GLA_HV = GLA_DV // GLA_HEADS
GLA_GATE_RANK = 16
GLA_GATE_NORMALIZER = 16.0
N_GROUPS = 4
EXPERTS_PER_GROUP = 8
N_EXPERTS = N_GROUPS * EXPERTS_PER_GROUP
D_EXPERT = 512
MOD_CHUNKS = 6
EPS = 1e-6

_U0, _Q0, _K0, _V0, _G0, _GT0, _LR0, _IN_END = 0, 512, 1024, 1536, 2560, 3584, 5632, 5664

V7X_LANES = 128
V7X_SUBLANES = 8
V7X_MXU_DIM = 256
V7X_SCOPED_VMEM_CAP = 60000 * 1024

GLA_CHUNK = 128
GLA_HALF = GLA_CHUNK // 2
TOK_TILE = 512
MOE_TILE = 128
COND_PAD = 8
ROUTE_ROWS = 40
N_PAIRS = N_GROUPS * (EXPERTS_PER_GROUP * (EXPERTS_PER_GROUP - 1) // 2)


def _vmem_limit(nbytes):
    return int(min(max(nbytes, 16 * 1024 * 1024), V7X_SCOPED_VMEM_CAP))


def _dot(a, b):
    return jnp.dot(a, b, preferred_element_type=f32)


def _dot_nt(a, b):
    return lax.dot_general(a, b, (((1,), (1,)), ((), ())), preferred_element_type=f32)


def _dot_tn(a, b):
    return lax.dot_general(a, b, (((0,), (0,)), ((), ())), preferred_element_type=f32)


def _sigmoid(x):
    return 1.0 / (1.0 + jnp.exp(-x))


def _silu(x):
    return x * _sigmoid(x)


def _log_sigmoid(z):
    return jnp.minimum(z, 0.0) - jnp.log1p(jnp.exp(-jnp.abs(z)))


def _norm_mod(x, g, scale, shift):
    ms = jnp.mean(x * x, axis=-1, keepdims=True)
    return (x * lax.rsqrt(ms + EPS) * g) * (1.0 + scale) + shift


def _tri_cumsum(tri, la):
    hi = la.astype(bf16)
    lo = (la - hi.astype(f32)).astype(bf16)
    return _dot(tri, hi) + _dot(tri, lo)


def _log_decays(hx, wlr_ref, a2_ref, ab_ref):
    lr = _dot(hx, wlr_ref[...])
    z = _dot(lr.astype(bf16), a2_ref[...]) + ab_ref[...]
    return _log_sigmoid(z) * (1.0 / GLA_GATE_NORMALIZER)


def _adaln_kernel(c_ref, w_ref, b_ref, o_ref):
    c = c_ref[...]
    o_ref[...] = _dot(_silu(c).astype(bf16), w_ref[...].astype(bf16)) + b_ref[...]


def _adaln(cond, w_mod, b_mod):
    rows, d = cond.shape
    n = w_mod.shape[1]
    tn = 512
    return pl.pallas_call(
        _adaln_kernel,
        out_shape=jax.ShapeDtypeStruct((rows, n), f32),
        grid=(n // tn,),
        in_specs=[pl.BlockSpec((rows, d), lambda j: (0, 0)),
                  pl.BlockSpec((d, tn), lambda j: (0, j)),
                  pl.BlockSpec((1, tn), lambda j: (0, j))],
        out_specs=pl.BlockSpec((rows, tn), lambda j: (0, j)),
        compiler_params=pltpu.CompilerParams(dimension_semantics=("arbitrary",)),
        name="adaln",
    )(cond, w_mod, b_mod)


def _ctx_kernel(ctx_ref, mod_ref, g_ref, wkv_ref, wlr_ref, a2_ref, ab_ref, sf_ref, sb_ref):
    x = ctx_ref[0]
    n = x.shape[0]
    m = mod_ref[0]
    hc = _norm_mod(x, g_ref[...], m[1:2], m[0:1]).astype(bf16)
    kv = _dot(hc, wkv_ref[...])
    la = _log_decays(hc, wlr_ref, a2_ref, ab_ref)
    row = lax.broadcasted_iota(i32, (n, n), 0)
    col = lax.broadcasted_iota(i32, (n, n), 1)
    lower = jnp.where(row >= col, 1.0, 0.0).astype(bf16)
    upper = jnp.where(col >= row, 1.0, 0.0).astype(bf16)
    cum_f = _tri_cumsum(lower, la[:, :GLA_DK])
    cum_b = _tri_cumsum(upper, la[:, GLA_DK:])
    k = kv[:, :GLA_DK]
    v = kv[:, GLA_DK:].astype(bf16)
    kr_f = (k * jnp.exp(cum_f[n - 1:n] - cum_f)).astype(bf16)
    kr_b = (k * jnp.exp(cum_b[0:1] - cum_b)).astype(bf16)
    for h in range(GLA_HEADS):
        ks = slice(h * GLA_HK, (h + 1) * GLA_HK)
        vs = slice(h * GLA_HV, (h + 1) * GLA_HV)
        sf_ref[0, h] = _dot_tn(kr_f[:, ks], v[:, vs])
        sb_ref[0, h] = _dot_tn(kr_b[:, ks], v[:, vs])


def _ctx_states(ctx, mod3, ctx_row, g, wkv, wlr, a2bd, ab):
    b, n, d = ctx.shape
    st = jax.ShapeDtypeStruct((b, GLA_HEADS, GLA_HK, GLA_HV), f32)
    const = lambda i: (0, 0)
    st_spec = pl.BlockSpec((1, GLA_HEADS, GLA_HK, GLA_HV), lambda i: (i, 0, 0, 0))
    return pl.pallas_call(
        _ctx_kernel,
        out_shape=(st, st),
        grid=(b,),
        in_specs=[pl.BlockSpec((1, n, d), lambda i: (i, 0, 0)),
                  pl.BlockSpec((1, MOD_CHUNKS, d), lambda i: (ctx_row, 0, 0)),
                  pl.BlockSpec((1, d), const),
                  pl.BlockSpec(wkv.shape, const),
                  pl.BlockSpec(wlr.shape, const),
                  pl.BlockSpec(a2bd.shape, const),
                  pl.BlockSpec(ab.shape, const)],
        out_specs=(st_spec, st_spec),
        compiler_params=pltpu.CompilerParams(dimension_semantics=("arbitrary",),
                                             vmem_limit_bytes=_vmem_limit(32 * 1024 * 1024)),
        name="ctx_states",
    )(ctx, mod3, g, wkv, wlr, a2bd, ab)


def _qkv_kernel(x_ref, mod_ref, g_ref, wqkv_ref, wlr_ref, a2_ref, ab_ref, q_ref, k_ref, v_ref, la_ref):
    m = mod_ref[0]
    hx = _norm_mod(x_ref[0], g_ref[...], m[1:2], m[0:1]).astype(bf16)
    p = _dot(hx, wqkv_ref[...])
    q_ref[0] = p[:, :GLA_DK].astype(bf16)
    k_ref[0] = p[:, GLA_DK:2 * GLA_DK].astype(bf16)
    v_ref[0] = p[:, 2 * GLA_DK:].astype(bf16)
    la_ref[0] = _log_decays(hx, wlr_ref, a2_ref, ab_ref)


def _qkv(x, mod3, g, wqkv, wlr, a2bd, ab):
    b, s, d = x.shape
    tm = TOK_TILE
    const = lambda i, j: (0, 0)
    tok = lambda w: pl.BlockSpec((1, tm, w), lambda i, j: (i, j, 0))
    return pl.pallas_call(
        _qkv_kernel,
        out_shape=(jax.ShapeDtypeStruct((b, s, GLA_DK), bf16), jax.ShapeDtypeStruct((b, s, GLA_DK), bf16),
                   jax.ShapeDtypeStruct((b, s, GLA_DV), bf16), jax.ShapeDtypeStruct((b, s, 2 * GLA_DK), f32)),
        grid=(b, s // tm),
        in_specs=[tok(d),
                  pl.BlockSpec((1, MOD_CHUNKS, d), lambda i, j: (i, 0, 0)),
                  pl.BlockSpec((1, d), const),
                  pl.BlockSpec(wqkv.shape, const),
                  pl.BlockSpec(wlr.shape, const),
                  pl.BlockSpec(a2bd.shape, const),
                  pl.BlockSpec(ab.shape, const)],
        out_specs=(tok(GLA_DK), tok(GLA_DK), tok(GLA_DV), tok(2 * GLA_DK)),
        compiler_params=pltpu.CompilerParams(dimension_semantics=("arbitrary", "arbitrary"),
                                             vmem_limit_bytes=_vmem_limit(48 * 1024 * 1024)),
        name="qkv",
    )(x, mod3, g, wqkv, wlr, a2bd, ab)


def _gla_kernel(q_ref, k_ref, v_ref, la_ref, s0f_ref, s0b_ref, o_ref, s_sc, oacc):
    seq = q_ref.shape[1]
    nc = seq // GLA_CHUNK
    c_len = GLA_CHUNK
    s_sc[0:GLA_HEADS] = s0f_ref[0]
    s_sc[GLA_HEADS:2 * GLA_HEADS] = s0b_ref[0]
    row = lax.broadcasted_iota(i32, (c_len, c_len), 0)
    col = lax.broadcasted_iota(i32, (c_len, c_len), 1)
    lower = row >= col
    upper = col >= row
    tri_l = jnp.where(lower, 1.0, 0.0).astype(bf16)
    tri_u = jnp.where(upper, 1.0, 0.0).astype(bf16)
    scale = GLA_HK ** -0.5

    def step(c_fwd, c_bwd, second_visit):
        prep = []
        tots = []
        for d, c, tri, r_row, t_row in ((0, c_fwd, tri_l, GLA_HALF - 1, c_len - 1), (1, c_bwd, tri_u, GLA_HALF, 0)):
            rows = pl.ds(pl.multiple_of(c * c_len, c_len), c_len)
            cum = _tri_cumsum(tri, la_ref[0, rows, d * GLA_DK:(d + 1) * GLA_DK])
            r = cum[r_row:r_row + 1]
            tot = cum[t_row:t_row + 1]
            qd = q_ref[0, rows, :].astype(f32) * (jnp.exp(cum - r) * scale)
            ki = k_ref[0, rows, :].astype(f32) * jnp.exp(r - cum)
            qs = (qd * jnp.exp(r)).astype(bf16)
            kr = (ki * jnp.exp(tot - r)).astype(bf16)
            prep.append((rows, qd.astype(bf16), qs, ki.astype(bf16), kr, v_ref[0, rows, :]))
            tots.extend(tot[:, h * GLA_HK:(h + 1) * GLA_HK] for h in range(GLA_HEADS))
        pad = jnp.zeros((c_len - 2 * GLA_HEADS, GLA_HK), f32)
        dec_cols = jnp.exp(jnp.concatenate(tots + [pad], axis=0).T)
        for d, mask in ((0, lower), (1, upper)):
            rows, qd, qs, ki, kr, vv = prep[d]
            for h in range(GLA_HEADS):
                j = d * GLA_HEADS + h
                ks = slice(h * GLA_HK, (h + 1) * GLA_HK)
                vs = slice(h * GLA_HV, (h + 1) * GLA_HV)
                a = jnp.where(mask, _dot_nt(qd[:, ks], ki[:, ks]), 0.0).astype(bf16)
                s_old = s_sc[j]
                o = _dot(jnp.concatenate([a, qs[:, ks]], axis=1),
                         jnp.concatenate([vv[:, vs], s_old.astype(bf16)], axis=0))
                s_sc[j] = dec_cols[:, j:j + 1] * s_old + _dot_tn(kr[:, ks], vv[:, vs])
                if second_visit:
                    o_ref[0, rows, vs] = (oacc[rows, vs] + o).astype(o_ref.dtype)
                else:
                    oacc[rows, vs] = o

    def first(i, carry):
        step(i, nc - 1 - i, False)
        return carry

    def second(i, carry):
        step(i, nc - 1 - i, True)
        return carry

    lax.fori_loop(0, nc // 2, first, 0)
    lax.fori_loop(nc // 2, nc, second, 0)


def _gla(q, k, v, la, s0f, s0b):
    b, s, _ = q.shape
    tok = lambda w: pl.BlockSpec((1, s, w), lambda i: (i, 0, 0))
    st_spec = pl.BlockSpec((1, GLA_HEADS, GLA_HK, GLA_HV), lambda i: (i, 0, 0, 0))
    return pl.pallas_call(
        _gla_kernel,
        out_shape=jax.ShapeDtypeStruct((b, s, GLA_DV), bf16),
        grid=(b,),
        in_specs=[tok(GLA_DK), tok(GLA_DK), tok(GLA_DV), tok(2 * GLA_DK), st_spec, st_spec],
        out_specs=tok(GLA_DV),
        scratch_shapes=[pltpu.VMEM((2 * GLA_HEADS, GLA_HK, GLA_HV), f32), pltpu.VMEM((s, GLA_DV), f32)],
        compiler_params=pltpu.CompilerParams(dimension_semantics=("arbitrary",),
                                             vmem_limit_bytes=V7X_SCOPED_VMEM_CAP),
        name="gla",
    )(q, k, v, la, s0f, s0b)


def _mix_kernel(x_ref, o_ref, mod_ref, n1_ref, n2_ref, wugg_ref, band_ref, icnt_ref, pwbd_ref, pscale_ref,
                wpool_ref, onorm_ref, wgla_ref, wo_ref, wr_ref, br_ref,
                x1_ref, h2p_ref, ri_ref, rw_ref):
    x = x_ref[0]
    tm = x.shape[0]
    m = mod_ref[0]
    hx = _norm_mod(x, n1_ref[...], m[1:2], m[0:1]).astype(bf16)
    p = _dot(hx, wugg_ref[...])
    u = p[:, :POOL_WIDTH]
    g = p[:, POOL_WIDTH:POOL_WIDTH + GLA_DV]
    gates = p[:, POOL_WIDTH + GLA_DV:]

    ub = u.astype(bf16)
    slab = band_ref.shape[1]
    sums = []
    for s in range(tm // slab):
        rs = slice(s * slab, (s + 1) * slab)
        sums.append(jnp.concatenate(
            [_dot(band_ref[gi], ub[rs, gi * POOL_GROUP_DIM:(gi + 1) * POOL_GROUP_DIM])
             for gi in range(len(POOL_WINDOWS))], axis=1))
    pooled = jnp.concatenate(sums, axis=0) * icnt_ref[...] - u
    y_pool = _dot(pooled.astype(bf16), pwbd_ref[...]) * pscale_ref[...]
    y_pool = _dot(y_pool.astype(bf16), wpool_ref[...])

    o = o_ref[0].astype(f32)
    normed = []
    for h in range(GLA_HEADS):
        oh = o[:, h * GLA_HV:(h + 1) * GLA_HV]
        normed.append(oh * lax.rsqrt(jnp.mean(oh * oh, axis=-1, keepdims=True) + EPS))
    on = jnp.concatenate(normed, axis=1) * onorm_ref[...]
    y_gla = _dot((on * _silu(g)).astype(bf16), wgla_ref[...])

    mixed = _sigmoid(gates[:, :D_MODEL]) * y_pool + _sigmoid(gates[:, D_MODEL:]) * y_gla
    x1 = x + m[2:3] * _dot(mixed.astype(bf16), wo_ref[...])
    x1_ref[0] = x1

    h2 = _norm_mod(x1, n2_ref[...], m[4:5], m[3:4]).astype(bf16)
    half = D_MODEL // 2
    lo = pltpu.bitcast(h2[:, :half].astype(f32), u32) >> 16
    hi = pltpu.bitcast(h2[:, half:].astype(f32), u32) & jnp.uint32(0xFFFF0000)
    h2p_ref[...] = hi | lo

    lt = (_dot(h2, wr_ref[...]) + br_ref[...]).T
    g0, g1, g2, g3 = lt[0:1], lt[1:2], lt[2:3], lt[3:4]
    gmax = jnp.maximum(jnp.maximum(g0, g1), jnp.maximum(g2, g3))
    gidx = jnp.where(g0 == gmax, 0, jnp.where(g1 == gmax, 1, jnp.where(g2 == gmax, 2, 3))).astype(i32)
    p_grp = 1.0 / (jnp.exp(g0 - gmax) + jnp.exp(g1 - gmax) + jnp.exp(g2 - gmax) + jnp.exp(g3 - gmax))
    e0 = V7X_SUBLANES
    ing = lt[e0 + 3 * EXPERTS_PER_GROUP:e0 + 4 * EXPERTS_PER_GROUP]
    for gi in (2, 1, 0):
        ing = jnp.where(gidx == gi, lt[e0 + gi * EXPERTS_PER_GROUP:e0 + (gi + 1) * EXPERTS_PER_GROUP], ing)
    rid = lax.broadcasted_iota(i32, ing.shape, 0)
    v1 = jnp.max(ing, axis=0, keepdims=True)
    i1 = jnp.min(jnp.where(ing == v1, rid, EXPERTS_PER_GROUP), axis=0, keepdims=True)
    rest = jnp.where(rid == i1, -jnp.inf, ing)
    v2 = jnp.max(rest, axis=0, keepdims=True)
    i2 = jnp.min(jnp.where(rest == v2, rid, EXPERTS_PER_GROUP), axis=0, keepdims=True)
    t = jnp.exp(v2 - v1)
    w1 = p_grp / (1.0 + t)
    w2 = p_grp * t / (1.0 + t)
    first_low = i1 < i2
    ilo = jnp.minimum(i1, i2)
    ihi = jnp.maximum(i1, i2)
    pair = ilo * (EXPERTS_PER_GROUP - 1) - ((ilo * (ilo - 1)) >> 1) + (ihi - ilo - 1)
    pid = gidx * (N_PAIRS // N_GROUPS) + pair
    ri_ref[...] = jnp.where(rid == 0, pid, 0)
    rw_ref[...] = jnp.where(rid == 0, jnp.where(first_low, w1, w2),
                            jnp.where(rid == 1, jnp.where(first_low, w2, w1), 0.0))


def _mix(x, o, mod3, n1, n2, wugg, band, icnt, pwbd, pscale, wpool, onorm, wgla, wo, wr, br):
    b, s, d = x.shape
    tm = TOK_TILE
    nj = s // tm
    t = b * s
    const2 = lambda i, j: (0, 0)
    const3 = lambda i, j: (0, 0, 0)
    tok = lambda w: pl.BlockSpec((1, tm, w), lambda i, j: (i, j, 0))
    flat = lambda r, w: pl.BlockSpec((r, w), lambda i, j: (i * nj + j, 0))
    lane = pl.BlockSpec((V7X_SUBLANES, tm), lambda i, j: (0, i * nj + j))
    return pl.pallas_call(
        _mix_kernel,
        out_shape=(jax.ShapeDtypeStruct((b, s, d), f32), jax.ShapeDtypeStruct((t, d // 2), u32),
                   jax.ShapeDtypeStruct((V7X_SUBLANES, t), i32), jax.ShapeDtypeStruct((V7X_SUBLANES, t), f32)),
        grid=(b, nj),
        in_specs=[tok(d), tok(GLA_DV),
                  pl.BlockSpec((1, MOD_CHUNKS, d), lambda i, j: (i, 0, 0)),
                  pl.BlockSpec((1, d), const2), pl.BlockSpec((1, d), const2),
                  pl.BlockSpec(wugg.shape, const2),
                  pl.BlockSpec(band.shape, const3),
                  pl.BlockSpec(icnt.shape, const2),
                  pl.BlockSpec(pwbd.shape, const2),
                  pl.BlockSpec(pscale.shape, const2),
                  pl.BlockSpec(wpool.shape, const2),
                  pl.BlockSpec(onorm.shape, const2),
                  pl.BlockSpec(wgla.shape, const2),
                  pl.BlockSpec(wo.shape, const2),
                  pl.BlockSpec(wr.shape, const2),
                  pl.BlockSpec(br.shape, const2)],
        out_specs=(tok(d), flat(tm, d // 2), lane, lane),
        compiler_params=pltpu.CompilerParams(dimension_semantics=("arbitrary", "arbitrary"),
                                             vmem_limit_bytes=V7X_SCOPED_VMEM_CAP),
        name="mix",
    )(x, o, mod3, n1, n2, wugg, band, icnt, pwbd, pscale, wpool, onorm, wgla, wo, wr, br)


def _moe_kernel(ta_ref, tb_ref, nv_ref, src0_ref, srcn_ref, dst_ref, wgt_ref, h2p_ref,
                w13a_ref, w2a_ref, w13b_ref, w2b_ref, y_ref, xbuf, ybuf, gsem, ssem):
    del ta_ref, tb_ref
    i = pl.program_id(0)
    last = pl.num_programs(0) - 1
    nv = nv_ref[0]
    tm = xbuf.shape[1]
    n_tok = y_ref.shape[0] - 2 * tm
    slot = i % 2

    def gather_row(idx_ref, s, r):
        return pltpu.make_async_copy(h2p_ref.at[pl.ds(idx_ref[0, 0, r], 1)], xbuf.at[s, pl.ds(r, 1)], gsem.at[s])

    def scatter_row(s, r):
        return pltpu.make_async_copy(ybuf.at[s, pl.ds(r, 1)], y_ref.at[pl.ds(dst_ref[0, 0, r], 1)], ssem.at[s])

    def start_rows(make):
        def body(r, carry):
            make(r).start()
            return carry
        lax.fori_loop(0, tm, body, 0, unroll=8)

    def gather_tile(s):
        return pltpu.make_async_copy(h2p_ref.at[pl.ds(0, tm)], xbuf.at[s], gsem.at[s])

    def scatter_tile(s):
        return pltpu.make_async_copy(ybuf.at[s], y_ref.at[pl.ds(0, tm)], ssem.at[s])

    @pl.when(i == 0)
    def _():
        ybuf[0] = jnp.zeros(ybuf.shape[1:], ybuf.dtype)
        for s in range(2):
            spare = pltpu.make_async_copy(ybuf.at[0], y_ref.at[pl.ds(n_tok + s * tm, tm)], ssem.at[0])
            spare.start()
            spare.wait()

        @pl.when(nv > 0)
        def _():
            start_rows(lambda r: gather_row(src0_ref, 0, r))

    @pl.when(i + 1 < nv)
    def _():
        start_rows(lambda r: gather_row(srcn_ref, 1 - slot, r))

    @pl.when(i < nv)
    def _():
        gather_tile(slot).wait()
        xp = xbuf[slot]
        xt = jnp.concatenate([pltpu.bitcast(xp << 16, f32).astype(bf16),
                              pltpu.bitcast(xp & jnp.uint32(0xFFFF0000), f32).astype(bf16)], axis=1)

        def expert(w13_ref, w2_ref):
            h = _dot(xt, w13_ref[0])
            hid = _silu(h[:, :D_EXPERT]) * h[:, D_EXPERT:]
            return _dot(hid.astype(bf16), w2_ref[0])

        w = wgt_ref[...]
        y = w[:, 0:1] * expert(w13a_ref, w2a_ref) + w[:, 1:2] * expert(w13b_ref, w2b_ref)

        @pl.when(i >= 2)
        def _():
            scatter_tile(slot).wait()

        ybuf[slot] = y
        start_rows(lambda r: scatter_row(slot, r))

    @pl.when(i == last)
    def _():
        @pl.when(nv >= 1)
        def _():
            scatter_tile((nv - 1) % 2).wait()

        @pl.when(nv >= 2)
        def _():
            scatter_tile(nv % 2).wait()


def _moe(tile_a, tile_b, nvalid, src, dst, wgt, h2p, w13, w2):
    nt, _, tm = src.shape
    t, half = h2p.shape
    d = 2 * half
    idx_blk = (1, 1, tm)
    grid_spec = pltpu.PrefetchScalarGridSpec(
        num_scalar_prefetch=3,
        grid=(nt,),
        in_specs=[pl.BlockSpec(idx_blk, lambda i, ta, tb, nv: (0, 0, 0), memory_space=pltpu.SMEM),
                  pl.BlockSpec(idx_blk, lambda i, ta, tb, nv: (jnp.minimum(i + 1, nt - 1), 0, 0),
                               memory_space=pltpu.SMEM),
                  pl.BlockSpec(idx_blk, lambda i, ta, tb, nv: (i, 0, 0), memory_space=pltpu.SMEM),
                  pl.BlockSpec((tm, 2), lambda i, ta, tb, nv: (i, 0)),
                  pl.BlockSpec(memory_space=pl.ANY),
                  pl.BlockSpec((1, d, 2 * D_EXPERT), lambda i, ta, tb, nv: (ta[i], 0, 0)),
                  pl.BlockSpec((1, D_EXPERT, d), lambda i, ta, tb, nv: (ta[i], 0, 0)),
                  pl.BlockSpec((1, d, 2 * D_EXPERT), lambda i, ta, tb, nv: (tb[i], 0, 0)),
                  pl.BlockSpec((1, D_EXPERT, d), lambda i, ta, tb, nv: (tb[i], 0, 0))],
        out_specs=pl.BlockSpec(memory_space=pl.ANY),
        scratch_shapes=[pltpu.VMEM((2, tm, half), u32), pltpu.VMEM((2, tm, d), f32),
                        pltpu.SemaphoreType.DMA((2,)), pltpu.SemaphoreType.DMA((2,))],
    )
    return pl.pallas_call(
        _moe_kernel,
        out_shape=jax.ShapeDtypeStruct((t + 2 * tm, d), f32),
        grid_spec=grid_spec,
        compiler_params=pltpu.CompilerParams(dimension_semantics=("arbitrary",),
                                             vmem_limit_bytes=_vmem_limit(40 * 1024 * 1024)),
        name="moe",
    )(tile_a, tile_b, nvalid, src, src, dst, wgt, h2p, w13, w2, w13, w2)


def _pair_tables():
    a, b = [], []
    for g in range(N_GROUPS):
        for i in range(EXPERTS_PER_GROUP):
            for j in range(i + 1, EXPERTS_PER_GROUP):
                a.append(g * EXPERTS_PER_GROUP + i)
                b.append(g * EXPERTS_PER_GROUP + j)
    return np.asarray(a, np.int32), np.asarray(b, np.int32)


def _dispatch_plan(pid, wlo, whi, tm):
    t = pid.shape[0]
    nt = (t + N_PAIRS * (tm - 1)) // tm
    order = jnp.argsort(pid, stable=True).astype(i32)
    starts = jnp.searchsorted(pid[order], jnp.arange(N_PAIRS + 1, dtype=i32), side="left").astype(i32)
    counts = starts[1:] - starts[:-1]
    tiles = (counts + tm - 1) // tm
    tile_end = jnp.cumsum(tiles)
    tile_start = tile_end - tiles
    nvalid = tile_end[-1:]
    tile = jnp.arange(nt, dtype=i32)
    grp = jnp.minimum(jnp.searchsorted(tile_end, tile, side="right"), N_PAIRS - 1).astype(i32)
    rank = (tile - tile_start[grp])[:, None] * tm + jnp.arange(tm, dtype=i32)[None, :]
    valid = jnp.logical_and(rank < counts[grp][:, None], (tile < nvalid[0])[:, None])
    tok = order[jnp.clip(starts[grp][:, None] + rank, 0, t - 1)]
    src = jnp.where(valid, tok, 0)
    spare = t + (tile % 2)[:, None] * tm + jnp.arange(tm, dtype=i32)[None, :]
    dst = jnp.where(valid, tok, spare).reshape(nt, 1, tm)
    wgt = jnp.stack([jnp.where(valid, wlo[src], 0.0), jnp.where(valid, whi[src], 0.0)], axis=-1).reshape(nt * tm, 2)
    pa, pb = _pair_tables()
    return jnp.asarray(pa)[grp], jnp.asarray(pb)[grp], nvalid.astype(i32), src.reshape(nt, 1, tm), dst, wgt


def _final_kernel(x1_ref, y_ref, mod_ref, g_ref, out_ref):
    m = mod_ref[0]
    x2 = x1_ref[0] + m[5:6] * y_ref[...]
    ms = jnp.mean(x2 * x2, axis=-1, keepdims=True)
    out_ref[0] = x2 * lax.rsqrt(ms + EPS) * g_ref[...]


def _final(x1, y, mod3, g):
    b, s, d = x1.shape
    tm = TOK_TILE
    nj = s // tm
    tok = pl.BlockSpec((1, tm, d), lambda i, j: (i, j, 0))
    return pl.pallas_call(
        _final_kernel,
        out_shape=jax.ShapeDtypeStruct((b, s, d), f32),
        grid=(b, nj),
        in_specs=[tok,
                  pl.BlockSpec((tm, d), lambda i, j: (i * nj + j, 0)),
                  pl.BlockSpec((1, MOD_CHUNKS, d), lambda i, j: (i, 0, 0)),
                  pl.BlockSpec((1, d), lambda i, j: (0, 0))],
        out_specs=tok,
        compiler_params=pltpu.CompilerParams(dimension_semantics=("arbitrary", "arbitrary"),
                                             vmem_limit_bytes=_vmem_limit(32 * 1024 * 1024)),
        name="final",
    )(x1, y, mod3, g)


def _pool_constants(tm):
    slab = V7X_MXU_DIM
    band = np.zeros((len(POOL_WINDOWS), slab, slab), np.float32)
    icnt = np.zeros((tm, POOL_WIDTH), np.float32)
    for gi, w in enumerate(POOL_WINDOWS):
        for t in range(slab):
            base, pos = (t // GRID_W) * GRID_W, t % GRID_W
            lo, hi = max(pos - w // 2, 0), min(pos + w // 2, GRID_W)
            band[gi, t, base + lo:base + hi] = 1.0
        for t in range(tm):
            pos = t % GRID_W
            cnt = min(pos + w // 2, GRID_W) - max(pos - w // 2, 0)
            icnt[t, gi * POOL_GROUP_DIM:(gi + 1) * POOL_GROUP_DIM] = 1.0 / cnt
    return jnp.asarray(band, bf16), jnp.asarray(icnt, f32)


def _block_diag(blocks):
    n = len(blocks)
    r, c = blocks[0].shape
    out = jnp.zeros((n * r, n * c), blocks[0].dtype)
    for i, blk in enumerate(blocks):
        out = out.at[i * r:(i + 1) * r, i * c:(i + 1) * c].set(blk)
    return out


def kernel(x, c, ctx, c_ctx, w_mod, b_mod, norm1_g, norm2_g, w_in, gla_a2_f, gla_ab_f, gla_a2_b, gla_ab_b,
           gla_onorm_g, pool_w, pool_scale, w_pool_br, w_gla_br, w_o, router_grp_w, router_grp_b,
           router_exp_w, router_exp_b, moe_w1, moe_w3, moe_w2, final_norm_g):
    b, s, d = x.shape
    depth = w_mod.shape[0]
    assert depth == 1 and d == D_MODEL and s % TOK_TILE == 0 and s % (2 * GLA_CHUNK) == 0
    assert TOK_TILE % V7X_MXU_DIM == 0 and V7X_MXU_DIM % GRID_W == 0
    t = b * s

    cond = jnp.concatenate([c, c_ctx[None, :], jnp.zeros((COND_PAD - 1, d), f32)], axis=0)
    mod3 = _adaln(cond, w_mod[0], b_mod[0][None, :]).reshape(b + COND_PAD, MOD_CHUNKS, d)

    w_in_b = w_in[0].astype(bf16)
    wqkv = w_in_b[:, _Q0:_G0]
    wkv = w_in_b[:, _K0:_G0]
    wugg = jnp.concatenate([w_in_b[:, _U0:_Q0], w_in_b[:, _G0:_LR0]], axis=1)
    wlr = jnp.pad(w_in_b[:, _LR0:_IN_END], ((0, 0), (0, V7X_LANES - 2 * GLA_GATE_RANK)))
    a2bd = jnp.pad(_block_diag([gla_a2_f[0], gla_a2_b[0]]).astype(bf16),
                   ((0, V7X_LANES - 2 * GLA_GATE_RANK), (0, 0)))
    ab = jnp.concatenate([gla_ab_f[0], gla_ab_b[0]])[None, :]
    n1 = norm1_g[0][None, :]
    n2 = norm2_g[0][None, :]

    s0f, s0b = _ctx_states(ctx, mod3, b, n1, wkv, wlr, a2bd, ab)
    q, k, v, la = _qkv(x, mod3, n1, wqkv, wlr, a2bd, ab)
    o = _gla(q, k, v, la, s0f, s0b)

    band, icnt = _pool_constants(TOK_TILE)
    pwbd = _block_diag([pool_w[0, gi] for gi in range(len(POOL_WINDOWS))]).astype(bf16)
    onorm = jnp.tile(gla_onorm_g[0], GLA_HEADS)[None, :]
    wr = jnp.zeros((d, V7X_LANES), f32)
    wr = wr.at[:, :N_GROUPS].set(router_grp_w[0]).at[:, V7X_SUBLANES:V7X_SUBLANES + N_EXPERTS].set(router_exp_w[0])
    br = jnp.zeros((1, V7X_LANES), f32)
    br = br.at[0, :N_GROUPS].set(router_grp_b[0]).at[0, V7X_SUBLANES:V7X_SUBLANES + N_EXPERTS].set(router_exp_b[0])
    x1, h2p, ri, rw = _mix(x, o, mod3, n1, n2, wugg, band, icnt, pwbd, pool_scale[0][None, :],
                           w_pool_br[0].astype(bf16), onorm, w_gla_br[0].astype(bf16), w_o[0].astype(bf16),
                           wr.astype(bf16), br)

    plan = _dispatch_plan(ri[0], rw[0], rw[1], MOE_TILE)
    w13 = jnp.concatenate([moe_w1[0], moe_w3[0]], axis=-1).astype(bf16)
    y = _moe(*plan, h2p, w13, moe_w2[0].astype(bf16))
    return _final(x1, y, mod3, final_norm_g[None, :])
```

```python
import jax
import jax.numpy as jnp
import numpy as np
from jax import lax
from jax.experimental import pallas as pl
from jax.experimental.pallas import tpu as pltpu

f32 = jnp.float32
bf16 = jnp.bfloat16
i32 = jnp.int32
u32 = jnp.uint32

D_MODEL = 1024
GRID_W = 64
POOL_WINDOWS = (2, 4, 8, 16)
POOL_GROUP_DIM = 128
POOL_WIDTH = len(POOL_WINDOWS) * POOL_GROUP_DIM
GLA_HEADS = 4
GLA_DK = 512
GLA_DV = 1024
GLA_HK = GLA_DK // GLA_HEADS
GLA_HV = GLA_DV // GLA_HEADS
GLA_GATE_RANK = 16
GLA_GATE_NORMALIZER = 16.0
N_GROUPS = 4
EXPERTS_PER_GROUP = 8
N_EXPERTS = N_GROUPS * EXPERTS_PER_GROUP
D_EXPERT = 512
MOD_CHUNKS = 6
EPS = 1e-6

_U0, _Q0, _K0, _V0, _G0, _GT0, _LR0, _IN_END = 0, 512, 1024, 1536, 2560, 3584, 5632, 5664

V7X_LANES = 128
V7X_SUBLANES = 8
V7X_MXU_DIM = 256
V7X_SCOPED_VMEM_CAP = 60000 * 1024

GLA_CHUNK = 128
GLA_HALF = GLA_CHUNK // 2
TOK_TILE = 512
MOE_TILE = 128
COND_PAD = 8
N_PAIRS = N_GROUPS * (EXPERTS_PER_GROUP * (EXPERTS_PER_GROUP - 1) // 2)


def _vmem_limit(nbytes):
    return int(min(max(nbytes, 16 * 1024 * 1024), V7X_SCOPED_VMEM_CAP))


def _dot(a, b):
    return jnp.dot(a, b, preferred_element_type=f32)


def _dot_nt(a, b):
    return lax.dot_general(a, b, (((1,), (1,)), ((), ())), preferred_element_type=f32)


def _dot_tn(a, b):
    return lax.dot_general(a, b, (((0,), (0,)), ((), ())), preferred_element_type=f32)


def _sigmoid(x):
    return 1.0 / (1.0 + jnp.exp(-x))


def _silu(x):
    return x * _sigmoid(x)


def _log_sigmoid(z):
    return jnp.minimum(z, 0.0) - jnp.log1p(jnp.exp(-jnp.abs(z)))


def _norm_mod(x, g, scale, shift):
    ms = jnp.mean(x * x, axis=-1, keepdims=True)
    return (x * lax.rsqrt(ms + EPS) * g) * (1.0 + scale) + shift


def _tri_cumsum(tri, la):
    hi = la.astype(bf16)
    lo = (la - hi.astype(f32)).astype(bf16)
    return _dot(tri, hi) + _dot(tri, lo)


def _log_decays(hx, wlr_ref, a2_ref, ab_ref):
    lr = _dot(hx, wlr_ref[...])
    z = _dot(lr.astype(bf16), a2_ref[...]) + ab_ref[...]
    return _log_sigmoid(z) * (1.0 / GLA_GATE_NORMALIZER)


def _adaln_kernel(c_ref, w_ref, b_ref, o_ref):
    c = c_ref[...]
    o_ref[...] = _dot(_silu(c).astype(bf16), w_ref[...].astype(bf16)) + b_ref[...]


def _adaln(cond, w_mod, b_mod):
    rows, d = cond.shape
    n = w_mod.shape[1]
    tn = 512
    return pl.pallas_call(
        _adaln_kernel,
        out_shape=jax.ShapeDtypeStruct((rows, n), f32),
        grid=(n // tn,),
        in_specs=[pl.BlockSpec((rows, d), lambda j: (0, 0)),
                  pl.BlockSpec((d, tn), lambda j: (0, j)),
                  pl.BlockSpec((1, tn), lambda j: (0, j))],
        out_specs=pl.BlockSpec((rows, tn), lambda j: (0, j)),
        compiler_params=pltpu.CompilerParams(dimension_semantics=("arbitrary",)),
        name="adaln",
    )(cond, w_mod, b_mod)


def _ctx_kernel(ctx_ref, mod_ref, g_ref, wkv_ref, wlr_ref, a2_ref, ab_ref, sf_ref, sb_ref):
    x = ctx_ref[0]
    n = x.shape[0]
    m = mod_ref[0]
    hc = _norm_mod(x, g_ref[...], m[1:2], m[0:1]).astype(bf16)
    kv = _dot(hc, wkv_ref[...])
    la = _log_decays(hc, wlr_ref, a2_ref, ab_ref)
    row = lax.broadcasted_iota(i32, (n, n), 0)
    col = lax.broadcasted_iota(i32, (n, n), 1)
    lower = jnp.where(row >= col, 1.0, 0.0).astype(bf16)
    upper = jnp.where(col >= row, 1.0, 0.0).astype(bf16)
    cum_f = _tri_cumsum(lower, la[:, :GLA_DK])
    cum_b = _tri_cumsum(upper, la[:, GLA_DK:])
    k = kv[:, :GLA_DK]
    v = kv[:, GLA_DK:].astype(bf16)
    kr_f = (k * jnp.exp(cum_f[n - 1:n] - cum_f)).astype(bf16)
    kr_b = (k * jnp.exp(cum_b[0:1] - cum_b)).astype(bf16)
    for h in range(GLA_HEADS):
        ks = slice(h * GLA_HK, (h + 1) * GLA_HK)
        vs = slice(h * GLA_HV, (h + 1) * GLA_HV)
        sf_ref[0, h] = _dot_tn(kr_f[:, ks], v[:, vs])
        sb_ref[0, h] = _dot_tn(kr_b[:, ks], v[:, vs])


def _ctx_states(ctx, mod3, ctx_row, g, wkv, wlr, a2bd, ab):
    b, n, d = ctx.shape
    st = jax.ShapeDtypeStruct((b, GLA_HEADS, GLA_HK, GLA_HV), f32)
    const = lambda i: (0, 0)
    st_spec = pl.BlockSpec((1, GLA_HEADS, GLA_HK, GLA_HV), lambda i: (i, 0, 0, 0))
    return pl.pallas_call(
        _ctx_kernel,
        out_shape=(st, st),
        grid=(b,),
        in_specs=[pl.BlockSpec((1, n, d), lambda i: (i, 0, 0)),
                  pl.BlockSpec((1, MOD_CHUNKS, d), lambda i: (ctx_row, 0, 0)),
                  pl.BlockSpec((1, d), const),
                  pl.BlockSpec(wkv.shape, const),
                  pl.BlockSpec(wlr.shape, const),
                  pl.BlockSpec(a2bd.shape, const),
                  pl.BlockSpec(ab.shape, const)],
        out_specs=(st_spec, st_spec),
        compiler_params=pltpu.CompilerParams(dimension_semantics=("arbitrary",),
                                             vmem_limit_bytes=_vmem_limit(32 * 1024 * 1024)),
        name="ctx_states",
    )(ctx, mod3, g, wkv, wlr, a2bd, ab)


def _qkv_kernel(x_ref, mod_ref, g_ref, wqkv_ref, wlr_ref, a2_ref, ab_ref, q_ref, k_ref, v_ref, la_ref):
    m = mod_ref[0]
    hx = _norm_mod(x_ref[0], g_ref[...], m[1:2], m[0:1]).astype(bf16)
    p = _dot(hx, wqkv_ref[...])
    q_ref[0] = p[:, :GLA_DK].astype(bf16)
    k_ref[0] = p[:, GLA_DK:2 * GLA_DK].astype(bf16)
    v_ref[0] = p[:, 2 * GLA_DK:].astype(bf16)
    la_ref[0] = _log_decays(hx, wlr_ref, a2_ref, ab_ref)


def _qkv(x, mod3, g, wqkv, wlr, a2bd, ab):
    b, s, d = x.shape
    tm = TOK_TILE
    const = lambda i, j: (0, 0)
    tok = lambda w: pl.BlockSpec((1, tm, w), lambda i, j: (i, j, 0))
    return pl.pallas_call(
        _qkv_kernel,
        out_shape=(jax.ShapeDtypeStruct((b, s, GLA_DK), bf16), jax.ShapeDtypeStruct((b, s, GLA_DK), bf16),
                   jax.ShapeDtypeStruct((b, s, GLA_DV), bf16), jax.ShapeDtypeStruct((b, s, 2 * GLA_DK), f32)),
        grid=(b, s // tm),
        in_specs=[tok(d),
                  pl.BlockSpec((1, MOD_CHUNKS, d), lambda i, j: (i, 0, 0)),
                  pl.BlockSpec((1, d), const),
                  pl.BlockSpec(wqkv.shape, const),
                  pl.BlockSpec(wlr.shape, const),
                  pl.BlockSpec(a2bd.shape, const),
                  pl.BlockSpec(ab.shape, const)],
        out_specs=(tok(GLA_DK), tok(GLA_DK), tok(GLA_DV), tok(2 * GLA_DK)),
        compiler_params=pltpu.CompilerParams(dimension_semantics=("arbitrary", "arbitrary"),
                                             vmem_limit_bytes=_vmem_limit(48 * 1024 * 1024)),
        name="qkv",
    )(x, mod3, g, wqkv, wlr, a2bd, ab)


def _gla_kernel(q_ref, k_ref, v_ref, la_ref, s0f_ref, s0b_ref, o_ref, s_sc, oacc):
    seq = q_ref.shape[1]
    nc = seq // GLA_CHUNK
    c_len = GLA_CHUNK
    s_sc[0:GLA_HEADS] = s0f_ref[0]
    s_sc[GLA_HEADS:2 * GLA_HEADS] = s0b_ref[0]
    row = lax.broadcasted_iota(i32, (c_len, c_len), 0)
    col = lax.broadcasted_iota(i32, (c_len, c_len), 1)
    lower = row >= col
    upper = col >= row
    tri_l = jnp.where(lower, 1.0, 0.0).astype(bf16)
    tri_u = jnp.where(upper, 1.0, 0.0).astype(bf16)
    scale = GLA_HK ** -0.5

    def step(c_fwd, c_bwd, second_visit):
        prep = []
        tots = []
        for d, c, tri, r_row, t_row in ((0, c_fwd, tri_l, GLA_HALF - 1, c_len - 1), (1, c_bwd, tri_u, GLA_HALF, 0)):
            rows = pl.ds(pl.multiple_of(c * c_len, c_len), c_len)
            cum = _tri_cumsum(tri, la_ref[0, rows, d * GLA_DK:(d + 1) * GLA_DK])
            r = cum[r_row:r_row + 1]
            tot = cum[t_row:t_row + 1]
            qd = q_ref[0, rows, :].astype(f32) * (jnp.exp(cum - r) * scale)
            ki = k_ref[0, rows, :].astype(f32) * jnp.exp(r - cum)
            qs = (qd * jnp.exp(r)).astype(bf16)
            kr = (ki * jnp.exp(tot - r)).astype(bf16)
            prep.append((rows, qd.astype(bf16), qs, ki.astype(bf16), kr, v_ref[0, rows, :]))
            tots.extend(tot[:, h * GLA_HK:(h + 1) * GLA_HK] for h in range(GLA_HEADS))
        pad = jnp.zeros((c_len - 2 * GLA_HEADS, GLA_HK), f32)
        dec_cols = jnp.exp(jnp.concatenate(tots + [pad], axis=0).T)
        for d, mask in ((0, lower), (1, upper)):
            rows, qd, qs, ki, kr, vv = prep[d]
            for h in range(GLA_HEADS):
                j = d * GLA_HEADS + h
                ks = slice(h * GLA_HK, (h + 1) * GLA_HK)
                vs = slice(h * GLA_HV, (h + 1) * GLA_HV)
                a = jnp.where(mask, _dot_nt(qd[:, ks], ki[:, ks]), 0.0).astype(bf16)
                s_old = s_sc[j]
                o = _dot(jnp.concatenate([a, qs[:, ks]], axis=1),
                         jnp.concatenate([vv[:, vs], s_old.astype(bf16)], axis=0))
                s_sc[j] = dec_cols[:, j:j + 1] * s_old + _dot_tn(kr[:, ks], vv[:, vs])
                if second_visit:
                    o_ref[0, rows, vs] = (oacc[rows, vs] + o).astype(o_ref.dtype)
                else:
                    oacc[rows, vs] = o

    def first(i, carry):
        step(i, nc - 1 - i, False)
        return carry

    def second(i, carry):
        step(i, nc - 1 - i, True)
        return carry

    lax.fori_loop(0, nc // 2, first, 0)
    lax.fori_loop(nc // 2, nc, second, 0)


def _gla(q, k, v, la, s0f, s0b):
    b, s, _ = q.shape
    tok = lambda w: pl.BlockSpec((1, s, w), lambda i: (i, 0, 0))
    st_spec = pl.BlockSpec((1, GLA_HEADS, GLA_HK, GLA_HV), lambda i: (i, 0, 0, 0))
    return pl.pallas_call(
        _gla_kernel,
        out_shape=jax.ShapeDtypeStruct((b, s, GLA_DV), bf16),
        grid=(b,),
        in_specs=[tok(GLA_DK), tok(GLA_DK), tok(GLA_DV), tok(2 * GLA_DK), st_spec, st_spec],
        out_specs=tok(GLA_DV),
        scratch_shapes=[pltpu.VMEM((2 * GLA_HEADS, GLA_HK, GLA_HV), f32), pltpu.VMEM((s, GLA_DV), f32)],
        compiler_params=pltpu.CompilerParams(dimension_semantics=("arbitrary",),
                                             vmem_limit_bytes=V7X_SCOPED_VMEM_CAP),
        name="gla",
    )(q, k, v, la, s0f, s0b)


def _mix_kernel(x_ref, o_ref, mod_ref, n1_ref, n2_ref, wugg_ref, band_ref, icnt_ref, pwbd_ref, pscale_ref,
                wpool_ref, onorm_ref, wgla_ref, wo_ref, wr_ref, br_ref,
                x1_ref, h2_ref, ri_ref, rw_ref):
    x = x_ref[0]
    tm = x.shape[0]
    m = mod_ref[0]
    hx = _norm_mod(x, n1_ref[...], m[1:2], m[0:1]).astype(bf16)
    p = _dot(hx, wugg_ref[...])
    u = p[:, :POOL_WIDTH]
    g = p[:, POOL_WIDTH:POOL_WIDTH + GLA_DV]
    gates = p[:, POOL_WIDTH + GLA_DV:]

    ub = u.astype(bf16)
    slab = band_ref.shape[1]
    sums = []
    for s in range(tm // slab):
        rs = slice(s * slab, (s + 1) * slab)
        sums.append(jnp.concatenate(
            [_dot(band_ref[gi], ub[rs, gi * POOL_GROUP_DIM:(gi + 1) * POOL_GROUP_DIM])
             for gi in range(len(POOL_WINDOWS))], axis=1))
    pooled = jnp.concatenate(sums, axis=0) * icnt_ref[...] - u
    y_pool = _dot(pooled.astype(bf16), pwbd_ref[...]) * pscale_ref[...]
    y_pool = _dot(y_pool.astype(bf16), wpool_ref[...])

    o = o_ref[0].astype(f32)
    normed = []
    for h in range(GLA_HEADS):
        oh = o[:, h * GLA_HV:(h + 1) * GLA_HV]
        normed.append(oh * lax.rsqrt(jnp.mean(oh * oh, axis=-1, keepdims=True) + EPS))
    on = jnp.concatenate(normed, axis=1) * onorm_ref[...]
    y_gla = _dot((on * _silu(g)).astype(bf16), wgla_ref[...])

    mixed = _sigmoid(gates[:, :D_MODEL]) * y_pool + _sigmoid(gates[:, D_MODEL:]) * y_gla
    x1 = x + m[2:3] * _dot(mixed.astype(bf16), wo_ref[...])
    x1_ref[0] = x1

    h2f = _norm_mod(x1, n2_ref[...], m[4:5], m[3:4])
    h2_ref[...] = h2f
    h2 = h2f.astype(bf16)

    lt = (_dot(h2, wr_ref[...]) + br_ref[...]).T
    g0, g1, g2, g3 = lt[0:1], lt[1:2], lt[2:3], lt[3:4]
    gmax = jnp.maximum(jnp.maximum(g0, g1), jnp.maximum(g2, g3))
    gidx = jnp.where(g0 == gmax, 0, jnp.where(g1 == gmax, 1, jnp.where(g2 == gmax, 2, 3))).astype(i32)
    p_grp = 1.0 / (jnp.exp(g0 - gmax) + jnp.exp(g1 - gmax) + jnp.exp(g2 - gmax) + jnp.exp(g3 - gmax))
    e0 = V7X_SUBLANES
    ing = lt[e0 + 3 * EXPERTS_PER_GROUP:e0 + 4 * EXPERTS_PER_GROUP]
    for gi in (2, 1, 0):
        ing = jnp.where(gidx == gi, lt[e0 + gi * EXPERTS_PER_GROUP:e0 + (gi + 1) * EXPERTS_PER_GROUP], ing)
    rid = lax.broadcasted_iota(i32, ing.shape, 0)
    v1 = jnp.max(ing, axis=0, keepdims=True)
    i1 = jnp.min(jnp.where(ing == v1, rid, EXPERTS_PER_GROUP), axis=0, keepdims=True)
    rest = jnp.where(rid == i1, -jnp.inf, ing)
    v2 = jnp.max(rest, axis=0, keepdims=True)
    i2 = jnp.min(jnp.where(rest == v2, rid, EXPERTS_PER_GROUP), axis=0, keepdims=True)
    t = jnp.exp(v2 - v1)
    w1 = p_grp / (1.0 + t)
    w2 = p_grp * t / (1.0 + t)
    first_low = i1 < i2
    ilo = jnp.minimum(i1, i2)
    ihi = jnp.maximum(i1, i2)
    pair = ilo * (EXPERTS_PER_GROUP - 1) - ((ilo * (ilo - 1)) >> 1) + (ihi - ilo - 1)
    pid = gidx * (N_PAIRS // N_GROUPS) + pair
    ri_ref[...] = jnp.where(rid == 0, pid, 0)
    rw_ref[...] = jnp.where(rid == 0, jnp.where(first_low, w1, w2),
                            jnp.where(rid == 1, jnp.where(first_low, w2, w1), 0.0))


def _mix(x, o, mod3, n1, n2, wugg, band, icnt, pwbd, pscale, wpool, onorm, wgla, wo, wr, br):
    b, s, d = x.shape
    tm = TOK_TILE
    nj = s // tm
    t = b * s
    const2 = lambda i, j: (0, 0)
    const3 = lambda i, j: (0, 0, 0)
    tok = lambda w: pl.BlockSpec((1, tm, w), lambda i, j: (i, j, 0))
    flat = lambda r, w: pl.BlockSpec((r, w), lambda i, j: (i * nj + j, 0))
    lane = pl.BlockSpec((V7X_SUBLANES, tm), lambda i, j: (0, i * nj + j))
    return pl.pallas_call(
        _mix_kernel,
        out_shape=(jax.ShapeDtypeStruct((b, s, d), f32), jax.ShapeDtypeStruct((t, d), f32),
                   jax.ShapeDtypeStruct((V7X_SUBLANES, t), i32), jax.ShapeDtypeStruct((V7X_SUBLANES, t), f32)),
        grid=(b, nj),
        in_specs=[tok(d), tok(GLA_DV),
                  pl.BlockSpec((1, MOD_CHUNKS, d), lambda i, j: (i, 0, 0)),
                  pl.BlockSpec((1, d), const2), pl.BlockSpec((1, d), const2),
                  pl.BlockSpec(wugg.shape, const2),
                  pl.BlockSpec(band.shape, const3),
                  pl.BlockSpec(icnt.shape, const2),
                  pl.BlockSpec(pwbd.shape, const2),
                  pl.BlockSpec(pscale.shape, const2),
                  pl.BlockSpec(wpool.shape, const2),
                  pl.BlockSpec(onorm.shape, const2),
                  pl.BlockSpec(wgla.shape, const2),
                  pl.BlockSpec(wo.shape, const2),
                  pl.BlockSpec(wr.shape, const2),
                  pl.BlockSpec(br.shape, const2)],
        out_specs=(tok(d), flat(tm, d), lane, lane),
        compiler_params=pltpu.CompilerParams(dimension_semantics=("arbitrary", "arbitrary"),
                                             vmem_limit_bytes=V7X_SCOPED_VMEM_CAP),
        name="mix",
    )(x, o, mod3, n1, n2, wugg, band, icnt, pwbd, pscale, wpool, onorm, wgla, wo, wr, br)


def _moe_kernel(ta_ref, tb_ref, nv_ref, src0_ref, srcn0_ref, srcn1_ref, dstp0_ref, dstp1_ref, dstl_ref,
                wgt0_ref, wgt1_ref, h2_ref, w13a0_ref, w2a0_ref, w13b0_ref, w2b0_ref,
                w13a1_ref, w2a1_ref, w13b1_ref, w2b1_ref, y_ref, xbuf, ybuf, gsem, ssem):
    del ta_ref, tb_ref
    j = pl.program_id(0)
    last = pl.num_programs(0) - 1
    nv = nv_ref[0]
    tm = xbuf.shape[1]
    n_tok = y_ref.shape[0] - 2 * tm

    def gather_row(idx_ref, s, r):
        return pltpu.make_async_copy(h2_ref.at[pl.ds(idx_ref[0, 0, r], 1)], xbuf.at[s, pl.ds(r, 1)], gsem.at[s])

    def scatter_row(idx_ref, s, r):
        return pltpu.make_async_copy(ybuf.at[s, pl.ds(r, 1)], y_ref.at[pl.ds(idx_ref[0, 0, r], 1)], ssem.at[s])

    def start_rows(make):
        def body(r, carry):
            make(r).start()
            return carry
        lax.fori_loop(0, tm, body, 0, unroll=8)

    def gather_tile(s):
        return pltpu.make_async_copy(h2_ref.at[pl.ds(0, tm)], xbuf.at[s], gsem.at[s])

    def scatter_tile(s):
        return pltpu.make_async_copy(ybuf.at[s], y_ref.at[pl.ds(0, tm)], ssem.at[s])

    @pl.when(j == 0)
    def _():
        ybuf[...] = jnp.zeros(ybuf.shape, ybuf.dtype)
        for s in range(2):
            spare = pltpu.make_async_copy(ybuf.at[0], y_ref.at[pl.ds(n_tok + s * tm, tm)], ssem.at[0])
            spare.start()
            spare.wait()

        @pl.when(nv > 0)
        def _():
            start_rows(lambda r: gather_row(src0_ref, 0, r))

    n_pieces = 12
    per_piece = -(-tm // n_pieces)

    def tile_body(s, tile, srcn_ref, dstp_ref, wgt_ref, w13a_ref, w2a_ref, w13b_ref, w2b_ref):
        @pl.when(tile < nv)
        def _():
            gather_tile(s).wait()

            @pl.when(tile >= 1)
            def _():
                scatter_tile(s).wait()

            piece = [0]

            def issue_rows():
                lo = piece[0] * per_piece
                rows = range(lo, min(lo + per_piece, tm))
                src_rows = [srcn_ref[0, 0, r] for r in rows]
                dst_rows = [dstp_ref[0, 0, r] for r in rows]
                for r, sr, dr in zip(rows, src_rows, dst_rows):
                    pltpu.make_async_copy(h2_ref.at[pl.ds(sr, 1)], xbuf.at[1 - s, pl.ds(r, 1)],
                                          gsem.at[1 - s]).start()
                    pltpu.make_async_copy(ybuf.at[1 - s, pl.ds(r, 1)], y_ref.at[pl.ds(dr, 1)],
                                          ssem.at[1 - s]).start()
                piece[0] += 1

            xt = xbuf[s].astype(bf16)
            w = wgt_ref[...]
            half = D_EXPERT // 2
            y = None
            for e, (w13_ref, w2_ref) in enumerate(((w13a_ref, w2a_ref), (w13b_ref, w2b_ref))):
                acc = None
                for c in range(2):
                    h1 = _dot(xt, w13_ref[0, :, c * half:(c + 1) * half])
                    issue_rows()
                    h3 = _dot(xt, w13_ref[0, :, D_EXPERT + c * half:D_EXPERT + (c + 1) * half])
                    issue_rows()
                    part = _dot((_silu(h1) * h3).astype(bf16), w2_ref[0, c * half:(c + 1) * half, :])
                    issue_rows()
                    acc = part if acc is None else acc + part
                y = w[:, e:e + 1] * acc if y is None else y + w[:, e:e + 1] * acc
            assert piece[0] == n_pieces
            ybuf[s] = y

    tile_body(0, 2 * j, srcn0_ref, dstp0_ref, wgt0_ref, w13a0_ref, w2a0_ref, w13b0_ref, w2b0_ref)
    tile_body(1, 2 * j + 1, srcn1_ref, dstp1_ref, wgt1_ref, w13a1_ref, w2a1_ref, w13b1_ref, w2b1_ref)

    @pl.when(jnp.logical_and(j == last, nv > 0))
    def _():
        sl = (nv - 1) % 2
        scatter_tile(1 - sl).wait()
        start_rows(lambda r: scatter_row(dstl_ref, sl, r))
        scatter_tile(sl).wait()
        gather_tile(1 - sl).wait()


def _moe(tile_a, tile_b, nvalid, src, dst, wgt, h2, w13, w2):
    nt, _, tm = src.shape
    assert nt % 2 == 0 and dst.shape[0] == nt + 1
    t, d = h2.shape
    idx_blk = (1, 1, tm)
    smem_idx = lambda fn: pl.BlockSpec(idx_blk, fn, memory_space=pltpu.SMEM)
    w13_blk = (1, d, 2 * D_EXPERT)
    w2_blk = (1, D_EXPERT, d)

    def weights(s):
        return [pl.BlockSpec(w13_blk, lambda j, ta, tb, nv: (ta[2 * j + s], 0, 0)),
                pl.BlockSpec(w2_blk, lambda j, ta, tb, nv: (ta[2 * j + s], 0, 0)),
                pl.BlockSpec(w13_blk, lambda j, ta, tb, nv: (tb[2 * j + s], 0, 0)),
                pl.BlockSpec(w2_blk, lambda j, ta, tb, nv: (tb[2 * j + s], 0, 0))]

    grid_spec = pltpu.PrefetchScalarGridSpec(
        num_scalar_prefetch=3,
        grid=(nt // 2,),
        in_specs=[smem_idx(lambda j, ta, tb, nv: (0, 0, 0)),
                  smem_idx(lambda j, ta, tb, nv: (jnp.minimum(2 * j + 1, nt - 1), 0, 0)),
                  smem_idx(lambda j, ta, tb, nv: (jnp.minimum(2 * j + 2, nt - 1), 0, 0)),
                  smem_idx(lambda j, ta, tb, nv: (jnp.where(j == 0, nt, 2 * j - 1), 0, 0)),
                  smem_idx(lambda j, ta, tb, nv: (2 * j, 0, 0)),
                  smem_idx(lambda j, ta, tb, nv: (jnp.maximum(nv[0] - 1, 0), 0, 0)),
                  pl.BlockSpec((tm, 2), lambda j, ta, tb, nv: (2 * j, 0)),
                  pl.BlockSpec((tm, 2), lambda j, ta, tb, nv: (2 * j + 1, 0)),
                  pl.BlockSpec(memory_space=pl.ANY)] + weights(0) + weights(1),
        out_specs=pl.BlockSpec(memory_space=pl.ANY),
        scratch_shapes=[pltpu.VMEM((2, tm, d), f32), pltpu.VMEM((2, tm, d), f32),
                        pltpu.SemaphoreType.DMA((2,)), pltpu.SemaphoreType.DMA((2,))],
    )
    return pl.pallas_call(
        _moe_kernel,
        out_shape=jax.ShapeDtypeStruct((t + 2 * tm, d), f32),
        grid_spec=grid_spec,
        compiler_params=pltpu.CompilerParams(dimension_semantics=("arbitrary",),
                                             vmem_limit_bytes=_vmem_limit(48 * 1024 * 1024)),
        name="moe",
    )(tile_a, tile_b, nvalid, src, src, src, dst, dst, dst, wgt, wgt, h2, w13, w2, w13, w2, w13, w2, w13, w2)


def _pair_tables():
    a, b = [], []
    for g in range(N_GROUPS):
        for i in range(EXPERTS_PER_GROUP):
            for j in range(i + 1, EXPERTS_PER_GROUP):
                a.append(g * EXPERTS_PER_GROUP + i)
                b.append(g * EXPERTS_PER_GROUP + j)
    return np.asarray(a, np.int32), np.asarray(b, np.int32)


def _dispatch_plan(pid, wlo, whi, tm):
    t = pid.shape[0]
    nt = 2 * (-(-((t + N_PAIRS * (tm - 1)) // tm) // 2))
    order = jnp.argsort(pid, stable=True).astype(i32)
    starts = jnp.searchsorted(pid[order], jnp.arange(N_PAIRS + 1, dtype=i32), side="left").astype(i32)
    counts = starts[1:] - starts[:-1]
    tiles = (counts + tm - 1) // tm
    tile_end = jnp.cumsum(tiles)
    tile_start = tile_end - tiles
    nvalid = tile_end[-1:]
    tile = jnp.arange(nt, dtype=i32)
    grp = jnp.minimum(jnp.searchsorted(tile_end, tile, side="right"), N_PAIRS - 1).astype(i32)
    rank = (tile - tile_start[grp])[:, None] * tm + jnp.arange(tm, dtype=i32)[None, :]
    valid = jnp.logical_and(rank < counts[grp][:, None], (tile < nvalid[0])[:, None])
    tok = order[jnp.clip(starts[grp][:, None] + rank, 0, t - 1)]
    src = jnp.where(valid, tok, 0)
    row = jnp.arange(tm, dtype=i32)[None, :]
    dst = jnp.where(valid, tok, t + (tile % 2)[:, None] * tm + row)
    dst = jnp.concatenate([dst, t + tm + row], axis=0).reshape(nt + 1, 1, tm)
    wgt = jnp.stack([jnp.where(valid, wlo[src], 0.0), jnp.where(valid, whi[src], 0.0)], axis=-1).reshape(nt * tm, 2)
    pa, pb = _pair_tables()
    return jnp.asarray(pa)[grp], jnp.asarray(pb)[grp], nvalid.astype(i32), src.reshape(nt, 1, tm), dst, wgt


def _final_kernel(x1_ref, y_ref, mod_ref, g_ref, out_ref):
    m = mod_ref[0]
    x2 = x1_ref[0] + m[5:6] * y_ref[...]
    ms = jnp.mean(x2 * x2, axis=-1, keepdims=True)
    out_ref[0] = x2 * lax.rsqrt(ms + EPS) * g_ref[...]


def _final(x1, y, mod3, g):
    b, s, d = x1.shape
    tm = TOK_TILE
    nj = s // tm
    tok = pl.BlockSpec((1, tm, d), lambda i, j: (i, j, 0))
    return pl.pallas_call(
        _final_kernel,
        out_shape=jax.ShapeDtypeStruct((b, s, d), f32),
        grid=(b, nj),
        in_specs=[tok,
                  pl.BlockSpec((tm, d), lambda i, j: (i * nj + j, 0)),
                  pl.BlockSpec((1, MOD_CHUNKS, d), lambda i, j: (i, 0, 0)),
                  pl.BlockSpec((1, d), lambda i, j: (0, 0))],
        out_specs=tok,
        compiler_params=pltpu.CompilerParams(dimension_semantics=("arbitrary", "arbitrary"),
                                             vmem_limit_bytes=_vmem_limit(32 * 1024 * 1024)),
        name="final",
    )(x1, y, mod3, g)


def _pool_constants(tm):
    slab = V7X_MXU_DIM
    band = np.zeros((len(POOL_WINDOWS), slab, slab), np.float32)
    icnt = np.zeros((tm, POOL_WIDTH), np.float32)
    for gi, w in enumerate(POOL_WINDOWS):
        for t in range(slab):
            base, pos = (t // GRID_W) * GRID_W, t % GRID_W
            lo, hi = max(pos - w // 2, 0), min(pos + w // 2, GRID_W)
            band[gi, t, base + lo:base + hi] = 1.0
        for t in range(tm):
            pos = t % GRID_W
            cnt = min(pos + w // 2, GRID_W) - max(pos - w // 2, 0)
            icnt[t, gi * POOL_GROUP_DIM:(gi + 1) * POOL_GROUP_DIM] = 1.0 / cnt
    return jnp.asarray(band, bf16), jnp.asarray(icnt, f32)


def _block_diag(blocks):
    n = len(blocks)
    r, c = blocks[0].shape
    out = jnp.zeros((n * r, n * c), blocks[0].dtype)
    for i, blk in enumerate(blocks):
        out = out.at[i * r:(i + 1) * r, i * c:(i + 1) * c].set(blk)
    return out


def kernel(x, c, ctx, c_ctx, w_mod, b_mod, norm1_g, norm2_g, w_in, gla_a2_f, gla_ab_f, gla_a2_b, gla_ab_b,
           gla_onorm_g, pool_w, pool_scale, w_pool_br, w_gla_br, w_o, router_grp_w, router_grp_b,
           router_exp_w, router_exp_b, moe_w1, moe_w3, moe_w2, final_norm_g):
    b, s, d = x.shape
    depth = w_mod.shape[0]
    assert depth == 1 and d == D_MODEL and s % TOK_TILE == 0 and s % (2 * GLA_CHUNK) == 0
    assert TOK_TILE % V7X_MXU_DIM == 0 and V7X_MXU_DIM % GRID_W == 0
    t = b * s

    cond = jnp.concatenate([c, c_ctx[None, :], jnp.zeros((COND_PAD - 1, d), f32)], axis=0)
    mod3 = _adaln(cond, w_mod[0], b_mod[0][None, :]).reshape(b + COND_PAD, MOD_CHUNKS, d)

    w_in_b = w_in[0].astype(bf16)
    wqkv = w_in_b[:, _Q0:_G0]
    wkv = w_in_b[:, _K0:_G0]
    wugg = jnp.concatenate([w_in_b[:, _U0:_Q0], w_in_b[:, _G0:_LR0]], axis=1)
    wlr = jnp.pad(w_in_b[:, _LR0:_IN_END], ((0, 0), (0, V7X_LANES - 2 * GLA_GATE_RANK)))
    a2bd = jnp.pad(_block_diag([gla_a2_f[0], gla_a2_b[0]]).astype(bf16),
                   ((0, V7X_LANES - 2 * GLA_GATE_RANK), (0, 0)))
    ab = jnp.concatenate([gla_ab_f[0], gla_ab_b[0]])[None, :]
    n1 = norm1_g[0][None, :]
    n2 = norm2_g[0][None, :]

    s0f, s0b = _ctx_states(ctx, mod3, b, n1, wkv, wlr, a2bd, ab)
    q, k, v, la = _qkv(x, mod3, n1, wqkv, wlr, a2bd, ab)
    o = _gla(q, k, v, la, s0f, s0b)

    band, icnt = _pool_constants(TOK_TILE)
    pwbd = _block_diag([pool_w[0, gi] for gi in range(len(POOL_WINDOWS))]).astype(bf16)
    onorm = jnp.tile(gla_onorm_g[0], GLA_HEADS)[None, :]
    wr = jnp.zeros((d, V7X_LANES), f32)
    wr = wr.at[:, :N_GROUPS].set(router_grp_w[0]).at[:, V7X_SUBLANES:V7X_SUBLANES + N_EXPERTS].set(router_exp_w[0])
    br = jnp.zeros((1, V7X_LANES), f32)
    br = br.at[0, :N_GROUPS].set(router_grp_b[0]).at[0, V7X_SUBLANES:V7X_SUBLANES + N_EXPERTS].set(router_exp_b[0])
    x1, h2, ri, rw = _mix(x, o, mod3, n1, n2, wugg, band, icnt, pwbd, pool_scale[0][None, :],
                           w_pool_br[0].astype(bf16), onorm, w_gla_br[0].astype(bf16), w_o[0].astype(bf16),
                           wr.astype(bf16), br)

    plan = _dispatch_plan(ri[0], rw[0], rw[1], MOE_TILE)
    w13 = jnp.concatenate([moe_w1[0], moe_w3[0]], axis=-1).astype(bf16)
    y = _moe(*plan, h2, w13, moe_w2[0].astype(bf16))
    return _final(x1, y, mod3, final_norm_g[None, :])
```

```python
import jax
import jax.numpy as jnp
import numpy as np
from jax import lax
from jax.experimental import pallas as pl
from jax.experimental.pallas import tpu as pltpu

f32 = jnp.float32
bf16 = jnp.bfloat16
i32 = jnp.int32
u32 = jnp.uint32

D_MODEL = 1024
GRID_W = 64
POOL_WINDOWS = (2, 4, 8, 16)
POOL_GROUP_DIM = 128
POOL_WIDTH = len(POOL_WINDOWS) * POOL_GROUP_DIM
GLA_HEADS = 4
GLA_DK = 512
GLA_DV = 1024
GLA_HK = GLA_DK // GLA_HEADS
GLA_HV = GLA_DV // GLA_HEADS
GLA_GATE_RANK = 16
GLA_GATE_NORMALIZER = 16.0
N_GROUPS = 4
EXPERTS_PER_GROUP = 8
N_EXPERTS = N_GROUPS * EXPERTS_PER_GROUP
D_EXPERT = 512
MOD_CHUNKS = 6
EPS = 1e-6

_U0, _Q0, _K0, _V0, _G0, _GT0, _LR0, _IN_END = 0, 512, 1024, 1536, 2560, 3584, 5632, 5664

V7X_LANES = 128
V7X_SUBLANES = 8
V7X_MXU_DIM = 256
V7X_SCOPED_VMEM_CAP = 60000 * 1024

GLA_CHUNK = 128
GLA_HALF = GLA_CHUNK // 2
TOK_TILE = 512
MOE_TILE = 128
COND_PAD = 8
N_PAIRS = N_GROUPS * (EXPERTS_PER_GROUP * (EXPERTS_PER_GROUP - 1) // 2)


def _vmem_limit(nbytes):
    return int(min(max(nbytes, 16 * 1024 * 1024), V7X_SCOPED_VMEM_CAP))


def _dot(a, b):
    return jnp.dot(a, b, preferred_element_type=f32)


def _dot_nt(a, b):
    return lax.dot_general(a, b, (((1,), (1,)), ((), ())), preferred_element_type=f32)


def _dot_tn(a, b):
    return lax.dot_general(a, b, (((0,), (0,)), ((), ())), preferred_element_type=f32)


def _sigmoid(x):
    return 1.0 / (1.0 + jnp.exp(-x))


def _silu(x):
    return x * _sigmoid(x)


def _log_sigmoid(z):
    return jnp.minimum(z, 0.0) - jnp.log1p(jnp.exp(-jnp.abs(z)))


def _norm_mod(x, g, scale, shift):
    ms = jnp.mean(x * x, axis=-1, keepdims=True)
    return (x * lax.rsqrt(ms + EPS) * g) * (1.0 + scale) + shift


def _tri_cumsum(tri, la):
    hi = la.astype(bf16)
    lo = (la - hi.astype(f32)).astype(bf16)
    return _dot(tri, hi) + _dot(tri, lo)


ROW_TILE = V7X_SUBLANES


def _rows_to_tiles(ref, val):
    n = val.shape[0]
    for a in range(ROW_TILE):
        ref[pl.ds(a, n, stride=ROW_TILE), :] = val[:, a * V7X_LANES:(a + 1) * V7X_LANES]


def _tiles_to_rows(ref):
    n = ref.shape[0] // ROW_TILE
    return jnp.concatenate([ref[pl.ds(a, n, stride=ROW_TILE), :] for a in range(ROW_TILE)], axis=1)


def _log_decays(hx, wlr_ref, a2_ref, ab_ref):
    lr = _dot(hx, wlr_ref[...])
    z = _dot(lr.astype(bf16), a2_ref[...]) + ab_ref[...]
    return _log_sigmoid(z) * (1.0 / GLA_GATE_NORMALIZER)


def _adaln_kernel(c_ref, w_ref, b_ref, o_ref):
    c = c_ref[...]
    o_ref[...] = _dot(_silu(c).astype(bf16), w_ref[...].astype(bf16)) + b_ref[...]


def _adaln(cond, w_mod, b_mod):
    rows, d = cond.shape
    n = w_mod.shape[1]
    tn = 512
    return pl.pallas_call(
        _adaln_kernel,
        out_shape=jax.ShapeDtypeStruct((rows, n), f32),
        grid=(n // tn,),
        in_specs=[pl.BlockSpec((rows, d), lambda j: (0, 0)),
                  pl.BlockSpec((d, tn), lambda j: (0, j)),
                  pl.BlockSpec((1, tn), lambda j: (0, j))],
        out_specs=pl.BlockSpec((rows, tn), lambda j: (0, j)),
        compiler_params=pltpu.CompilerParams(dimension_semantics=("arbitrary",)),
        name="adaln",
    )(cond, w_mod, b_mod)


def _ctx_kernel(ctx_ref, mod_ref, g_ref, wkv_ref, wlr_ref, a2_ref, ab_ref, sf_ref, sb_ref):
    x = ctx_ref[0]
    n = x.shape[0]
    m = mod_ref[0]
    hc = _norm_mod(x, g_ref[...], m[1:2], m[0:1]).astype(bf16)
    kv = _dot(hc, wkv_ref[...])
    la = _log_decays(hc, wlr_ref, a2_ref, ab_ref)
    row = lax.broadcasted_iota(i32, (n, n), 0)
    col = lax.broadcasted_iota(i32, (n, n), 1)
    lower = jnp.where(row >= col, 1.0, 0.0).astype(bf16)
    upper = jnp.where(col >= row, 1.0, 0.0).astype(bf16)
    cum_f = _tri_cumsum(lower, la[:, :GLA_DK])
    cum_b = _tri_cumsum(upper, la[:, GLA_DK:])
    k = kv[:, :GLA_DK]
    v = kv[:, GLA_DK:].astype(bf16)
    kr_f = (k * jnp.exp(cum_f[n - 1:n] - cum_f)).astype(bf16)
    kr_b = (k * jnp.exp(cum_b[0:1] - cum_b)).astype(bf16)
    for h in range(GLA_HEADS):
        ks = slice(h * GLA_HK, (h + 1) * GLA_HK)
        vs = slice(h * GLA_HV, (h + 1) * GLA_HV)
        sf_ref[0, h] = _dot_tn(kr_f[:, ks], v[:, vs])
        sb_ref[0, h] = _dot_tn(kr_b[:, ks], v[:, vs])


def _ctx_states(ctx, mod3, ctx_row, g, wkv, wlr, a2bd, ab):
    b, n, d = ctx.shape
    st = jax.ShapeDtypeStruct((b, GLA_HEADS, GLA_HK, GLA_HV), f32)
    const = lambda i: (0, 0)
    st_spec = pl.BlockSpec((1, GLA_HEADS, GLA_HK, GLA_HV), lambda i: (i, 0, 0, 0))
    return pl.pallas_call(
        _ctx_kernel,
        out_shape=(st, st),
        grid=(b,),
        in_specs=[pl.BlockSpec((1, n, d), lambda i: (i, 0, 0)),
                  pl.BlockSpec((1, MOD_CHUNKS, d), lambda i: (ctx_row, 0, 0)),
                  pl.BlockSpec((1, d), const),
                  pl.BlockSpec(wkv.shape, const),
                  pl.BlockSpec(wlr.shape, const),
                  pl.BlockSpec(a2bd.shape, const),
                  pl.BlockSpec(ab.shape, const)],
        out_specs=(st_spec, st_spec),
        compiler_params=pltpu.CompilerParams(dimension_semantics=("arbitrary",),
                                             vmem_limit_bytes=_vmem_limit(32 * 1024 * 1024)),
        name="ctx_states",
    )(ctx, mod3, g, wkv, wlr, a2bd, ab)


def _qkv_kernel(x_ref, mod_ref, g_ref, wqkv_ref, wlr_ref, a2_ref, ab_ref, q_ref, k_ref, v_ref, la_ref):
    m = mod_ref[0]
    hx = _norm_mod(x_ref[0], g_ref[...], m[1:2], m[0:1]).astype(bf16)
    p = _dot(hx, wqkv_ref[...])
    q_ref[0] = p[:, :GLA_DK].astype(bf16)
    k_ref[0] = p[:, GLA_DK:2 * GLA_DK].astype(bf16)
    v_ref[0] = p[:, 2 * GLA_DK:].astype(bf16)
    la_ref[0] = _log_decays(hx, wlr_ref, a2_ref, ab_ref)


def _qkv(x, mod3, g, wqkv, wlr, a2bd, ab):
    b, s, d = x.shape
    tm = TOK_TILE
    const = lambda i, j: (0, 0)
    tok = lambda w: pl.BlockSpec((1, tm, w), lambda i, j: (i, j, 0))
    return pl.pallas_call(
        _qkv_kernel,
        out_shape=(jax.ShapeDtypeStruct((b, s, GLA_DK), bf16), jax.ShapeDtypeStruct((b, s, GLA_DK), bf16),
                   jax.ShapeDtypeStruct((b, s, GLA_DV), bf16), jax.ShapeDtypeStruct((b, s, 2 * GLA_DK), f32)),
        grid=(b, s // tm),
        in_specs=[tok(d),
                  pl.BlockSpec((1, MOD_CHUNKS, d), lambda i, j: (i, 0, 0)),
                  pl.BlockSpec((1, d), const),
                  pl.BlockSpec(wqkv.shape, const),
                  pl.BlockSpec(wlr.shape, const),
                  pl.BlockSpec(a2bd.shape, const),
                  pl.BlockSpec(ab.shape, const)],
        out_specs=(tok(GLA_DK), tok(GLA_DK), tok(GLA_DV), tok(2 * GLA_DK)),
        compiler_params=pltpu.CompilerParams(dimension_semantics=("arbitrary", "arbitrary"),
                                             vmem_limit_bytes=_vmem_limit(48 * 1024 * 1024)),
        name="qkv",
    )(x, mod3, g, wqkv, wlr, a2bd, ab)


def _gla_kernel(q_ref, k_ref, v_ref, la_ref, s0f_ref, s0b_ref, o_ref, s_sc, oacc):
    seq = q_ref.shape[1]
    nc = seq // GLA_CHUNK
    c_len = GLA_CHUNK
    s_sc[0:GLA_HEADS] = s0f_ref[0]
    s_sc[GLA_HEADS:2 * GLA_HEADS] = s0b_ref[0]
    row = lax.broadcasted_iota(i32, (c_len, c_len), 0)
    col = lax.broadcasted_iota(i32, (c_len, c_len), 1)
    lower = row >= col
    upper = col >= row
    tri_l = jnp.where(lower, 1.0, 0.0).astype(bf16)
    tri_u = jnp.where(upper, 1.0, 0.0).astype(bf16)
    scale = GLA_HK ** -0.5

    def step(c_fwd, c_bwd, second_visit):
        prep = []
        tots = []
        for d, c, tri, r_row, t_row in ((0, c_fwd, tri_l, GLA_HALF - 1, c_len - 1), (1, c_bwd, tri_u, GLA_HALF, 0)):
            rows = pl.ds(pl.multiple_of(c * c_len, c_len), c_len)
            cum = _tri_cumsum(tri, la_ref[0, rows, d * GLA_DK:(d + 1) * GLA_DK])
            r = cum[r_row:r_row + 1]
            tot = cum[t_row:t_row + 1]
            qd = q_ref[0, rows, :].astype(f32) * (jnp.exp(cum - r) * scale)
            ki = k_ref[0, rows, :].astype(f32) * jnp.exp(r - cum)
            qs = (qd * jnp.exp(r)).astype(bf16)
            kr = (ki * jnp.exp(tot - r)).astype(bf16)
            prep.append((rows, qd.astype(bf16), qs, ki.astype(bf16), kr, v_ref[0, rows, :]))
            tots.extend(tot[:, h * GLA_HK:(h + 1) * GLA_HK] for h in range(GLA_HEADS))
        pad = jnp.zeros((c_len - 2 * GLA_HEADS, GLA_HK), f32)
        dec_cols = jnp.exp(jnp.concatenate(tots + [pad], axis=0).T)
        for d, mask in ((0, lower), (1, upper)):
            rows, qd, qs, ki, kr, vv = prep[d]
            for h in range(GLA_HEADS):
                j = d * GLA_HEADS + h
                ks = slice(h * GLA_HK, (h + 1) * GLA_HK)
                vs = slice(h * GLA_HV, (h + 1) * GLA_HV)
                a = jnp.where(mask, _dot_nt(qd[:, ks], ki[:, ks]), 0.0).astype(bf16)
                s_old = s_sc[j]
                o = _dot(jnp.concatenate([a, qs[:, ks]], axis=1),
                         jnp.concatenate([vv[:, vs], s_old.astype(bf16)], axis=0))
                s_sc[j] = dec_cols[:, j:j + 1] * s_old + _dot_tn(kr[:, ks], vv[:, vs])
                if second_visit:
                    o_ref[0, rows, vs] = (oacc[rows, vs] + o).astype(o_ref.dtype)
                else:
                    oacc[rows, vs] = o

    def first(i, carry):
        step(i, nc - 1 - i, False)
        return carry

    def second(i, carry):
        step(i, nc - 1 - i, True)
        return carry

    lax.fori_loop(0, nc // 2, first, 0)
    lax.fori_loop(nc // 2, nc, second, 0)


def _gla(q, k, v, la, s0f, s0b):
    b, s, _ = q.shape
    tok = lambda w: pl.BlockSpec((1, s, w), lambda i: (i, 0, 0))
    st_spec = pl.BlockSpec((1, GLA_HEADS, GLA_HK, GLA_HV), lambda i: (i, 0, 0, 0))
    return pl.pallas_call(
        _gla_kernel,
        out_shape=jax.ShapeDtypeStruct((b, s, GLA_DV), bf16),
        grid=(b,),
        in_specs=[tok(GLA_DK), tok(GLA_DK), tok(GLA_DV), tok(2 * GLA_DK), st_spec, st_spec],
        out_specs=tok(GLA_DV),
        scratch_shapes=[pltpu.VMEM((2 * GLA_HEADS, GLA_HK, GLA_HV), f32), pltpu.VMEM((s, GLA_DV), f32)],
        compiler_params=pltpu.CompilerParams(dimension_semantics=("arbitrary",),
                                             vmem_limit_bytes=V7X_SCOPED_VMEM_CAP),
        name="gla",
    )(q, k, v, la, s0f, s0b)


def _mix_kernel(x_ref, o_ref, mod_ref, n1_ref, n2_ref, wugg_ref, band_ref, icnt_ref, pwbd_ref, pscale_ref,
                wpool_ref, onorm_ref, wgla_ref, wo_ref, wr_ref, br_ref,
                x1_ref, h2_ref, ri_ref, rw_ref):
    x = x_ref[0]
    tm = x.shape[0]
    m = mod_ref[0]
    hx = _norm_mod(x, n1_ref[...], m[1:2], m[0:1]).astype(bf16)
    p = _dot(hx, wugg_ref[...])
    u = p[:, :POOL_WIDTH]
    g = p[:, POOL_WIDTH:POOL_WIDTH + GLA_DV]
    gates = p[:, POOL_WIDTH + GLA_DV:]

    ub = u.astype(bf16)
    slab = band_ref.shape[1]
    sums = []
    for s in range(tm // slab):
        rs = slice(s * slab, (s + 1) * slab)
        sums.append(jnp.concatenate(
            [_dot(band_ref[gi], ub[rs, gi * POOL_GROUP_DIM:(gi + 1) * POOL_GROUP_DIM])
             for gi in range(len(POOL_WINDOWS))], axis=1))
    pooled = jnp.concatenate(sums, axis=0) * icnt_ref[...] - u
    y_pool = _dot(pooled.astype(bf16), pwbd_ref[...]) * pscale_ref[...]
    y_pool = _dot(y_pool.astype(bf16), wpool_ref[...])

    o = o_ref[0].astype(f32)
    normed = []
    for h in range(GLA_HEADS):
        oh = o[:, h * GLA_HV:(h + 1) * GLA_HV]
        normed.append(oh * lax.rsqrt(jnp.mean(oh * oh, axis=-1, keepdims=True) + EPS))
    on = jnp.concatenate(normed, axis=1) * onorm_ref[...]
    y_gla = _dot((on * _silu(g)).astype(bf16), wgla_ref[...])

    mixed = _sigmoid(gates[:, :D_MODEL]) * y_pool + _sigmoid(gates[:, D_MODEL:]) * y_gla
    x1 = x + m[2:3] * _dot(mixed.astype(bf16), wo_ref[...])
    x1_ref[0] = x1

    h2f = _norm_mod(x1, n2_ref[...], m[4:5], m[3:4])
    _rows_to_tiles(h2_ref, h2f)
    h2 = h2f.astype(bf16)

    lt = (_dot(h2, wr_ref[...]) + br_ref[...]).T
    g0, g1, g2, g3 = lt[0:1], lt[1:2], lt[2:3], lt[3:4]
    gmax = jnp.maximum(jnp.maximum(g0, g1), jnp.maximum(g2, g3))
    gidx = jnp.where(g0 == gmax, 0, jnp.where(g1 == gmax, 1, jnp.where(g2 == gmax, 2, 3))).astype(i32)
    p_grp = 1.0 / (jnp.exp(g0 - gmax) + jnp.exp(g1 - gmax) + jnp.exp(g2 - gmax) + jnp.exp(g3 - gmax))
    e0 = V7X_SUBLANES
    ing = lt[e0 + 3 * EXPERTS_PER_GROUP:e0 + 4 * EXPERTS_PER_GROUP]
    for gi in (2, 1, 0):
        ing = jnp.where(gidx == gi, lt[e0 + gi * EXPERTS_PER_GROUP:e0 + (gi + 1) * EXPERTS_PER_GROUP], ing)
    rid = lax.broadcasted_iota(i32, ing.shape, 0)
    v1 = jnp.max(ing, axis=0, keepdims=True)
    i1 = jnp.min(jnp.where(ing == v1, rid, EXPERTS_PER_GROUP), axis=0, keepdims=True)
    rest = jnp.where(rid == i1, -jnp.inf, ing)
    v2 = jnp.max(rest, axis=0, keepdims=True)
    i2 = jnp.min(jnp.where(rest == v2, rid, EXPERTS_PER_GROUP), axis=0, keepdims=True)
    t = jnp.exp(v2 - v1)
    w1 = p_grp / (1.0 + t)
    w2 = p_grp * t / (1.0 + t)
    first_low = i1 < i2
    ilo = jnp.minimum(i1, i2)
    ihi = jnp.maximum(i1, i2)
    pair = ilo * (EXPERTS_PER_GROUP - 1) - ((ilo * (ilo - 1)) >> 1) + (ihi - ilo - 1)
    pid = gidx * (N_PAIRS // N_GROUPS) + pair
    ri_ref[...] = jnp.where(rid == 0, pid, 0)
    rw_ref[...] = jnp.where(rid == 0, jnp.where(first_low, w1, w2),
                            jnp.where(rid == 1, jnp.where(first_low, w2, w1), 0.0))


def _mix(x, o, mod3, n1, n2, wugg, band, icnt, pwbd, pscale, wpool, onorm, wgla, wo, wr, br):
    b, s, d = x.shape
    tm = TOK_TILE
    nj = s // tm
    t = b * s
    const2 = lambda i, j: (0, 0)
    const3 = lambda i, j: (0, 0, 0)
    tok = lambda w: pl.BlockSpec((1, tm, w), lambda i, j: (i, j, 0))
    lane = pl.BlockSpec((V7X_SUBLANES, tm), lambda i, j: (0, i * nj + j))
    return pl.pallas_call(
        _mix_kernel,
        out_shape=(jax.ShapeDtypeStruct((b, s, d), f32), jax.ShapeDtypeStruct((t * ROW_TILE, V7X_LANES), f32),
                   jax.ShapeDtypeStruct((V7X_SUBLANES, t), i32), jax.ShapeDtypeStruct((V7X_SUBLANES, t), f32)),
        grid=(b, nj),
        in_specs=[tok(d), tok(GLA_DV),
                  pl.BlockSpec((1, MOD_CHUNKS, d), lambda i, j: (i, 0, 0)),
                  pl.BlockSpec((1, d), const2), pl.BlockSpec((1, d), const2),
                  pl.BlockSpec(wugg.shape, const2),
                  pl.BlockSpec(band.shape, const3),
                  pl.BlockSpec(icnt.shape, const2),
                  pl.BlockSpec(pwbd.shape, const2),
                  pl.BlockSpec(pscale.shape, const2),
                  pl.BlockSpec(wpool.shape, const2),
                  pl.BlockSpec(onorm.shape, const2),
                  pl.BlockSpec(wgla.shape, const2),
                  pl.BlockSpec(wo.shape, const2),
                  pl.BlockSpec(wr.shape, const2),
                  pl.BlockSpec(br.shape, const2)],
        out_specs=(tok(d), pl.BlockSpec((tm * ROW_TILE, V7X_LANES), lambda i, j: (i * nj + j, 0)), lane, lane),
        compiler_params=pltpu.CompilerParams(dimension_semantics=("arbitrary", "arbitrary"),
                                             vmem_limit_bytes=V7X_SCOPED_VMEM_CAP),
        name="mix",
    )(x, o, mod3, n1, n2, wugg, band, icnt, pwbd, pscale, wpool, onorm, wgla, wo, wr, br)


def _moe_kernel(ta_ref, tb_ref, nv_ref, src0_ref, srcn0_ref, srcn1_ref, dstp0_ref, dstp1_ref, dstl_ref,
                wgt0_ref, wgt1_ref, h2_ref, w13a0_ref, w2a0_ref, w13b0_ref, w2b0_ref,
                w13a1_ref, w2a1_ref, w13b1_ref, w2b1_ref, y_ref, xbuf, ybuf, gsem, ssem):
    del ta_ref, tb_ref
    j = pl.program_id(0)
    last = pl.num_programs(0) - 1
    nv = nv_ref[0]
    tm = xbuf.shape[1] // ROW_TILE
    spare0 = y_ref.shape[0] - 2 * tm * ROW_TILE

    def token_tile(ref, off):
        return ref.at[pl.ds(pl.multiple_of(off, ROW_TILE), ROW_TILE)]

    def gather_row(off, s, r8):
        return pltpu.make_async_copy(token_tile(h2_ref, off), xbuf.at[s, pl.ds(r8, ROW_TILE)], gsem.at[s])

    def scatter_row(off, s, r8):
        return pltpu.make_async_copy(ybuf.at[s, pl.ds(r8, ROW_TILE)], token_tile(y_ref, off), ssem.at[s])

    def start_rows(make, idx_ref):
        def body(r, carry):
            make(idx_ref[0, 0, r], pl.multiple_of(r * ROW_TILE, ROW_TILE)).start()
            return carry
        lax.fori_loop(0, tm, body, 0, unroll=8)

    def gather_tile(s):
        return pltpu.make_async_copy(h2_ref.at[pl.ds(0, tm * ROW_TILE)], xbuf.at[s], gsem.at[s])

    def scatter_tile(s):
        return pltpu.make_async_copy(ybuf.at[s], y_ref.at[pl.ds(0, tm * ROW_TILE)], ssem.at[s])

    @pl.when(j == 0)
    def _():
        ybuf[...] = jnp.zeros(ybuf.shape, ybuf.dtype)
        for s in range(2):
            spare = pltpu.make_async_copy(ybuf.at[0], y_ref.at[pl.ds(spare0 + s * tm * ROW_TILE, tm * ROW_TILE)],
                                          ssem.at[0])
            spare.start()
            spare.wait()

        @pl.when(nv > 0)
        def _():
            start_rows(lambda off, r8: gather_row(off, 0, r8), src0_ref)

    n_pieces = 12
    per_piece = -(-tm // n_pieces)

    def tile_body(s, tile, srcn_ref, dstp_ref, wgt_ref, w13a_ref, w2a_ref, w13b_ref, w2b_ref):
        @pl.when(tile < nv)
        def _():
            gather_tile(s).wait()

            @pl.when(tile >= 1)
            def _():
                scatter_tile(s).wait()

            piece = [0]

            def issue_rows():
                lo = piece[0] * per_piece
                rows = range(lo, min(lo + per_piece, tm))
                src_rows = [srcn_ref[0, 0, r] for r in rows]
                dst_rows = [dstp_ref[0, 0, r] for r in rows]
                for r, sr, dr in zip(rows, src_rows, dst_rows):
                    gather_row(sr, 1 - s, r * ROW_TILE).start()
                    scatter_row(dr, 1 - s, r * ROW_TILE).start()
                piece[0] += 1

            xt = _tiles_to_rows(xbuf.at[s]).astype(bf16)
            w = wgt_ref[...]
            half = D_EXPERT // 2
            y = None
            for e, (w13_ref, w2_ref) in enumerate(((w13a_ref, w2a_ref), (w13b_ref, w2b_ref))):
                acc = None
                for c in range(2):
                    h1 = _dot(xt, w13_ref[0, :, c * half:(c + 1) * half])
                    issue_rows()
                    h3 = _dot(xt, w13_ref[0, :, D_EXPERT + c * half:D_EXPERT + (c + 1) * half])
                    issue_rows()
                    part = _dot((_silu(h1) * h3).astype(bf16), w2_ref[0, c * half:(c + 1) * half, :])
                    issue_rows()
                    acc = part if acc is None else acc + part
                y = w[:, e:e + 1] * acc if y is None else y + w[:, e:e + 1] * acc
            assert piece[0] == n_pieces
            _rows_to_tiles(ybuf.at[s], y)

    tile_body(0, 2 * j, srcn0_ref, dstp0_ref, wgt0_ref, w13a0_ref, w2a0_ref, w13b0_ref, w2b0_ref)
    tile_body(1, 2 * j + 1, srcn1_ref, dstp1_ref, wgt1_ref, w13a1_ref, w2a1_ref, w13b1_ref, w2b1_ref)

    @pl.when(jnp.logical_and(j == last, nv > 0))
    def _():
        sl = (nv - 1) % 2
        scatter_tile(1 - sl).wait()
        start_rows(lambda off, r8: scatter_row(off, sl, r8), dstl_ref)
        scatter_tile(sl).wait()
        gather_tile(1 - sl).wait()


def _moe(tile_a, tile_b, nvalid, src, dst, wgt, h2, w13, w2):
    nt, _, tm = src.shape
    assert nt % 2 == 0 and dst.shape[0] == nt + 1
    rows, lanes = h2.shape
    d = ROW_TILE * lanes
    idx_blk = (1, 1, tm)
    smem_idx = lambda fn: pl.BlockSpec(idx_blk, fn, memory_space=pltpu.SMEM)
    w13_blk = (1, d, 2 * D_EXPERT)
    w2_blk = (1, D_EXPERT, d)

    def weights(s):
        return [pl.BlockSpec(w13_blk, lambda j, ta, tb, nv: (ta[2 * j + s], 0, 0)),
                pl.BlockSpec(w2_blk, lambda j, ta, tb, nv: (ta[2 * j + s], 0, 0)),
                pl.BlockSpec(w13_blk, lambda j, ta, tb, nv: (tb[2 * j + s], 0, 0)),
                pl.BlockSpec(w2_blk, lambda j, ta, tb, nv: (tb[2 * j + s], 0, 0))]

    grid_spec = pltpu.PrefetchScalarGridSpec(
        num_scalar_prefetch=3,
        grid=(nt // 2,),
        in_specs=[smem_idx(lambda j, ta, tb, nv: (0, 0, 0)),
                  smem_idx(lambda j, ta, tb, nv: (jnp.minimum(2 * j + 1, nt - 1), 0, 0)),
                  smem_idx(lambda j, ta, tb, nv: (jnp.minimum(2 * j + 2, nt - 1), 0, 0)),
                  smem_idx(lambda j, ta, tb, nv: (jnp.where(j == 0, nt, 2 * j - 1), 0, 0)),
                  smem_idx(lambda j, ta, tb, nv: (2 * j, 0, 0)),
                  smem_idx(lambda j, ta, tb, nv: (jnp.maximum(nv[0] - 1, 0), 0, 0)),
                  pl.BlockSpec((tm, 2), lambda j, ta, tb, nv: (2 * j, 0)),
                  pl.BlockSpec((tm, 2), lambda j, ta, tb, nv: (2 * j + 1, 0)),
                  pl.BlockSpec(memory_space=pl.ANY)] + weights(0) + weights(1),
        out_specs=pl.BlockSpec(memory_space=pl.ANY),
        scratch_shapes=[pltpu.VMEM((2, tm * ROW_TILE, lanes), f32), pltpu.VMEM((2, tm * ROW_TILE, lanes), f32),
                        pltpu.SemaphoreType.DMA((2,)), pltpu.SemaphoreType.DMA((2,))],
    )
    return pl.pallas_call(
        _moe_kernel,
        out_shape=jax.ShapeDtypeStruct((rows + 2 * tm * ROW_TILE, lanes), f32),
        grid_spec=grid_spec,
        compiler_params=pltpu.CompilerParams(dimension_semantics=("arbitrary",),
                                             vmem_limit_bytes=_vmem_limit(48 * 1024 * 1024)),
        name="moe",
    )(tile_a, tile_b, nvalid, src, src, src, dst, dst, dst, wgt, wgt, h2, w13, w2, w13, w2, w13, w2, w13, w2)


def _pair_tables():
    a, b = [], []
    for g in range(N_GROUPS):
        for i in range(EXPERTS_PER_GROUP):
            for j in range(i + 1, EXPERTS_PER_GROUP):
                a.append(g * EXPERTS_PER_GROUP + i)
                b.append(g * EXPERTS_PER_GROUP + j)
    return np.asarray(a, np.int32), np.asarray(b, np.int32)


def _dispatch_plan(pid, wlo, whi, tm):
    t = pid.shape[0]
    nt = 2 * (-(-((t + N_PAIRS * (tm - 1)) // tm) // 2))
    order = jnp.argsort(pid, stable=True).astype(i32)
    starts = jnp.searchsorted(pid[order], jnp.arange(N_PAIRS + 1, dtype=i32), side="left").astype(i32)
    counts = starts[1:] - starts[:-1]
    tiles = (counts + tm - 1) // tm
    tile_end = jnp.cumsum(tiles)
    tile_start = tile_end - tiles
    nvalid = tile_end[-1:]
    tile = jnp.arange(nt, dtype=i32)
    grp = jnp.minimum(jnp.searchsorted(tile_end, tile, side="right"), N_PAIRS - 1).astype(i32)
    rank = (tile - tile_start[grp])[:, None] * tm + jnp.arange(tm, dtype=i32)[None, :]
    valid = jnp.logical_and(rank < counts[grp][:, None], (tile < nvalid[0])[:, None])
    tok = order[jnp.clip(starts[grp][:, None] + rank, 0, t - 1)]
    src = jnp.where(valid, tok, 0)
    row = jnp.arange(tm, dtype=i32)[None, :]
    dst = jnp.where(valid, tok, t + (tile % 2)[:, None] * tm + row)
    dst = jnp.concatenate([dst, t + tm + row], axis=0).reshape(nt + 1, 1, tm)
    wgt = jnp.stack([jnp.where(valid, wlo[src], 0.0), jnp.where(valid, whi[src], 0.0)], axis=-1).reshape(nt * tm, 2)
    pa, pb = _pair_tables()
    return (jnp.asarray(pa)[grp], jnp.asarray(pb)[grp], nvalid.astype(i32),
            src.reshape(nt, 1, tm) * ROW_TILE, dst * ROW_TILE, wgt)


def _final_kernel(x1_ref, y_ref, mod_ref, g_ref, out_ref):
    m = mod_ref[0]
    x2 = x1_ref[0] + m[5:6] * _tiles_to_rows(y_ref)
    ms = jnp.mean(x2 * x2, axis=-1, keepdims=True)
    out_ref[0] = x2 * lax.rsqrt(ms + EPS) * g_ref[...]


def _final(x1, y, mod3, g):
    b, s, d = x1.shape
    tm = TOK_TILE
    nj = s // tm
    tok = pl.BlockSpec((1, tm, d), lambda i, j: (i, j, 0))
    return pl.pallas_call(
        _final_kernel,
        out_shape=jax.ShapeDtypeStruct((b, s, d), f32),
        grid=(b, nj),
        in_specs=[tok,
                  pl.BlockSpec((tm * ROW_TILE, y.shape[1]), lambda i, j: (i * nj + j, 0)),
                  pl.BlockSpec((1, MOD_CHUNKS, d), lambda i, j: (i, 0, 0)),
                  pl.BlockSpec((1, d), lambda i, j: (0, 0))],
        out_specs=tok,
        compiler_params=pltpu.CompilerParams(dimension_semantics=("arbitrary", "arbitrary"),
                                             vmem_limit_bytes=_vmem_limit(32 * 1024 * 1024)),
        name="final",
    )(x1, y, mod3, g)


def _pool_constants(tm):
    slab = V7X_MXU_DIM
    band = np.zeros((len(POOL_WINDOWS), slab, slab), np.float32)
    icnt = np.zeros((tm, POOL_WIDTH), np.float32)
    for gi, w in enumerate(POOL_WINDOWS):
        for t in range(slab):
            base, pos = (t // GRID_W) * GRID_W, t % GRID_W
            lo, hi = max(pos - w // 2, 0), min(pos + w // 2, GRID_W)
            band[gi, t, base + lo:base + hi] = 1.0
        for t in range(tm):
            pos = t % GRID_W
            cnt = min(pos + w // 2, GRID_W) - max(pos - w // 2, 0)
            icnt[t, gi * POOL_GROUP_DIM:(gi + 1) * POOL_GROUP_DIM] = 1.0 / cnt
    return jnp.asarray(band, bf16), jnp.asarray(icnt, f32)


def _block_diag(blocks):
    n = len(blocks)
    r, c = blocks[0].shape
    out = jnp.zeros((n * r, n * c), blocks[0].dtype)
    for i, blk in enumerate(blocks):
        out = out.at[i * r:(i + 1) * r, i * c:(i + 1) * c].set(blk)
    return out


def kernel(x, c, ctx, c_ctx, w_mod, b_mod, norm1_g, norm2_g, w_in, gla_a2_f, gla_ab_f, gla_a2_b, gla_ab_b,
           gla_onorm_g, pool_w, pool_scale, w_pool_br, w_gla_br, w_o, router_grp_w, router_grp_b,
           router_exp_w, router_exp_b, moe_w1, moe_w3, moe_w2, final_norm_g):
    b, s, d = x.shape
    depth = w_mod.shape[0]
    assert depth == 1 and d == D_MODEL and s % TOK_TILE == 0 and s % (2 * GLA_CHUNK) == 0
    assert TOK_TILE % V7X_MXU_DIM == 0 and V7X_MXU_DIM % GRID_W == 0
    t = b * s

    cond = jnp.concatenate([c, c_ctx[None, :], jnp.zeros((COND_PAD - 1, d), f32)], axis=0)
    mod3 = _adaln(cond, w_mod[0], b_mod[0][None, :]).reshape(b + COND_PAD, MOD_CHUNKS, d)

    w_in_b = w_in[0].astype(bf16)
    wqkv = w_in_b[:, _Q0:_G0]
    wkv = w_in_b[:, _K0:_G0]
    wugg = jnp.concatenate([w_in_b[:, _U0:_Q0], w_in_b[:, _G0:_LR0]], axis=1)
    wlr = jnp.pad(w_in_b[:, _LR0:_IN_END], ((0, 0), (0, V7X_LANES - 2 * GLA_GATE_RANK)))
    a2bd = jnp.pad(_block_diag([gla_a2_f[0], gla_a2_b[0]]).astype(bf16),
                   ((0, V7X_LANES - 2 * GLA_GATE_RANK), (0, 0)))
    ab = jnp.concatenate([gla_ab_f[0], gla_ab_b[0]])[None, :]
    n1 = norm1_g[0][None, :]
    n2 = norm2_g[0][None, :]

    s0f, s0b = _ctx_states(ctx, mod3, b, n1, wkv, wlr, a2bd, ab)
    q, k, v, la = _qkv(x, mod3, n1, wqkv, wlr, a2bd, ab)
    o = _gla(q, k, v, la, s0f, s0b)

    band, icnt = _pool_constants(TOK_TILE)
    pwbd = _block_diag([pool_w[0, gi] for gi in range(len(POOL_WINDOWS))]).astype(bf16)
    onorm = jnp.tile(gla_onorm_g[0], GLA_HEADS)[None, :]
    wr = jnp.zeros((d, V7X_LANES), f32)
    wr = wr.at[:, :N_GROUPS].set(router_grp_w[0]).at[:, V7X_SUBLANES:V7X_SUBLANES + N_EXPERTS].set(router_exp_w[0])
    br = jnp.zeros((1, V7X_LANES), f32)
    br = br.at[0, :N_GROUPS].set(router_grp_b[0]).at[0, V7X_SUBLANES:V7X_SUBLANES + N_EXPERTS].set(router_exp_b[0])
    x1, h2, ri, rw = _mix(x, o, mod3, n1, n2, wugg, band, icnt, pwbd, pool_scale[0][None, :],
                           w_pool_br[0].astype(bf16), onorm, w_gla_br[0].astype(bf16), w_o[0].astype(bf16),
                           wr.astype(bf16), br)

    plan = _dispatch_plan(ri[0], rw[0], rw[1], MOE_TILE)
    w13 = jnp.concatenate([moe_w1[0], moe_w3[0]], axis=-1).astype(bf16)
    y = _moe(*plan, h2, w13, moe_w2[0].astype(bf16))
    return _final(x1, y, mod3, final_norm_g[None, :])
```

```python
import jax
import jax.numpy as jnp
import numpy as np
from jax import lax
from jax.experimental import pallas as pl
from jax.experimental.pallas import tpu as pltpu

f32 = jnp.float32
bf16 = jnp.bfloat16
i32 = jnp.int32
u32 = jnp.uint32

D_MODEL = 1024
GRID_W = 64
POOL_WINDOWS = (2, 4, 8, 16)
POOL_GROUP_DIM = 128
POOL_WIDTH = len(POOL_WINDOWS) * POOL_GROUP_DIM
GLA_HEADS = 4
GLA_DK = 512
GLA_DV = 1024
GLA_HK = GLA_DK // GLA_HEADS
GLA_HV = GLA_DV // GLA_HEADS
GLA_GATE_RANK = 16
GLA_GATE_NORMALIZER = 16.0
N_GROUPS = 4
EXPERTS_PER_GROUP = 8
N_EXPERTS = N_GROUPS * EXPERTS_PER_GROUP
D_EXPERT = 512
MOD_CHUNKS = 6
EPS = 1e-6

_U0, _Q0, _K0, _V0, _G0, _GT0, _LR0, _IN_END = 0, 512, 1024, 1536, 2560, 3584, 5632, 5664

V7X_LANES = 128
V7X_SUBLANES = 8
V7X_MXU_DIM = 256
V7X_SCOPED_VMEM_CAP = 60000 * 1024

GLA_CHUNK = 128
GLA_HALF = GLA_CHUNK // 2
TOK_TILE = 512
MOE_TILE = 128
COND_PAD = 8
N_PAIRS = N_GROUPS * (EXPERTS_PER_GROUP * (EXPERTS_PER_GROUP - 1) // 2)


def _vmem_limit(nbytes):
    return int(min(max(nbytes, 16 * 1024 * 1024), V7X_SCOPED_VMEM_CAP))


def _dot(a, b):
    return jnp.dot(a, b, preferred_element_type=f32)


def _dot_nt(a, b):
    return lax.dot_general(a, b, (((1,), (1,)), ((), ())), preferred_element_type=f32)


def _dot_tn(a, b):
    return lax.dot_general(a, b, (((0,), (0,)), ((), ())), preferred_element_type=f32)


def _sigmoid(x):
    return 1.0 / (1.0 + jnp.exp(-x))


def _silu(x):
    return x * _sigmoid(x)


def _log_sigmoid(z):
    return jnp.minimum(z, 0.0) - jnp.log1p(jnp.exp(-jnp.abs(z)))


def _norm_mod(x, g, scale, shift):
    ms = jnp.mean(x * x, axis=-1, keepdims=True)
    return (x * lax.rsqrt(ms + EPS) * g) * (1.0 + scale) + shift


def _tri_cumsum(tri, la):
    hi = la.astype(bf16)
    lo = (la - hi.astype(f32)).astype(bf16)
    return _dot(tri, hi) + _dot(tri, lo)


ROW_TILE = V7X_SUBLANES


def _rows_to_tiles(ref, val):
    n = val.shape[0]
    for a in range(ROW_TILE):
        ref[pl.ds(a, n, stride=ROW_TILE), :] = val[:, a * V7X_LANES:(a + 1) * V7X_LANES]


def _tiles_to_rows(ref):
    n = ref.shape[0] // ROW_TILE
    return jnp.concatenate([ref[pl.ds(a, n, stride=ROW_TILE), :] for a in range(ROW_TILE)], axis=1)


def _log_decays(hx, wlr_ref, a2_ref, ab_ref):
    lr = _dot(hx, wlr_ref[...])
    z = _dot(lr.astype(bf16), a2_ref[...]) + ab_ref[...]
    return _log_sigmoid(z) * (1.0 / GLA_GATE_NORMALIZER)


def _adaln_kernel(c_ref, w_ref, b_ref, o_ref):
    c = c_ref[...]
    o_ref[...] = _dot(_silu(c).astype(bf16), w_ref[...].astype(bf16)) + b_ref[...]


def _adaln(cond, w_mod, b_mod):
    rows, d = cond.shape
    n = w_mod.shape[1]
    tn = 512
    return pl.pallas_call(
        _adaln_kernel,
        out_shape=jax.ShapeDtypeStruct((rows, n), f32),
        grid=(n // tn,),
        in_specs=[pl.BlockSpec((rows, d), lambda j: (0, 0)),
                  pl.BlockSpec((d, tn), lambda j: (0, j)),
                  pl.BlockSpec((1, tn), lambda j: (0, j))],
        out_specs=pl.BlockSpec((rows, tn), lambda j: (0, j)),
        compiler_params=pltpu.CompilerParams(dimension_semantics=("arbitrary",)),
        name="adaln",
    )(cond, w_mod, b_mod)


def _ctx_kernel(ctx_ref, mod_ref, g_ref, wkv_ref, wlr_ref, a2_ref, ab_ref, sf_ref, sb_ref):
    x = ctx_ref[0]
    n = x.shape[0]
    m = mod_ref[0]
    hc = _norm_mod(x, g_ref[...], m[1:2], m[0:1]).astype(bf16)
    kv = _dot(hc, wkv_ref[...])
    la = _log_decays(hc, wlr_ref, a2_ref, ab_ref)
    row = lax.broadcasted_iota(i32, (n, n), 0)
    col = lax.broadcasted_iota(i32, (n, n), 1)
    lower = jnp.where(row >= col, 1.0, 0.0).astype(bf16)
    upper = jnp.where(col >= row, 1.0, 0.0).astype(bf16)
    cum_f = _tri_cumsum(lower, la[:, :GLA_DK])
    cum_b = _tri_cumsum(upper, la[:, GLA_DK:])
    k = kv[:, :GLA_DK]
    v = kv[:, GLA_DK:].astype(bf16)
    kr_f = (k * jnp.exp(cum_f[n - 1:n] - cum_f)).astype(bf16)
    kr_b = (k * jnp.exp(cum_b[0:1] - cum_b)).astype(bf16)
    for h in range(GLA_HEADS):
        ks = slice(h * GLA_HK, (h + 1) * GLA_HK)
        vs = slice(h * GLA_HV, (h + 1) * GLA_HV)
        sf_ref[0, h] = _dot_tn(kr_f[:, ks], v[:, vs])
        sb_ref[0, h] = _dot_tn(kr_b[:, ks], v[:, vs])


def _ctx_states(ctx, mod3, ctx_row, g, wkv, wlr, a2bd, ab):
    b, n, d = ctx.shape
    st = jax.ShapeDtypeStruct((b, GLA_HEADS, GLA_HK, GLA_HV), f32)
    const = lambda i: (0, 0)
    st_spec = pl.BlockSpec((1, GLA_HEADS, GLA_HK, GLA_HV), lambda i: (i, 0, 0, 0))
    return pl.pallas_call(
        _ctx_kernel,
        out_shape=(st, st),
        grid=(b,),
        in_specs=[pl.BlockSpec((1, n, d), lambda i: (i, 0, 0)),
                  pl.BlockSpec((1, MOD_CHUNKS, d), lambda i: (ctx_row, 0, 0)),
                  pl.BlockSpec((1, d), const),
                  pl.BlockSpec(wkv.shape, const),
                  pl.BlockSpec(wlr.shape, const),
                  pl.BlockSpec(a2bd.shape, const),
                  pl.BlockSpec(ab.shape, const)],
        out_specs=(st_spec, st_spec),
        compiler_params=pltpu.CompilerParams(dimension_semantics=("arbitrary",),
                                             vmem_limit_bytes=_vmem_limit(32 * 1024 * 1024)),
        name="ctx_states",
    )(ctx, mod3, g, wkv, wlr, a2bd, ab)


def _qkv_kernel(x_ref, mod_ref, g_ref, wqkv_ref, wlr_ref, a2_ref, ab_ref, q_ref, k_ref, v_ref, la_ref):
    m = mod_ref[0]
    hx = _norm_mod(x_ref[0], g_ref[...], m[1:2], m[0:1]).astype(bf16)
    p = _dot(hx, wqkv_ref[...])
    q_ref[0] = p[:, :GLA_DK].astype(bf16)
    k_ref[0] = p[:, GLA_DK:2 * GLA_DK].astype(bf16)
    v_ref[0] = p[:, 2 * GLA_DK:].astype(bf16)
    la_ref[0] = _log_decays(hx, wlr_ref, a2_ref, ab_ref)


def _qkv(x, mod3, g, wqkv, wlr, a2bd, ab):
    b, s, d = x.shape
    tm = TOK_TILE
    const = lambda i, j: (0, 0)
    tok = lambda w: pl.BlockSpec((1, tm, w), lambda i, j: (i, j, 0))
    return pl.pallas_call(
        _qkv_kernel,
        out_shape=(jax.ShapeDtypeStruct((b, s, GLA_DK), bf16), jax.ShapeDtypeStruct((b, s, GLA_DK), bf16),
                   jax.ShapeDtypeStruct((b, s, GLA_DV), bf16), jax.ShapeDtypeStruct((b, s, 2 * GLA_DK), f32)),
        grid=(b, s // tm),
        in_specs=[tok(d),
                  pl.BlockSpec((1, MOD_CHUNKS, d), lambda i, j: (i, 0, 0)),
                  pl.BlockSpec((1, d), const),
                  pl.BlockSpec(wqkv.shape, const),
                  pl.BlockSpec(wlr.shape, const),
                  pl.BlockSpec(a2bd.shape, const),
                  pl.BlockSpec(ab.shape, const)],
        out_specs=(tok(GLA_DK), tok(GLA_DK), tok(GLA_DV), tok(2 * GLA_DK)),
        compiler_params=pltpu.CompilerParams(dimension_semantics=("arbitrary", "arbitrary"),
                                             vmem_limit_bytes=_vmem_limit(48 * 1024 * 1024)),
        name="qkv",
    )(x, mod3, g, wqkv, wlr, a2bd, ab)


def _gla_kernel(q_ref, k_ref, v_ref, la_ref, s0f_ref, s0b_ref, o_ref, s_sc, oacc):
    seq = q_ref.shape[1]
    nc = seq // GLA_CHUNK
    c_len = GLA_CHUNK
    s_sc[0:GLA_HEADS] = s0f_ref[0]
    s_sc[GLA_HEADS:2 * GLA_HEADS] = s0b_ref[0]
    row = lax.broadcasted_iota(i32, (c_len, c_len), 0)
    col = lax.broadcasted_iota(i32, (c_len, c_len), 1)
    lower = row >= col
    upper = col >= row
    tri_l = jnp.where(lower, 1.0, 0.0).astype(bf16)
    tri_u = jnp.where(upper, 1.0, 0.0).astype(bf16)
    scale = GLA_HK ** -0.5

    def step(c_fwd, c_bwd, second_visit):
        prep = []
        tots = []
        for d, c, tri, r_row, t_row in ((0, c_fwd, tri_l, GLA_HALF - 1, c_len - 1), (1, c_bwd, tri_u, GLA_HALF, 0)):
            rows = pl.ds(pl.multiple_of(c * c_len, c_len), c_len)
            cum = _tri_cumsum(tri, la_ref[0, rows, d * GLA_DK:(d + 1) * GLA_DK])
            r = cum[r_row:r_row + 1]
            tot = cum[t_row:t_row + 1]
            qd = q_ref[0, rows, :].astype(f32) * (jnp.exp(cum - r) * scale)
            ki = k_ref[0, rows, :].astype(f32) * jnp.exp(r - cum)
            qs = (qd * jnp.exp(r)).astype(bf16)
            kr = (ki * jnp.exp(tot - r)).astype(bf16)
            prep.append((rows, qd.astype(bf16), qs, ki.astype(bf16), kr, v_ref[0, rows, :]))
            tots.extend(tot[:, h * GLA_HK:(h + 1) * GLA_HK] for h in range(GLA_HEADS))
        pad = jnp.zeros((c_len - 2 * GLA_HEADS, GLA_HK), f32)
        dec_cols = jnp.exp(jnp.concatenate(tots + [pad], axis=0).T)
        for d, mask in ((0, lower), (1, upper)):
            rows, qd, qs, ki, kr, vv = prep[d]
            for h in range(GLA_HEADS):
                j = d * GLA_HEADS + h
                ks = slice(h * GLA_HK, (h + 1) * GLA_HK)
                vs = slice(h * GLA_HV, (h + 1) * GLA_HV)
                a = jnp.where(mask, _dot_nt(qd[:, ks], ki[:, ks]), 0.0).astype(bf16)
                s_old = s_sc[j]
                o = _dot(jnp.concatenate([a, qs[:, ks]], axis=1),
                         jnp.concatenate([vv[:, vs], s_old.astype(bf16)], axis=0))
                s_sc[j] = dec_cols[:, j:j + 1] * s_old + _dot_tn(kr[:, ks], vv[:, vs])
                if second_visit:
                    o_ref[0, rows, vs] = (oacc[rows, vs] + o).astype(o_ref.dtype)
                else:
                    oacc[rows, vs] = o

    def first(i, carry):
        step(i, nc - 1 - i, False)
        return carry

    def second(i, carry):
        step(i, nc - 1 - i, True)
        return carry

    lax.fori_loop(0, nc // 2, first, 0)
    lax.fori_loop(nc // 2, nc, second, 0)


def _gla(q, k, v, la, s0f, s0b):
    b, s, _ = q.shape
    tok = lambda w: pl.BlockSpec((1, s, w), lambda i: (i, 0, 0))
    st_spec = pl.BlockSpec((1, GLA_HEADS, GLA_HK, GLA_HV), lambda i: (i, 0, 0, 0))
    return pl.pallas_call(
        _gla_kernel,
        out_shape=jax.ShapeDtypeStruct((b, s, GLA_DV), bf16),
        grid=(b,),
        in_specs=[tok(GLA_DK), tok(GLA_DK), tok(GLA_DV), tok(2 * GLA_DK), st_spec, st_spec],
        out_specs=tok(GLA_DV),
        scratch_shapes=[pltpu.VMEM((2 * GLA_HEADS, GLA_HK, GLA_HV), f32), pltpu.VMEM((s, GLA_DV), f32)],
        compiler_params=pltpu.CompilerParams(dimension_semantics=("arbitrary",),
                                             vmem_limit_bytes=V7X_SCOPED_VMEM_CAP),
        name="gla",
    )(q, k, v, la, s0f, s0b)


def _mix_kernel(x_ref, o_ref, mod_ref, n1_ref, n2_ref, wugg_ref, band_ref, icnt_ref, pwbd_ref, pscale_ref,
                wpool_ref, onorm_ref, wgla_ref, wo_ref, wr_ref, br_ref,
                x1_ref, h2_ref, ri_ref, rw_ref):
    x = x_ref[0]
    tm = x.shape[0]
    m = mod_ref[0]
    hx = _norm_mod(x, n1_ref[...], m[1:2], m[0:1]).astype(bf16)
    p = _dot(hx, wugg_ref[...])
    u = p[:, :POOL_WIDTH]
    g = p[:, POOL_WIDTH:POOL_WIDTH + GLA_DV]
    gates = p[:, POOL_WIDTH + GLA_DV:]

    ub = u.astype(bf16)
    slab = band_ref.shape[1]
    sums = []
    for s in range(tm // slab):
        rs = slice(s * slab, (s + 1) * slab)
        sums.append(jnp.concatenate(
            [_dot(band_ref[gi], ub[rs, gi * POOL_GROUP_DIM:(gi + 1) * POOL_GROUP_DIM])
             for gi in range(len(POOL_WINDOWS))], axis=1))
    pooled = jnp.concatenate(sums, axis=0) * icnt_ref[...] - u
    y_pool = _dot(pooled.astype(bf16), pwbd_ref[...]) * pscale_ref[...]
    y_pool = _dot(y_pool.astype(bf16), wpool_ref[...])

    o = o_ref[0].astype(f32)
    normed = []
    for h in range(GLA_HEADS):
        oh = o[:, h * GLA_HV:(h + 1) * GLA_HV]
        normed.append(oh * lax.rsqrt(jnp.mean(oh * oh, axis=-1, keepdims=True) + EPS))
    on = jnp.concatenate(normed, axis=1) * onorm_ref[...]
    y_gla = _dot((on * _silu(g)).astype(bf16), wgla_ref[...])

    mixed = _sigmoid(gates[:, :D_MODEL]) * y_pool + _sigmoid(gates[:, D_MODEL:]) * y_gla
    x1 = x + m[2:3] * _dot(mixed.astype(bf16), wo_ref[...])
    x1_ref[0] = x1

    h2f = _norm_mod(x1, n2_ref[...], m[4:5], m[3:4])
    _rows_to_tiles(h2_ref, h2f)
    h2 = h2f.astype(bf16)

    lt = (_dot(h2, wr_ref[...]) + br_ref[...]).T
    g0, g1, g2, g3 = lt[0:1], lt[1:2], lt[2:3], lt[3:4]
    gmax = jnp.maximum(jnp.maximum(g0, g1), jnp.maximum(g2, g3))
    gidx = jnp.where(g0 == gmax, 0, jnp.where(g1 == gmax, 1, jnp.where(g2 == gmax, 2, 3))).astype(i32)
    p_grp = 1.0 / (jnp.exp(g0 - gmax) + jnp.exp(g1 - gmax) + jnp.exp(g2 - gmax) + jnp.exp(g3 - gmax))
    e0 = V7X_SUBLANES
    ing = lt[e0 + 3 * EXPERTS_PER_GROUP:e0 + 4 * EXPERTS_PER_GROUP]
    for gi in (2, 1, 0):
        ing = jnp.where(gidx == gi, lt[e0 + gi * EXPERTS_PER_GROUP:e0 + (gi + 1) * EXPERTS_PER_GROUP], ing)
    rid = lax.broadcasted_iota(i32, ing.shape, 0)
    v1 = jnp.max(ing, axis=0, keepdims=True)
    i1 = jnp.min(jnp.where(ing == v1, rid, EXPERTS_PER_GROUP), axis=0, keepdims=True)
    rest = jnp.where(rid == i1, -jnp.inf, ing)
    v2 = jnp.max(rest, axis=0, keepdims=True)
    i2 = jnp.min(jnp.where(rest == v2, rid, EXPERTS_PER_GROUP), axis=0, keepdims=True)
    t = jnp.exp(v2 - v1)
    w1 = p_grp / (1.0 + t)
    w2 = p_grp * t / (1.0 + t)
    first_low = i1 < i2
    ilo = jnp.minimum(i1, i2)
    ihi = jnp.maximum(i1, i2)
    pair = ilo * (EXPERTS_PER_GROUP - 1) - ((ilo * (ilo - 1)) >> 1) + (ihi - ilo - 1)
    pid = gidx * (N_PAIRS // N_GROUPS) + pair
    ri_ref[...] = jnp.where(rid == 0, pid, 0)
    rw_ref[...] = jnp.where(rid == 0, jnp.where(first_low, w1, w2),
                            jnp.where(rid == 1, jnp.where(first_low, w2, w1), 0.0))


def _mix(x, o, mod3, n1, n2, wugg, band, icnt, pwbd, pscale, wpool, onorm, wgla, wo, wr, br):
    b, s, d = x.shape
    tm = TOK_TILE
    nj = s // tm
    t = b * s
    const2 = lambda i, j: (0, 0)
    const3 = lambda i, j: (0, 0, 0)
    tok = lambda w: pl.BlockSpec((1, tm, w), lambda i, j: (i, j, 0))
    lane = pl.BlockSpec((V7X_SUBLANES, tm), lambda i, j: (0, i * nj + j))
    return pl.pallas_call(
        _mix_kernel,
        out_shape=(jax.ShapeDtypeStruct((b, s, d), f32), jax.ShapeDtypeStruct((t * ROW_TILE, V7X_LANES), f32),
                   jax.ShapeDtypeStruct((V7X_SUBLANES, t), i32), jax.ShapeDtypeStruct((V7X_SUBLANES, t), f32)),
        grid=(b, nj),
        in_specs=[tok(d), tok(GLA_DV),
                  pl.BlockSpec((1, MOD_CHUNKS, d), lambda i, j: (i, 0, 0)),
                  pl.BlockSpec((1, d), const2), pl.BlockSpec((1, d), const2),
                  pl.BlockSpec(wugg.shape, const2),
                  pl.BlockSpec(band.shape, const3),
                  pl.BlockSpec(icnt.shape, const2),
                  pl.BlockSpec(pwbd.shape, const2),
                  pl.BlockSpec(pscale.shape, const2),
                  pl.BlockSpec(wpool.shape, const2),
                  pl.BlockSpec(onorm.shape, const2),
                  pl.BlockSpec(wgla.shape, const2),
                  pl.BlockSpec(wo.shape, const2),
                  pl.BlockSpec(wr.shape, const2),
                  pl.BlockSpec(br.shape, const2)],
        out_specs=(tok(d), pl.BlockSpec((tm * ROW_TILE, V7X_LANES), lambda i, j: (i * nj + j, 0)), lane, lane),
        compiler_params=pltpu.CompilerParams(dimension_semantics=("arbitrary", "arbitrary"),
                                             vmem_limit_bytes=V7X_SCOPED_VMEM_CAP),
        name="mix",
    )(x, o, mod3, n1, n2, wugg, band, icnt, pwbd, pscale, wpool, onorm, wgla, wo, wr, br)


def _moe_kernel(ta_ref, tb_ref, nv_ref, src0_ref, src1_ref, srcn0_ref, srcn1_ref, dstp0_ref, dstp1_ref, dstl_ref,
                wgt0_ref, wgt1_ref, h2_ref, w13a0_ref, w2a0_ref, w13b0_ref, w2b0_ref,
                w13a1_ref, w2a1_ref, w13b1_ref, w2b1_ref, y_ref, xbuf, ybuf, gsem, ssem):
    del ta_ref, tb_ref
    j = pl.program_id(0)
    last = pl.num_programs(0) - 1
    nv = nv_ref[0]
    tm = xbuf.shape[1] // ROW_TILE
    spare0 = y_ref.shape[0] - 2 * tm * ROW_TILE

    def token_tile(ref, off):
        return ref.at[pl.ds(pl.multiple_of(off, ROW_TILE), ROW_TILE)]

    def gather_row(off, s, r8):
        return pltpu.make_async_copy(token_tile(h2_ref, off), xbuf.at[s, pl.ds(r8, ROW_TILE)], gsem.at[s])

    def scatter_row(off, s, r8):
        return pltpu.make_async_copy(ybuf.at[s, pl.ds(r8, ROW_TILE)], token_tile(y_ref, off), ssem.at[s])

    def start_rows(make, idx_ref):
        def body(r, carry):
            make(idx_ref[0, 0, r], pl.multiple_of(r * ROW_TILE, ROW_TILE)).start()
            return carry
        lax.fori_loop(0, tm, body, 0, unroll=8)

    def gather_tile(s):
        return pltpu.make_async_copy(h2_ref.at[pl.ds(0, tm * ROW_TILE)], xbuf.at[s], gsem.at[s])

    def scatter_tile(s):
        return pltpu.make_async_copy(ybuf.at[s], y_ref.at[pl.ds(0, tm * ROW_TILE)], ssem.at[s])

    @pl.when(j == 0)
    def _():
        ybuf[...] = jnp.zeros(ybuf.shape, ybuf.dtype)
        for s in range(2):
            spare = pltpu.make_async_copy(ybuf.at[0], y_ref.at[pl.ds(spare0 + s * tm * ROW_TILE, tm * ROW_TILE)],
                                          ssem.at[0])
            spare.start()
            spare.wait()

        @pl.when(nv > 0)
        def _():
            start_rows(lambda off, r8: gather_row(off, 0, r8), src0_ref)
            start_rows(lambda off, r8: gather_row(off, 1, r8), src1_ref)

    n_pieces = 12
    per_piece = -(-tm // n_pieces)

    def tile_body(s, tile, srcn_ref, dstp_ref, wgt_ref, w13a_ref, w2a_ref, w13b_ref, w2b_ref):
        @pl.when(tile < nv)
        def _():
            gather_tile(s).wait()
            xt = _tiles_to_rows(xbuf.at[s]).astype(bf16)
            piece = [0]

            def issue_rows():
                lo = piece[0] * per_piece
                rows = range(lo, min(lo + per_piece, tm))
                src_rows = [srcn_ref[0, 0, r] for r in rows]
                dst_rows = [dstp_ref[0, 0, r] for r in rows]
                for r, sr, dr in zip(rows, src_rows, dst_rows):
                    gather_row(sr, s, r * ROW_TILE).start()
                    scatter_row(dr, 1 - s, r * ROW_TILE).start()
                piece[0] += 1

            w = wgt_ref[...]
            half = D_EXPERT // 2
            y = None
            for e, (w13_ref, w2_ref) in enumerate(((w13a_ref, w2a_ref), (w13b_ref, w2b_ref))):
                acc = None
                for c in range(2):
                    h1 = _dot(xt, w13_ref[0, :, c * half:(c + 1) * half])
                    issue_rows()
                    h3 = _dot(xt, w13_ref[0, :, D_EXPERT + c * half:D_EXPERT + (c + 1) * half])
                    issue_rows()
                    part = _dot((_silu(h1) * h3).astype(bf16), w2_ref[0, c * half:(c + 1) * half, :])
                    issue_rows()
                    acc = part if acc is None else acc + part
                y = w[:, e:e + 1] * acc if y is None else y + w[:, e:e + 1] * acc
            assert piece[0] == n_pieces

            @pl.when(tile >= 1)
            def _():
                scatter_tile(s).wait()

            _rows_to_tiles(ybuf.at[s], y)

    tile_body(0, 2 * j, srcn0_ref, dstp0_ref, wgt0_ref, w13a0_ref, w2a0_ref, w13b0_ref, w2b0_ref)
    tile_body(1, 2 * j + 1, srcn1_ref, dstp1_ref, wgt1_ref, w13a1_ref, w2a1_ref, w13b1_ref, w2b1_ref)

    @pl.when(jnp.logical_and(j == last, nv > 0))
    def _():
        sl = (nv - 1) % 2
        scatter_tile(1 - sl).wait()
        start_rows(lambda off, r8: scatter_row(off, sl, r8), dstl_ref)
        scatter_tile(sl).wait()
        gather_tile(0).wait()
        gather_tile(1).wait()


def _moe(tile_a, tile_b, nvalid, src, dst, wgt, h2, w13, w2):
    nt, _, tm = src.shape
    assert nt % 2 == 0 and dst.shape[0] == nt + 1
    rows, lanes = h2.shape
    d = ROW_TILE * lanes
    idx_blk = (1, 1, tm)
    smem_idx = lambda fn: pl.BlockSpec(idx_blk, fn, memory_space=pltpu.SMEM)
    w13_blk = (1, d, 2 * D_EXPERT)
    w2_blk = (1, D_EXPERT, d)

    def weights(s):
        return [pl.BlockSpec(w13_blk, lambda j, ta, tb, nv: (ta[2 * j + s], 0, 0)),
                pl.BlockSpec(w2_blk, lambda j, ta, tb, nv: (ta[2 * j + s], 0, 0)),
                pl.BlockSpec(w13_blk, lambda j, ta, tb, nv: (tb[2 * j + s], 0, 0)),
                pl.BlockSpec(w2_blk, lambda j, ta, tb, nv: (tb[2 * j + s], 0, 0))]

    grid_spec = pltpu.PrefetchScalarGridSpec(
        num_scalar_prefetch=3,
        grid=(nt // 2,),
        in_specs=[smem_idx(lambda j, ta, tb, nv: (0, 0, 0)),
                  smem_idx(lambda j, ta, tb, nv: (1, 0, 0)),
                  smem_idx(lambda j, ta, tb, nv: (jnp.minimum(2 * j + 2, nt - 1), 0, 0)),
                  smem_idx(lambda j, ta, tb, nv: (jnp.minimum(2 * j + 3, nt - 1), 0, 0)),
                  smem_idx(lambda j, ta, tb, nv: (jnp.where(j == 0, nt, 2 * j - 1), 0, 0)),
                  smem_idx(lambda j, ta, tb, nv: (2 * j, 0, 0)),
                  smem_idx(lambda j, ta, tb, nv: (jnp.maximum(nv[0] - 1, 0), 0, 0)),
                  pl.BlockSpec((tm, 2), lambda j, ta, tb, nv: (2 * j, 0)),
                  pl.BlockSpec((tm, 2), lambda j, ta, tb, nv: (2 * j + 1, 0)),
                  pl.BlockSpec(memory_space=pl.ANY)] + weights(0) + weights(1),
        out_specs=pl.BlockSpec(memory_space=pl.ANY),
        scratch_shapes=[pltpu.VMEM((2, tm * ROW_TILE, lanes), f32), pltpu.VMEM((2, tm * ROW_TILE, lanes), f32),
                        pltpu.SemaphoreType.DMA((2,)), pltpu.SemaphoreType.DMA((2,))],
    )
    return pl.pallas_call(
        _moe_kernel,
        out_shape=jax.ShapeDtypeStruct((rows + 2 * tm * ROW_TILE, lanes), f32),
        grid_spec=grid_spec,
        compiler_params=pltpu.CompilerParams(dimension_semantics=("arbitrary",),
                                             vmem_limit_bytes=_vmem_limit(48 * 1024 * 1024)),
        name="moe",
    )(tile_a, tile_b, nvalid, src, src, src, src, dst, dst, dst, wgt, wgt, h2, w13, w2, w13, w2, w13, w2, w13, w2)


def _pair_tables():
    a, b = [], []
    for g in range(N_GROUPS):
        for i in range(EXPERTS_PER_GROUP):
            for j in range(i + 1, EXPERTS_PER_GROUP):
                a.append(g * EXPERTS_PER_GROUP + i)
                b.append(g * EXPERTS_PER_GROUP + j)
    return np.asarray(a, np.int32), np.asarray(b, np.int32)


def _dispatch_plan(pid, wlo, whi, tm):
    t = pid.shape[0]
    nt = 2 * (-(-((t + N_PAIRS * (tm - 1)) // tm) // 2))
    order = jnp.argsort(pid, stable=True).astype(i32)
    starts = jnp.searchsorted(pid[order], jnp.arange(N_PAIRS + 1, dtype=i32), side="left").astype(i32)
    counts = starts[1:] - starts[:-1]
    tiles = (counts + tm - 1) // tm
    tile_end = jnp.cumsum(tiles)
    tile_start = tile_end - tiles
    nvalid = tile_end[-1:]
    tile = jnp.arange(nt, dtype=i32)
    grp = jnp.minimum(jnp.searchsorted(tile_end, tile, side="right"), N_PAIRS - 1).astype(i32)
    rank = (tile - tile_start[grp])[:, None] * tm + jnp.arange(tm, dtype=i32)[None, :]
    valid = jnp.logical_and(rank < counts[grp][:, None], (tile < nvalid[0])[:, None])
    tok = order[jnp.clip(starts[grp][:, None] + rank, 0, t - 1)]
    src = jnp.where(valid, tok, 0)
    row = jnp.arange(tm, dtype=i32)[None, :]
    dst = jnp.where(valid, tok, t + (tile % 2)[:, None] * tm + row)
    dst = jnp.concatenate([dst, t + tm + row], axis=0).reshape(nt + 1, 1, tm)
    wgt = jnp.stack([jnp.where(valid, wlo[src], 0.0), jnp.where(valid, whi[src], 0.0)], axis=-1).reshape(nt * tm, 2)
    pa, pb = _pair_tables()
    return (jnp.asarray(pa)[grp], jnp.asarray(pb)[grp], nvalid.astype(i32),
            src.reshape(nt, 1, tm) * ROW_TILE, dst * ROW_TILE, wgt)


def _final_kernel(x1_ref, y_ref, mod_ref, g_ref, out_ref):
    m = mod_ref[0]
    x2 = x1_ref[0] + m[5:6] * _tiles_to_rows(y_ref)
    ms = jnp.mean(x2 * x2, axis=-1, keepdims=True)
    out_ref[0] = x2 * lax.rsqrt(ms + EPS) * g_ref[...]


def _final(x1, y, mod3, g):
    b, s, d = x1.shape
    tm = TOK_TILE
    nj = s // tm
    tok = pl.BlockSpec((1, tm, d), lambda i, j: (i, j, 0))
    return pl.pallas_call(
        _final_kernel,
        out_shape=jax.ShapeDtypeStruct((b, s, d), f32),
        grid=(b, nj),
        in_specs=[tok,
                  pl.BlockSpec((tm * ROW_TILE, y.shape[1]), lambda i, j: (i * nj + j, 0)),
                  pl.BlockSpec((1, MOD_CHUNKS, d), lambda i, j: (i, 0, 0)),
                  pl.BlockSpec((1, d), lambda i, j: (0, 0))],
        out_specs=tok,
        compiler_params=pltpu.CompilerParams(dimension_semantics=("arbitrary", "arbitrary"),
                                             vmem_limit_bytes=_vmem_limit(32 * 1024 * 1024)),
        name="final",
    )(x1, y, mod3, g)


def _pool_constants(tm):
    slab = V7X_MXU_DIM
    band = np.zeros((len(POOL_WINDOWS), slab, slab), np.float32)
    icnt = np.zeros((tm, POOL_WIDTH), np.float32)
    for gi, w in enumerate(POOL_WINDOWS):
        for t in range(slab):
            base, pos = (t // GRID_W) * GRID_W, t % GRID_W
            lo, hi = max(pos - w // 2, 0), min(pos + w // 2, GRID_W)
            band[gi, t, base + lo:base + hi] = 1.0
        for t in range(tm):
            pos = t % GRID_W
            cnt = min(pos + w // 2, GRID_W) - max(pos - w // 2, 0)
            icnt[t, gi * POOL_GROUP_DIM:(gi + 1) * POOL_GROUP_DIM] = 1.0 / cnt
    return jnp.asarray(band, bf16), jnp.asarray(icnt, f32)


def _block_diag(blocks):
    n = len(blocks)
    r, c = blocks[0].shape
    out = jnp.zeros((n * r, n * c), blocks[0].dtype)
    for i, blk in enumerate(blocks):
        out = out.at[i * r:(i + 1) * r, i * c:(i + 1) * c].set(blk)
    return out


def kernel(x, c, ctx, c_ctx, w_mod, b_mod, norm1_g, norm2_g, w_in, gla_a2_f, gla_ab_f, gla_a2_b, gla_ab_b,
           gla_onorm_g, pool_w, pool_scale, w_pool_br, w_gla_br, w_o, router_grp_w, router_grp_b,
           router_exp_w, router_exp_b, moe_w1, moe_w3, moe_w2, final_norm_g):
    b, s, d = x.shape
    depth = w_mod.shape[0]
    assert depth == 1 and d == D_MODEL and s % TOK_TILE == 0 and s % (2 * GLA_CHUNK) == 0
    assert TOK_TILE % V7X_MXU_DIM == 0 and V7X_MXU_DIM % GRID_W == 0
    t = b * s

    cond = jnp.concatenate([c, c_ctx[None, :], jnp.zeros((COND_PAD - 1, d), f32)], axis=0)
    mod3 = _adaln(cond, w_mod[0], b_mod[0][None, :]).reshape(b + COND_PAD, MOD_CHUNKS, d)

    w_in_b = w_in[0].astype(bf16)
    wqkv = w_in_b[:, _Q0:_G0]
    wkv = w_in_b[:, _K0:_G0]
    wugg = jnp.concatenate([w_in_b[:, _U0:_Q0], w_in_b[:, _G0:_LR0]], axis=1)
    wlr = jnp.pad(w_in_b[:, _LR0:_IN_END], ((0, 0), (0, V7X_LANES - 2 * GLA_GATE_RANK)))
    a2bd = jnp.pad(_block_diag([gla_a2_f[0], gla_a2_b[0]]).astype(bf16),
                   ((0, V7X_LANES - 2 * GLA_GATE_RANK), (0, 0)))
    ab = jnp.concatenate([gla_ab_f[0], gla_ab_b[0]])[None, :]
    n1 = norm1_g[0][None, :]
    n2 = norm2_g[0][None, :]

    s0f, s0b = _ctx_states(ctx, mod3, b, n1, wkv, wlr, a2bd, ab)
    q, k, v, la = _qkv(x, mod3, n1, wqkv, wlr, a2bd, ab)
    o = _gla(q, k, v, la, s0f, s0b)

    band, icnt = _pool_constants(TOK_TILE)
    pwbd = _block_diag([pool_w[0, gi] for gi in range(len(POOL_WINDOWS))]).astype(bf16)
    onorm = jnp.tile(gla_onorm_g[0], GLA_HEADS)[None, :]
    wr = jnp.zeros((d, V7X_LANES), f32)
    wr = wr.at[:, :N_GROUPS].set(router_grp_w[0]).at[:, V7X_SUBLANES:V7X_SUBLANES + N_EXPERTS].set(router_exp_w[0])
    br = jnp.zeros((1, V7X_LANES), f32)
    br = br.at[0, :N_GROUPS].set(router_grp_b[0]).at[0, V7X_SUBLANES:V7X_SUBLANES + N_EXPERTS].set(router_exp_b[0])
    x1, h2, ri, rw = _mix(x, o, mod3, n1, n2, wugg, band, icnt, pwbd, pool_scale[0][None, :],
                           w_pool_br[0].astype(bf16), onorm, w_gla_br[0].astype(bf16), w_o[0].astype(bf16),
                           wr.astype(bf16), br)

    plan = _dispatch_plan(ri[0], rw[0], rw[1], MOE_TILE)
    w13 = jnp.concatenate([moe_w1[0], moe_w3[0]], axis=-1).astype(bf16)
    y = _moe(*plan, h2, w13, moe_w2[0].astype(bf16))
    return _final(x1, y, mod3, final_norm_g[None, :])
```

```python
import jax
import jax.numpy as jnp
import numpy as np
from jax import lax
from jax.experimental import pallas as pl
from jax.experimental.pallas import tpu as pltpu

f32 = jnp.float32
bf16 = jnp.bfloat16
i32 = jnp.int32
u32 = jnp.uint32

D_MODEL = 1024
GRID_W = 64
POOL_WINDOWS = (2, 4, 8, 16)
POOL_GROUP_DIM = 128
POOL_WIDTH = len(POOL_WINDOWS) * POOL_GROUP_DIM
GLA_HEADS = 4
GLA_DK = 512
GLA_DV = 1024
GLA_HK = GLA_DK // GLA_HEADS
GLA_HV = GLA_DV // GLA_HEADS
GLA_GATE_RANK = 16
GLA_GATE_NORMALIZER = 16.0
N_GROUPS = 4
EXPERTS_PER_GROUP = 8
N_EXPERTS = N_GROUPS * EXPERTS_PER_GROUP
D_EXPERT = 512
MOD_CHUNKS = 6
EPS = 1e-6

_U0, _Q0, _K0, _V0, _G0, _GT0, _LR0, _IN_END = 0, 512, 1024, 1536, 2560, 3584, 5632, 5664

V7X_LANES = 128
V7X_SUBLANES = 8
V7X_MXU_DIM = 256
V7X_SCOPED_VMEM_CAP = 60000 * 1024

GLA_CHUNK = 128
GLA_HALF = GLA_CHUNK // 2
TOK_TILE = 512
MOE_TILE = 128
COND_PAD = 8
N_PAIRS = N_GROUPS * (EXPERTS_PER_GROUP * (EXPERTS_PER_GROUP - 1) // 2)


def _vmem_limit(nbytes):
    return int(min(max(nbytes, 16 * 1024 * 1024), V7X_SCOPED_VMEM_CAP))


def _dot(a, b):
    return jnp.dot(a, b, preferred_element_type=f32)


def _dot_nt(a, b):
    return lax.dot_general(a, b, (((1,), (1,)), ((), ())), preferred_element_type=f32)


def _dot_tn(a, b):
    return lax.dot_general(a, b, (((0,), (0,)), ((), ())), preferred_element_type=f32)


def _sigmoid(x):
    return 1.0 / (1.0 + jnp.exp(-x))


def _silu(x):
    return x * _sigmoid(x)


def _log_sigmoid(z):
    return jnp.minimum(z, 0.0) - jnp.log1p(jnp.exp(-jnp.abs(z)))


def _norm_mod(x, g, scale, shift):
    ms = jnp.mean(x * x, axis=-1, keepdims=True)
    return (x * lax.rsqrt(ms + EPS) * g) * (1.0 + scale) + shift


def _tri_cumsum(tri, la):
    hi = la.astype(bf16)
    lo = (la - hi.astype(f32)).astype(bf16)
    return _dot(tri, hi) + _dot(tri, lo)


ROW_TILE = V7X_SUBLANES


def _rows_to_tiles(ref, val):
    n = val.shape[0]
    for a in range(ROW_TILE):
        ref[pl.ds(a, n, stride=ROW_TILE), :] = val[:, a * V7X_LANES:(a + 1) * V7X_LANES]


def _tiles_to_rows(ref):
    n = ref.shape[0] // ROW_TILE
    return jnp.concatenate([ref[pl.ds(a, n, stride=ROW_TILE), :] for a in range(ROW_TILE)], axis=1)


def _log_decays(hx, wlr_ref, a2_ref, ab_ref):
    lr = _dot(hx, wlr_ref[...])
    z = _dot(lr.astype(bf16), a2_ref[...]) + ab_ref[...]
    return _log_sigmoid(z) * (1.0 / GLA_GATE_NORMALIZER)


def _adaln_kernel(c_ref, w_ref, b_ref, o_ref):
    c = c_ref[...]
    o_ref[...] = _dot(_silu(c).astype(bf16), w_ref[...].astype(bf16)) + b_ref[...]


def _adaln(cond, w_mod, b_mod):
    rows, d = cond.shape
    n = w_mod.shape[1]
    tn = 512
    return pl.pallas_call(
        _adaln_kernel,
        out_shape=jax.ShapeDtypeStruct((rows, n), f32),
        grid=(n // tn,),
        in_specs=[pl.BlockSpec((rows, d), lambda j: (0, 0)),
                  pl.BlockSpec((d, tn), lambda j: (0, j)),
                  pl.BlockSpec((1, tn), lambda j: (0, j))],
        out_specs=pl.BlockSpec((rows, tn), lambda j: (0, j)),
        compiler_params=pltpu.CompilerParams(dimension_semantics=("arbitrary",)),
        name="adaln",
    )(cond, w_mod, b_mod)


def _ctx_kernel(ctx_ref, mod_ref, g_ref, wkv_ref, wlr_ref, a2_ref, ab_ref, sf_ref, sb_ref):
    x = ctx_ref[0]
    n = x.shape[0]
    m = mod_ref[0]
    hc = _norm_mod(x, g_ref[...], m[1:2], m[0:1]).astype(bf16)
    kv = _dot(hc, wkv_ref[...])
    la = _log_decays(hc, wlr_ref, a2_ref, ab_ref)
    row = lax.broadcasted_iota(i32, (n, n), 0)
    col = lax.broadcasted_iota(i32, (n, n), 1)
    lower = jnp.where(row >= col, 1.0, 0.0).astype(bf16)
    upper = jnp.where(col >= row, 1.0, 0.0).astype(bf16)
    cum_f = _tri_cumsum(lower, la[:, :GLA_DK])
    cum_b = _tri_cumsum(upper, la[:, GLA_DK:])
    k = kv[:, :GLA_DK]
    v = kv[:, GLA_DK:].astype(bf16)
    kr_f = (k * jnp.exp(cum_f[n - 1:n] - cum_f)).astype(bf16)
    kr_b = (k * jnp.exp(cum_b[0:1] - cum_b)).astype(bf16)
    for h in range(GLA_HEADS):
        ks = slice(h * GLA_HK, (h + 1) * GLA_HK)
        vs = slice(h * GLA_HV, (h + 1) * GLA_HV)
        sf_ref[0, h] = _dot_tn(kr_f[:, ks], v[:, vs])
        sb_ref[0, h] = _dot_tn(kr_b[:, ks], v[:, vs])


def _ctx_states(ctx, mod3, ctx_row, g, wkv, wlr, a2bd, ab):
    b, n, d = ctx.shape
    st = jax.ShapeDtypeStruct((b, GLA_HEADS, GLA_HK, GLA_HV), f32)
    const = lambda i: (0, 0)
    st_spec = pl.BlockSpec((1, GLA_HEADS, GLA_HK, GLA_HV), lambda i: (i, 0, 0, 0))
    return pl.pallas_call(
        _ctx_kernel,
        out_shape=(st, st),
        grid=(b,),
        in_specs=[pl.BlockSpec((1, n, d), lambda i: (i, 0, 0)),
                  pl.BlockSpec((1, MOD_CHUNKS, d), lambda i: (ctx_row, 0, 0)),
                  pl.BlockSpec((1, d), const),
                  pl.BlockSpec(wkv.shape, const),
                  pl.BlockSpec(wlr.shape, const),
                  pl.BlockSpec(a2bd.shape, const),
                  pl.BlockSpec(ab.shape, const)],
        out_specs=(st_spec, st_spec),
        compiler_params=pltpu.CompilerParams(dimension_semantics=("arbitrary",),
                                             vmem_limit_bytes=_vmem_limit(32 * 1024 * 1024)),
        name="ctx_states",
    )(ctx, mod3, g, wkv, wlr, a2bd, ab)


def _qkv_kernel(x_ref, mod_ref, g_ref, wqkv_ref, wlr_ref, a2_ref, ab_ref, q_ref, k_ref, v_ref, la_ref):
    m = mod_ref[0]
    hx = _norm_mod(x_ref[0], g_ref[...], m[1:2], m[0:1]).astype(bf16)
    p = _dot(hx, wqkv_ref[...])
    q_ref[0] = p[:, :GLA_DK].astype(bf16)
    k_ref[0] = p[:, GLA_DK:2 * GLA_DK].astype(bf16)
    v_ref[0] = p[:, 2 * GLA_DK:].astype(bf16)
    la_ref[0] = _log_decays(hx, wlr_ref, a2_ref, ab_ref)


def _qkv(x, mod3, g, wqkv, wlr, a2bd, ab):
    b, s, d = x.shape
    tm = TOK_TILE
    const = lambda i, j: (0, 0)
    tok = lambda w: pl.BlockSpec((1, tm, w), lambda i, j: (i, j, 0))
    return pl.pallas_call(
        _qkv_kernel,
        out_shape=(jax.ShapeDtypeStruct((b, s, GLA_DK), bf16), jax.ShapeDtypeStruct((b, s, GLA_DK), bf16),
                   jax.ShapeDtypeStruct((b, s, GLA_DV), bf16), jax.ShapeDtypeStruct((b, s, 2 * GLA_DK), f32)),
        grid=(b, s // tm),
        in_specs=[tok(d),
                  pl.BlockSpec((1, MOD_CHUNKS, d), lambda i, j: (i, 0, 0)),
                  pl.BlockSpec((1, d), const),
                  pl.BlockSpec(wqkv.shape, const),
                  pl.BlockSpec(wlr.shape, const),
                  pl.BlockSpec(a2bd.shape, const),
                  pl.BlockSpec(ab.shape, const)],
        out_specs=(tok(GLA_DK), tok(GLA_DK), tok(GLA_DV), tok(2 * GLA_DK)),
        compiler_params=pltpu.CompilerParams(dimension_semantics=("arbitrary", "arbitrary"),
                                             vmem_limit_bytes=_vmem_limit(48 * 1024 * 1024)),
        name="qkv",
    )(x, mod3, g, wqkv, wlr, a2bd, ab)


def _gla_kernel(q_ref, k_ref, v_ref, la_ref, s0f_ref, s0b_ref, o_ref, s_sc, oacc):
    seq = q_ref.shape[1]
    nc = seq // GLA_CHUNK
    c_len = GLA_CHUNK
    s_sc[0:GLA_HEADS] = s0f_ref[0]
    s_sc[GLA_HEADS:2 * GLA_HEADS] = s0b_ref[0]
    row = lax.broadcasted_iota(i32, (c_len, c_len), 0)
    col = lax.broadcasted_iota(i32, (c_len, c_len), 1)
    lower = row >= col
    upper = col >= row
    tri_l = jnp.where(lower, 1.0, 0.0).astype(bf16)
    tri_u = jnp.where(upper, 1.0, 0.0).astype(bf16)
    scale = GLA_HK ** -0.5

    def step(c_fwd, c_bwd, second_visit):
        prep = []
        tots = []
        for d, c, tri, r_row, t_row in ((0, c_fwd, tri_l, GLA_HALF - 1, c_len - 1), (1, c_bwd, tri_u, GLA_HALF, 0)):
            rows = pl.ds(pl.multiple_of(c * c_len, c_len), c_len)
            cum = _tri_cumsum(tri, la_ref[0, rows, d * GLA_DK:(d + 1) * GLA_DK])
            r = cum[r_row:r_row + 1]
            tot = cum[t_row:t_row + 1]
            qd = q_ref[0, rows, :].astype(f32) * (jnp.exp(cum - r) * scale)
            ki = k_ref[0, rows, :].astype(f32) * jnp.exp(r - cum)
            qs = (qd * jnp.exp(r)).astype(bf16)
            kr = (ki * jnp.exp(tot - r)).astype(bf16)
            prep.append((rows, qd.astype(bf16), qs, ki.astype(bf16), kr, v_ref[0, rows, :]))
            tots.extend(tot[:, h * GLA_HK:(h + 1) * GLA_HK] for h in range(GLA_HEADS))
        pad = jnp.zeros((c_len - 2 * GLA_HEADS, GLA_HK), f32)
        dec_cols = jnp.exp(jnp.concatenate(tots + [pad], axis=0).T)
        for d, mask in ((0, lower), (1, upper)):
            rows, qd, qs, ki, kr, vv = prep[d]
            for h in range(GLA_HEADS):
                j = d * GLA_HEADS + h
                ks = slice(h * GLA_HK, (h + 1) * GLA_HK)
                vs = slice(h * GLA_HV, (h + 1) * GLA_HV)
                a = jnp.where(mask, _dot_nt(qd[:, ks], ki[:, ks]), 0.0).astype(bf16)
                s_old = s_sc[j]
                o = _dot(jnp.concatenate([a, qs[:, ks]], axis=1),
                         jnp.concatenate([vv[:, vs], s_old.astype(bf16)], axis=0))
                s_sc[j] = dec_cols[:, j:j + 1] * s_old + _dot_tn(kr[:, ks], vv[:, vs])
                if second_visit:
                    o_ref[0, rows, vs] = (oacc[rows, vs] + o).astype(o_ref.dtype)
                else:
                    oacc[rows, vs] = o

    def first(i, carry):
        step(i, nc - 1 - i, False)
        return carry

    def second(i, carry):
        step(i, nc - 1 - i, True)
        return carry

    lax.fori_loop(0, nc // 2, first, 0)
    lax.fori_loop(nc // 2, nc, second, 0)


def _gla(q, k, v, la, s0f, s0b):
    b, s, _ = q.shape
    tok = lambda w: pl.BlockSpec((1, s, w), lambda i: (i, 0, 0))
    st_spec = pl.BlockSpec((1, GLA_HEADS, GLA_HK, GLA_HV), lambda i: (i, 0, 0, 0))
    return pl.pallas_call(
        _gla_kernel,
        out_shape=jax.ShapeDtypeStruct((b, s, GLA_DV), bf16),
        grid=(b,),
        in_specs=[tok(GLA_DK), tok(GLA_DK), tok(GLA_DV), tok(2 * GLA_DK), st_spec, st_spec],
        out_specs=tok(GLA_DV),
        scratch_shapes=[pltpu.VMEM((2 * GLA_HEADS, GLA_HK, GLA_HV), f32), pltpu.VMEM((s, GLA_DV), f32)],
        compiler_params=pltpu.CompilerParams(dimension_semantics=("arbitrary",),
                                             vmem_limit_bytes=V7X_SCOPED_VMEM_CAP),
        name="gla",
    )(q, k, v, la, s0f, s0b)


def _mix_kernel(x_ref, o_ref, mod_ref, n1_ref, n2_ref, wugg_ref, band_ref, icnt_ref, pwbd_ref, pscale_ref,
                wpool_ref, onorm_ref, wgla_ref, wo_ref, wr_ref, br_ref,
                x1_ref, h2_ref, ri_ref, rw_ref):
    x = x_ref[0]
    tm = x.shape[0]
    m = mod_ref[0]
    hx = _norm_mod(x, n1_ref[...], m[1:2], m[0:1]).astype(bf16)
    p = _dot(hx, wugg_ref[...])
    u = p[:, :POOL_WIDTH]
    g = p[:, POOL_WIDTH:POOL_WIDTH + GLA_DV]
    gates = p[:, POOL_WIDTH + GLA_DV:]

    ub = u.astype(bf16)
    slab = band_ref.shape[1]
    sums = []
    for s in range(tm // slab):
        rs = slice(s * slab, (s + 1) * slab)
        sums.append(jnp.concatenate(
            [_dot(band_ref[gi], ub[rs, gi * POOL_GROUP_DIM:(gi + 1) * POOL_GROUP_DIM])
             for gi in range(len(POOL_WINDOWS))], axis=1))
    pooled = jnp.concatenate(sums, axis=0) * icnt_ref[...] - u
    y_pool = _dot(pooled.astype(bf16), pwbd_ref[...]) * pscale_ref[...]
    y_pool = _dot(y_pool.astype(bf16), wpool_ref[...])

    o = o_ref[0].astype(f32)
    normed = []
    for h in range(GLA_HEADS):
        oh = o[:, h * GLA_HV:(h + 1) * GLA_HV]
        normed.append(oh * lax.rsqrt(jnp.mean(oh * oh, axis=-1, keepdims=True) + EPS))
    on = jnp.concatenate(normed, axis=1) * onorm_ref[...]
    y_gla = _dot((on * _silu(g)).astype(bf16), wgla_ref[...])

    mixed = _sigmoid(gates[:, :D_MODEL]) * y_pool + _sigmoid(gates[:, D_MODEL:]) * y_gla
    x1 = x + m[2:3] * _dot(mixed.astype(bf16), wo_ref[...])
    x1_ref[0] = x1

    h2f = _norm_mod(x1, n2_ref[...], m[4:5], m[3:4])
    _rows_to_tiles(h2_ref, h2f)
    h2 = h2f.astype(bf16)

    lt = (_dot(h2, wr_ref[...]) + br_ref[...]).T
    g0, g1, g2, g3 = lt[0:1], lt[1:2], lt[2:3], lt[3:4]
    gmax = jnp.maximum(jnp.maximum(g0, g1), jnp.maximum(g2, g3))
    gidx = jnp.where(g0 == gmax, 0, jnp.where(g1 == gmax, 1, jnp.where(g2 == gmax, 2, 3))).astype(i32)
    p_grp = 1.0 / (jnp.exp(g0 - gmax) + jnp.exp(g1 - gmax) + jnp.exp(g2 - gmax) + jnp.exp(g3 - gmax))
    e0 = V7X_SUBLANES
    ing = lt[e0 + 3 * EXPERTS_PER_GROUP:e0 + 4 * EXPERTS_PER_GROUP]
    for gi in (2, 1, 0):
        ing = jnp.where(gidx == gi, lt[e0 + gi * EXPERTS_PER_GROUP:e0 + (gi + 1) * EXPERTS_PER_GROUP], ing)
    rid = lax.broadcasted_iota(i32, ing.shape, 0)
    v1 = jnp.max(ing, axis=0, keepdims=True)
    i1 = jnp.min(jnp.where(ing == v1, rid, EXPERTS_PER_GROUP), axis=0, keepdims=True)
    rest = jnp.where(rid == i1, -jnp.inf, ing)
    v2 = jnp.max(rest, axis=0, keepdims=True)
    i2 = jnp.min(jnp.where(rest == v2, rid, EXPERTS_PER_GROUP), axis=0, keepdims=True)
    t = jnp.exp(v2 - v1)
    w1 = p_grp / (1.0 + t)
    w2 = p_grp * t / (1.0 + t)
    first_low = i1 < i2
    ilo = jnp.minimum(i1, i2)
    ihi = jnp.maximum(i1, i2)
    pair = ilo * (EXPERTS_PER_GROUP - 1) - ((ilo * (ilo - 1)) >> 1) + (ihi - ilo - 1)
    pid = gidx * (N_PAIRS // N_GROUPS) + pair
    ri_ref[...] = jnp.where(rid == 0, pid, 0)
    rw_ref[...] = jnp.where(rid == 0, jnp.where(first_low, w1, w2),
                            jnp.where(rid == 1, jnp.where(first_low, w2, w1), 0.0))


def _mix(x, o, mod3, n1, n2, wugg, band, icnt, pwbd, pscale, wpool, onorm, wgla, wo, wr, br):
    b, s, d = x.shape
    tm = TOK_TILE
    nj = s // tm
    t = b * s
    const2 = lambda i, j: (0, 0)
    const3 = lambda i, j: (0, 0, 0)
    tok = lambda w: pl.BlockSpec((1, tm, w), lambda i, j: (i, j, 0))
    lane = pl.BlockSpec((V7X_SUBLANES, tm), lambda i, j: (0, i * nj + j))
    return pl.pallas_call(
        _mix_kernel,
        out_shape=(jax.ShapeDtypeStruct((b, s, d), f32), jax.ShapeDtypeStruct((t * ROW_TILE, V7X_LANES), f32),
                   jax.ShapeDtypeStruct((V7X_SUBLANES, t), i32), jax.ShapeDtypeStruct((V7X_SUBLANES, t), f32)),
        grid=(b, nj),
        in_specs=[tok(d), tok(GLA_DV),
                  pl.BlockSpec((1, MOD_CHUNKS, d), lambda i, j: (i, 0, 0)),
                  pl.BlockSpec((1, d), const2), pl.BlockSpec((1, d), const2),
                  pl.BlockSpec(wugg.shape, const2),
                  pl.BlockSpec(band.shape, const3),
                  pl.BlockSpec(icnt.shape, const2),
                  pl.BlockSpec(pwbd.shape, const2),
                  pl.BlockSpec(pscale.shape, const2),
                  pl.BlockSpec(wpool.shape, const2),
                  pl.BlockSpec(onorm.shape, const2),
                  pl.BlockSpec(wgla.shape, const2),
                  pl.BlockSpec(wo.shape, const2),
                  pl.BlockSpec(wr.shape, const2),
                  pl.BlockSpec(br.shape, const2)],
        out_specs=(tok(d), pl.BlockSpec((tm * ROW_TILE, V7X_LANES), lambda i, j: (i * nj + j, 0)), lane, lane),
        compiler_params=pltpu.CompilerParams(dimension_semantics=("arbitrary", "arbitrary"),
                                             vmem_limit_bytes=V7X_SCOPED_VMEM_CAP),
        name="mix",
    )(x, o, mod3, n1, n2, wugg, band, icnt, pwbd, pscale, wpool, onorm, wgla, wo, wr, br)


def _moe_kernel(ta_ref, tb_ref, nv_ref, src0_ref, src1_ref, srcn0_ref, srcn1_ref, dstp0_ref, dstp1_ref, dstl_ref,
                wgt0_ref, wgt1_ref, h2_ref, w13a0_ref, w2a0_ref, w13b0_ref, w2b0_ref,
                w13a1_ref, w2a1_ref, w13b1_ref, w2b1_ref, y_ref, xbuf, ybuf, gsem, ssem):
    del ta_ref, tb_ref
    j = pl.program_id(0)
    last = pl.num_programs(0) - 1
    nv = nv_ref[0]
    tm = xbuf.shape[1] // ROW_TILE
    spare0 = y_ref.shape[0] - 2 * tm * ROW_TILE

    def token_tile(ref, off):
        return ref.at[pl.ds(pl.multiple_of(off, ROW_TILE), ROW_TILE)]

    def gather_row(off, s, r8):
        return pltpu.make_async_copy(token_tile(h2_ref, off), xbuf.at[s, pl.ds(r8, ROW_TILE)], gsem.at[s])

    def scatter_row(off, s, r8):
        return pltpu.make_async_copy(ybuf.at[s, pl.ds(r8, ROW_TILE)], token_tile(y_ref, off), ssem.at[s])

    def start_rows(make, idx_ref):
        def body(r, carry):
            make(idx_ref[0, 0, r], pl.multiple_of(r * ROW_TILE, ROW_TILE)).start()
            return carry
        lax.fori_loop(0, tm, body, 0, unroll=8)

    def gather_tile(s):
        return pltpu.make_async_copy(h2_ref.at[pl.ds(0, tm * ROW_TILE)], xbuf.at[s], gsem.at[s])

    def scatter_tile(s):
        return pltpu.make_async_copy(ybuf.at[s], y_ref.at[pl.ds(0, tm * ROW_TILE)], ssem.at[s])

    @pl.when(j == 0)
    def _():
        ybuf[...] = jnp.zeros(ybuf.shape, ybuf.dtype)
        for s in range(2):
            spare = pltpu.make_async_copy(ybuf.at[0], y_ref.at[pl.ds(spare0 + s * tm * ROW_TILE, tm * ROW_TILE)],
                                          ssem.at[0])
            spare.start()
            spare.wait()

        @pl.when(nv > 0)
        def _():
            start_rows(lambda off, r8: gather_row(off, 0, r8), src0_ref)
            start_rows(lambda off, r8: gather_row(off, 1, r8), src1_ref)

    n_pieces = 12
    per_piece = -(-tm // n_pieces)

    def tile_body(s, tile, srcn_ref, dstp_ref, wgt_ref, w13a_ref, w2a_ref, w13b_ref, w2b_ref):
        @pl.when(tile < nv)
        def _():
            gather_tile(s).wait()
            xt = _tiles_to_rows(xbuf.at[s]).astype(bf16)
            piece = [0]

            def issue_rows():
                lo = piece[0] * per_piece
                rows = range(lo, min(lo + per_piece, tm))
                src_rows = [srcn_ref[0, 0, r] for r in rows]
                dst_rows = [dstp_ref[0, 0, r] for r in rows]
                for r, sr, dr in zip(rows, src_rows, dst_rows):
                    gather_row(sr, s, r * ROW_TILE).start(priority=0)
                    scatter_row(dr, 1 - s, r * ROW_TILE).start(priority=1)
                piece[0] += 1

            w = wgt_ref[...]
            half = D_EXPERT // 2
            y = None
            for e, (w13_ref, w2_ref) in enumerate(((w13a_ref, w2a_ref), (w13b_ref, w2b_ref))):
                acc = None
                for c in range(2):
                    h1 = _dot(xt, w13_ref[0, :, c * half:(c + 1) * half])
                    issue_rows()
                    h3 = _dot(xt, w13_ref[0, :, D_EXPERT + c * half:D_EXPERT + (c + 1) * half])
                    issue_rows()
                    part = _dot((_silu(h1) * h3).astype(bf16), w2_ref[0, c * half:(c + 1) * half, :])
                    issue_rows()
                    acc = part if acc is None else acc + part
                y = w[:, e:e + 1] * acc if y is None else y + w[:, e:e + 1] * acc
            assert piece[0] == n_pieces

            @pl.when(tile >= 1)
            def _():
                scatter_tile(s).wait()

            _rows_to_tiles(ybuf.at[s], y)

    tile_body(0, 2 * j, srcn0_ref, dstp0_ref, wgt0_ref, w13a0_ref, w2a0_ref, w13b0_ref, w2b0_ref)
    tile_body(1, 2 * j + 1, srcn1_ref, dstp1_ref, wgt1_ref, w13a1_ref, w2a1_ref, w13b1_ref, w2b1_ref)

    @pl.when(jnp.logical_and(j == last, nv > 0))
    def _():
        sl = (nv - 1) % 2
        scatter_tile(1 - sl).wait()
        start_rows(lambda off, r8: scatter_row(off, sl, r8), dstl_ref)
        scatter_tile(sl).wait()
        gather_tile(0).wait()
        gather_tile(1).wait()


def _moe(tile_a, tile_b, nvalid, src, dst, wgt, h2, w13, w2):
    nt, _, tm = src.shape
    assert nt % 2 == 0 and dst.shape[0] == nt + 1
    rows, lanes = h2.shape
    d = ROW_TILE * lanes
    idx_blk = (1, 1, tm)
    smem_idx = lambda fn: pl.BlockSpec(idx_blk, fn, memory_space=pltpu.SMEM)
    w13_blk = (1, d, 2 * D_EXPERT)
    w2_blk = (1, D_EXPERT, d)

    def weights(s):
        return [pl.BlockSpec(w13_blk, lambda j, ta, tb, nv: (ta[2 * j + s], 0, 0)),
                pl.BlockSpec(w2_blk, lambda j, ta, tb, nv: (ta[2 * j + s], 0, 0)),
                pl.BlockSpec(w13_blk, lambda j, ta, tb, nv: (tb[2 * j + s], 0, 0)),
                pl.BlockSpec(w2_blk, lambda j, ta, tb, nv: (tb[2 * j + s], 0, 0))]

    grid_spec = pltpu.PrefetchScalarGridSpec(
        num_scalar_prefetch=3,
        grid=(nt // 2,),
        in_specs=[smem_idx(lambda j, ta, tb, nv: (0, 0, 0)),
                  smem_idx(lambda j, ta, tb, nv: (1, 0, 0)),
                  smem_idx(lambda j, ta, tb, nv: (jnp.minimum(2 * j + 2, nt - 1), 0, 0)),
                  smem_idx(lambda j, ta, tb, nv: (jnp.minimum(2 * j + 3, nt - 1), 0, 0)),
                  smem_idx(lambda j, ta, tb, nv: (jnp.where(j == 0, nt, 2 * j - 1), 0, 0)),
                  smem_idx(lambda j, ta, tb, nv: (2 * j, 0, 0)),
                  smem_idx(lambda j, ta, tb, nv: (jnp.maximum(nv[0] - 1, 0), 0, 0)),
                  pl.BlockSpec((tm, 2), lambda j, ta, tb, nv: (2 * j, 0)),
                  pl.BlockSpec((tm, 2), lambda j, ta, tb, nv: (2 * j + 1, 0)),
                  pl.BlockSpec(memory_space=pl.ANY)] + weights(0) + weights(1),
        out_specs=pl.BlockSpec(memory_space=pl.ANY),
        scratch_shapes=[pltpu.VMEM((2, tm * ROW_TILE, lanes), f32), pltpu.VMEM((2, tm * ROW_TILE, lanes), f32),
                        pltpu.SemaphoreType.DMA((2,)), pltpu.SemaphoreType.DMA((2,))],
    )
    return pl.pallas_call(
        _moe_kernel,
        out_shape=jax.ShapeDtypeStruct((rows + 2 * tm * ROW_TILE, lanes), f32),
        grid_spec=grid_spec,
        compiler_params=pltpu.CompilerParams(dimension_semantics=("arbitrary",),
                                             vmem_limit_bytes=_vmem_limit(48 * 1024 * 1024)),
        name="moe",
    )(tile_a, tile_b, nvalid, src, src, src, src, dst, dst, dst, wgt, wgt, h2, w13, w2, w13, w2, w13, w2, w13, w2)


def _pair_tables():
    a, b = [], []
    for g in range(N_GROUPS):
        for i in range(EXPERTS_PER_GROUP):
            for j in range(i + 1, EXPERTS_PER_GROUP):
                a.append(g * EXPERTS_PER_GROUP + i)
                b.append(g * EXPERTS_PER_GROUP + j)
    return np.asarray(a, np.int32), np.asarray(b, np.int32)


def _dispatch_plan(pid, wlo, whi, tm):
    t = pid.shape[0]
    nt = 2 * (-(-((t + N_PAIRS * (tm - 1)) // tm) // 2))
    order = jnp.argsort(pid, stable=True).astype(i32)
    starts = jnp.searchsorted(pid[order], jnp.arange(N_PAIRS + 1, dtype=i32), side="left").astype(i32)
    counts = starts[1:] - starts[:-1]
    tiles = (counts + tm - 1) // tm
    tile_end = jnp.cumsum(tiles)
    tile_start = tile_end - tiles
    nvalid = tile_end[-1:]
    tile = jnp.arange(nt, dtype=i32)
    grp = jnp.minimum(jnp.searchsorted(tile_end, tile, side="right"), N_PAIRS - 1).astype(i32)
    rank = (tile - tile_start[grp])[:, None] * tm + jnp.arange(tm, dtype=i32)[None, :]
    valid = jnp.logical_and(rank < counts[grp][:, None], (tile < nvalid[0])[:, None])
    tok = order[jnp.clip(starts[grp][:, None] + rank, 0, t - 1)]
    src = jnp.where(valid, tok, 0)
    row = jnp.arange(tm, dtype=i32)[None, :]
    dst = jnp.where(valid, tok, t + (tile % 2)[:, None] * tm + row)
    dst = jnp.concatenate([dst, t + tm + row], axis=0).reshape(nt + 1, 1, tm)
    wgt = jnp.stack([jnp.where(valid, wlo[src], 0.0), jnp.where(valid, whi[src], 0.0)], axis=-1).reshape(nt * tm, 2)
    pa, pb = _pair_tables()
    return (jnp.asarray(pa)[grp], jnp.asarray(pb)[grp], nvalid.astype(i32),
            src.reshape(nt, 1, tm) * ROW_TILE, dst * ROW_TILE, wgt)


def _final_kernel(x1_ref, y_ref, mod_ref, g_ref, out_ref):
    m = mod_ref[0]
    x2 = x1_ref[0] + m[5:6] * _tiles_to_rows(y_ref)
    ms = jnp.mean(x2 * x2, axis=-1, keepdims=True)
    out_ref[0] = x2 * lax.rsqrt(ms + EPS) * g_ref[...]


def _final(x1, y, mod3, g):
    b, s, d = x1.shape
    tm = TOK_TILE
    nj = s // tm
    tok = pl.BlockSpec((1, tm, d), lambda i, j: (i, j, 0))
    return pl.pallas_call(
        _final_kernel,
        out_shape=jax.ShapeDtypeStruct((b, s, d), f32),
        grid=(b, nj),
        in_specs=[tok,
                  pl.BlockSpec((tm * ROW_TILE, y.shape[1]), lambda i, j: (i * nj + j, 0)),
                  pl.BlockSpec((1, MOD_CHUNKS, d), lambda i, j: (i, 0, 0)),
                  pl.BlockSpec((1, d), lambda i, j: (0, 0))],
        out_specs=tok,
        compiler_params=pltpu.CompilerParams(dimension_semantics=("arbitrary", "arbitrary"),
                                             vmem_limit_bytes=_vmem_limit(32 * 1024 * 1024)),
        name="final",
    )(x1, y, mod3, g)


def _pool_constants(tm):
    slab = V7X_MXU_DIM
    band = np.zeros((len(POOL_WINDOWS), slab, slab), np.float32)
    icnt = np.zeros((tm, POOL_WIDTH), np.float32)
    for gi, w in enumerate(POOL_WINDOWS):
        for t in range(slab):
            base, pos = (t // GRID_W) * GRID_W, t % GRID_W
            lo, hi = max(pos - w // 2, 0), min(pos + w // 2, GRID_W)
            band[gi, t, base + lo:base + hi] = 1.0
        for t in range(tm):
            pos = t % GRID_W
            cnt = min(pos + w // 2, GRID_W) - max(pos - w // 2, 0)
            icnt[t, gi * POOL_GROUP_DIM:(gi + 1) * POOL_GROUP_DIM] = 1.0 / cnt
    return jnp.asarray(band, bf16), jnp.asarray(icnt, f32)


def _block_diag(blocks):
    n = len(blocks)
    r, c = blocks[0].shape
    out = jnp.zeros((n * r, n * c), blocks[0].dtype)
    for i, blk in enumerate(blocks):
        out = out.at[i * r:(i + 1) * r, i * c:(i + 1) * c].set(blk)
    return out


def kernel(x, c, ctx, c_ctx, w_mod, b_mod, norm1_g, norm2_g, w_in, gla_a2_f, gla_ab_f, gla_a2_b, gla_ab_b,
           gla_onorm_g, pool_w, pool_scale, w_pool_br, w_gla_br, w_o, router_grp_w, router_grp_b,
           router_exp_w, router_exp_b, moe_w1, moe_w3, moe_w2, final_norm_g):
    b, s, d = x.shape
    depth = w_mod.shape[0]
    assert depth == 1 and d == D_MODEL and s % TOK_TILE == 0 and s % (2 * GLA_CHUNK) == 0
    assert TOK_TILE % V7X_MXU_DIM == 0 and V7X_MXU_DIM % GRID_W == 0
    t = b * s

    cond = jnp.concatenate([c, c_ctx[None, :], jnp.zeros((COND_PAD - 1, d), f32)], axis=0)
    mod3 = _adaln(cond, w_mod[0], b_mod[0][None, :]).reshape(b + COND_PAD, MOD_CHUNKS, d)

    w_in_b = w_in[0].astype(bf16)
    wqkv = w_in_b[:, _Q0:_G0]
    wkv = w_in_b[:, _K0:_G0]
    wugg = jnp.concatenate([w_in_b[:, _U0:_Q0], w_in_b[:, _G0:_LR0]], axis=1)
    wlr = jnp.pad(w_in_b[:, _LR0:_IN_END], ((0, 0), (0, V7X_LANES - 2 * GLA_GATE_RANK)))
    a2bd = jnp.pad(_block_diag([gla_a2_f[0], gla_a2_b[0]]).astype(bf16),
                   ((0, V7X_LANES - 2 * GLA_GATE_RANK), (0, 0)))
    ab = jnp.concatenate([gla_ab_f[0], gla_ab_b[0]])[None, :]
    n1 = norm1_g[0][None, :]
    n2 = norm2_g[0][None, :]

    s0f, s0b = _ctx_states(ctx, mod3, b, n1, wkv, wlr, a2bd, ab)
    q, k, v, la = _qkv(x, mod3, n1, wqkv, wlr, a2bd, ab)
    o = _gla(q, k, v, la, s0f, s0b)

    band, icnt = _pool_constants(TOK_TILE)
    pwbd = _block_diag([pool_w[0, gi] for gi in range(len(POOL_WINDOWS))]).astype(bf16)
    onorm = jnp.tile(gla_onorm_g[0], GLA_HEADS)[None, :]
    wr = jnp.zeros((d, V7X_LANES), f32)
    wr = wr.at[:, :N_GROUPS].set(router_grp_w[0]).at[:, V7X_SUBLANES:V7X_SUBLANES + N_EXPERTS].set(router_exp_w[0])
    br = jnp.zeros((1, V7X_LANES), f32)
    br = br.at[0, :N_GROUPS].set(router_grp_b[0]).at[0, V7X_SUBLANES:V7X_SUBLANES + N_EXPERTS].set(router_exp_b[0])
    x1, h2, ri, rw = _mix(x, o, mod3, n1, n2, wugg, band, icnt, pwbd, pool_scale[0][None, :],
                           w_pool_br[0].astype(bf16), onorm, w_gla_br[0].astype(bf16), w_o[0].astype(bf16),
                           wr.astype(bf16), br)

    plan = _dispatch_plan(ri[0], rw[0], rw[1], MOE_TILE)
    w13 = jnp.concatenate([moe_w1[0], moe_w3[0]], axis=-1).astype(bf16)
    y = _moe(*plan, h2, w13, moe_w2[0].astype(bf16))
    return _final(x1, y, mod3, final_norm_g[None, :])
```

```python
import jax
import jax.numpy as jnp
import numpy as np
from jax import lax
from jax.experimental import pallas as pl
from jax.experimental.pallas import tpu as pltpu

f32 = jnp.float32
bf16 = jnp.bfloat16
i32 = jnp.int32
u32 = jnp.uint32

D_MODEL = 1024
GRID_W = 64
POOL_WINDOWS = (2, 4, 8, 16)
POOL_GROUP_DIM = 128
POOL_WIDTH = len(POOL_WINDOWS) * POOL_GROUP_DIM
GLA_HEADS = 4
GLA_DK = 512
GLA_DV = 1024
GLA_HK = GLA_DK // GLA_HEADS
GLA_HV = GLA_DV // GLA_HEADS
GLA_GATE_RANK = 16
GLA_GATE_NORMALIZER = 16.0
N_GROUPS = 4
EXPERTS_PER_GROUP = 8
N_EXPERTS = N_GROUPS * EXPERTS_PER_GROUP
D_EXPERT = 512
MOD_CHUNKS = 6
EPS = 1e-6

_U0, _Q0, _K0, _V0, _G0, _GT0, _LR0, _IN_END = 0, 512, 1024, 1536, 2560, 3584, 5632, 5664

V7X_LANES = 128
V7X_SUBLANES = 8
V7X_MXU_DIM = 256
V7X_SCOPED_VMEM_CAP = 60000 * 1024

GLA_CHUNK = 128
GLA_HALF = GLA_CHUNK // 2
TOK_TILE = 512
MOE_TILE = 256
COND_PAD = 8
N_PAIRS = N_GROUPS * (EXPERTS_PER_GROUP * (EXPERTS_PER_GROUP - 1) // 2)


def _vmem_limit(nbytes):
    return int(min(max(nbytes, 16 * 1024 * 1024), V7X_SCOPED_VMEM_CAP))


def _dot(a, b):
    return jnp.dot(a, b, preferred_element_type=f32)


def _dot_nt(a, b):
    return lax.dot_general(a, b, (((1,), (1,)), ((), ())), preferred_element_type=f32)


def _dot_tn(a, b):
    return lax.dot_general(a, b, (((0,), (0,)), ((), ())), preferred_element_type=f32)


def _sigmoid(x):
    return 1.0 / (1.0 + jnp.exp(-x))


def _silu(x):
    return x * _sigmoid(x)


def _log_sigmoid(z):
    return jnp.minimum(z, 0.0) - jnp.log1p(jnp.exp(-jnp.abs(z)))


def _norm_mod(x, g, scale, shift):
    ms = jnp.mean(x * x, axis=-1, keepdims=True)
    return (x * lax.rsqrt(ms + EPS) * g) * (1.0 + scale) + shift


def _tri_cumsum(tri, la):
    hi = la.astype(bf16)
    lo = (la - hi.astype(f32)).astype(bf16)
    return _dot(tri, hi) + _dot(tri, lo)


ROW_TILE = V7X_SUBLANES


def _rows_to_tiles(ref, val):
    n = val.shape[0]
    for a in range(ROW_TILE):
        ref[pl.ds(a, n, stride=ROW_TILE), :] = val[:, a * V7X_LANES:(a + 1) * V7X_LANES]


def _tiles_to_rows(ref):
    n = ref.shape[0] // ROW_TILE
    return jnp.concatenate([ref[pl.ds(a, n, stride=ROW_TILE), :] for a in range(ROW_TILE)], axis=1)


def _log_decays(hx, wlr_ref, a2_ref, ab_ref):
    lr = _dot(hx, wlr_ref[...])
    z = _dot(lr.astype(bf16), a2_ref[...]) + ab_ref[...]
    return _log_sigmoid(z) * (1.0 / GLA_GATE_NORMALIZER)


def _adaln_kernel(c_ref, w_ref, b_ref, o_ref):
    c = c_ref[...]
    o_ref[...] = _dot(_silu(c).astype(bf16), w_ref[...].astype(bf16)) + b_ref[...]


def _adaln(cond, w_mod, b_mod):
    rows, d = cond.shape
    n = w_mod.shape[1]
    tn = 512
    return pl.pallas_call(
        _adaln_kernel,
        out_shape=jax.ShapeDtypeStruct((rows, n), f32),
        grid=(n // tn,),
        in_specs=[pl.BlockSpec((rows, d), lambda j: (0, 0)),
                  pl.BlockSpec((d, tn), lambda j: (0, j)),
                  pl.BlockSpec((1, tn), lambda j: (0, j))],
        out_specs=pl.BlockSpec((rows, tn), lambda j: (0, j)),
        compiler_params=pltpu.CompilerParams(dimension_semantics=("arbitrary",)),
        name="adaln",
    )(cond, w_mod, b_mod)


def _ctx_kernel(ctx_ref, mod_ref, g_ref, wkv_ref, wlr_ref, a2_ref, ab_ref, sf_ref, sb_ref):
    x = ctx_ref[0]
    n = x.shape[0]
    m = mod_ref[0]
    hc = _norm_mod(x, g_ref[...], m[1:2], m[0:1]).astype(bf16)
    kv = _dot(hc, wkv_ref[...])
    la = _log_decays(hc, wlr_ref, a2_ref, ab_ref)
    row = lax.broadcasted_iota(i32, (n, n), 0)
    col = lax.broadcasted_iota(i32, (n, n), 1)
    lower = jnp.where(row >= col, 1.0, 0.0).astype(bf16)
    upper = jnp.where(col >= row, 1.0, 0.0).astype(bf16)
    cum_f = _tri_cumsum(lower, la[:, :GLA_DK])
    cum_b = _tri_cumsum(upper, la[:, GLA_DK:])
    k = kv[:, :GLA_DK]
    v = kv[:, GLA_DK:].astype(bf16)
    kr_f = (k * jnp.exp(cum_f[n - 1:n] - cum_f)).astype(bf16)
    kr_b = (k * jnp.exp(cum_b[0:1] - cum_b)).astype(bf16)
    for h in range(GLA_HEADS):
        ks = slice(h * GLA_HK, (h + 1) * GLA_HK)
        vs = slice(h * GLA_HV, (h + 1) * GLA_HV)
        sf_ref[0, h] = _dot_tn(kr_f[:, ks], v[:, vs])
        sb_ref[0, h] = _dot_tn(kr_b[:, ks], v[:, vs])


def _ctx_states(ctx, mod3, ctx_row, g, wkv, wlr, a2bd, ab):
    b, n, d = ctx.shape
    st = jax.ShapeDtypeStruct((b, GLA_HEADS, GLA_HK, GLA_HV), f32)
    const = lambda i: (0, 0)
    st_spec = pl.BlockSpec((1, GLA_HEADS, GLA_HK, GLA_HV), lambda i: (i, 0, 0, 0))
    return pl.pallas_call(
        _ctx_kernel,
        out_shape=(st, st),
        grid=(b,),
        in_specs=[pl.BlockSpec((1, n, d), lambda i: (i, 0, 0)),
                  pl.BlockSpec((1, MOD_CHUNKS, d), lambda i: (ctx_row, 0, 0)),
                  pl.BlockSpec((1, d), const),
                  pl.BlockSpec(wkv.shape, const),
                  pl.BlockSpec(wlr.shape, const),
                  pl.BlockSpec(a2bd.shape, const),
                  pl.BlockSpec(ab.shape, const)],
        out_specs=(st_spec, st_spec),
        compiler_params=pltpu.CompilerParams(dimension_semantics=("arbitrary",),
                                             vmem_limit_bytes=_vmem_limit(32 * 1024 * 1024)),
        name="ctx_states",
    )(ctx, mod3, g, wkv, wlr, a2bd, ab)


def _qkv_kernel(x_ref, mod_ref, g_ref, wqkv_ref, wlr_ref, a2_ref, ab_ref, q_ref, k_ref, v_ref, la_ref):
    m = mod_ref[0]
    hx = _norm_mod(x_ref[0], g_ref[...], m[1:2], m[0:1]).astype(bf16)
    p = _dot(hx, wqkv_ref[...])
    q_ref[0] = p[:, :GLA_DK].astype(bf16)
    k_ref[0] = p[:, GLA_DK:2 * GLA_DK].astype(bf16)
    v_ref[0] = p[:, 2 * GLA_DK:].astype(bf16)
    la_ref[0] = _log_decays(hx, wlr_ref, a2_ref, ab_ref)


def _qkv(x, mod3, g, wqkv, wlr, a2bd, ab):
    b, s, d = x.shape
    tm = TOK_TILE
    const = lambda i, j: (0, 0)
    tok = lambda w: pl.BlockSpec((1, tm, w), lambda i, j: (i, j, 0))
    return pl.pallas_call(
        _qkv_kernel,
        out_shape=(jax.ShapeDtypeStruct((b, s, GLA_DK), bf16), jax.ShapeDtypeStruct((b, s, GLA_DK), bf16),
                   jax.ShapeDtypeStruct((b, s, GLA_DV), bf16), jax.ShapeDtypeStruct((b, s, 2 * GLA_DK), f32)),
        grid=(b, s // tm),
        in_specs=[tok(d),
                  pl.BlockSpec((1, MOD_CHUNKS, d), lambda i, j: (i, 0, 0)),
                  pl.BlockSpec((1, d), const),
                  pl.BlockSpec(wqkv.shape, const),
                  pl.BlockSpec(wlr.shape, const),
                  pl.BlockSpec(a2bd.shape, const),
                  pl.BlockSpec(ab.shape, const)],
        out_specs=(tok(GLA_DK), tok(GLA_DK), tok(GLA_DV), tok(2 * GLA_DK)),
        compiler_params=pltpu.CompilerParams(dimension_semantics=("arbitrary", "arbitrary"),
                                             vmem_limit_bytes=_vmem_limit(48 * 1024 * 1024)),
        name="qkv",
    )(x, mod3, g, wqkv, wlr, a2bd, ab)


def _gla_kernel(q_ref, k_ref, v_ref, la_ref, s0f_ref, s0b_ref, o_ref, s_sc, oacc):
    seq = q_ref.shape[1]
    nc = seq // GLA_CHUNK
    c_len = GLA_CHUNK
    s_sc[0:GLA_HEADS] = s0f_ref[0]
    s_sc[GLA_HEADS:2 * GLA_HEADS] = s0b_ref[0]
    row = lax.broadcasted_iota(i32, (c_len, c_len), 0)
    col = lax.broadcasted_iota(i32, (c_len, c_len), 1)
    lower = row >= col
    upper = col >= row
    tri_l = jnp.where(lower, 1.0, 0.0).astype(bf16)
    tri_u = jnp.where(upper, 1.0, 0.0).astype(bf16)
    scale = GLA_HK ** -0.5

    def step(c_fwd, c_bwd, second_visit):
        prep = []
        tots = []
        for d, c, tri, r_row, t_row in ((0, c_fwd, tri_l, GLA_HALF - 1, c_len - 1), (1, c_bwd, tri_u, GLA_HALF, 0)):
            rows = pl.ds(pl.multiple_of(c * c_len, c_len), c_len)
            cum = _tri_cumsum(tri, la_ref[0, rows, d * GLA_DK:(d + 1) * GLA_DK])
            r = cum[r_row:r_row + 1]
            tot = cum[t_row:t_row + 1]
            qd = q_ref[0, rows, :].astype(f32) * (jnp.exp(cum - r) * scale)
            ki = k_ref[0, rows, :].astype(f32) * jnp.exp(r - cum)
            qs = (qd * jnp.exp(r)).astype(bf16)
            kr = (ki * jnp.exp(tot - r)).astype(bf16)
            prep.append((rows, qd.astype(bf16), qs, ki.astype(bf16), kr, v_ref[0, rows, :]))
            tots.extend(tot[:, h * GLA_HK:(h + 1) * GLA_HK] for h in range(GLA_HEADS))
        pad = jnp.zeros((c_len - 2 * GLA_HEADS, GLA_HK), f32)
        dec_cols = jnp.exp(jnp.concatenate(tots + [pad], axis=0).T)
        for d, mask in ((0, lower), (1, upper)):
            rows, qd, qs, ki, kr, vv = prep[d]
            for h in range(GLA_HEADS):
                j = d * GLA_HEADS + h
                ks = slice(h * GLA_HK, (h + 1) * GLA_HK)
                vs = slice(h * GLA_HV, (h + 1) * GLA_HV)
                a = jnp.where(mask, _dot_nt(qd[:, ks], ki[:, ks]), 0.0).astype(bf16)
                s_old = s_sc[j]
                o = _dot(jnp.concatenate([a, qs[:, ks]], axis=1),
                         jnp.concatenate([vv[:, vs], s_old.astype(bf16)], axis=0))
                s_sc[j] = dec_cols[:, j:j + 1] * s_old + _dot_tn(kr[:, ks], vv[:, vs])
                if second_visit:
                    o_ref[0, rows, vs] = (oacc[rows, vs] + o).astype(o_ref.dtype)
                else:
                    oacc[rows, vs] = o

    def first(i, carry):
        step(i, nc - 1 - i, False)
        return carry

    def second(i, carry):
        step(i, nc - 1 - i, True)
        return carry

    lax.fori_loop(0, nc // 2, first, 0)
    lax.fori_loop(nc // 2, nc, second, 0)


def _gla(q, k, v, la, s0f, s0b):
    b, s, _ = q.shape
    tok = lambda w: pl.BlockSpec((1, s, w), lambda i: (i, 0, 0))
    st_spec = pl.BlockSpec((1, GLA_HEADS, GLA_HK, GLA_HV), lambda i: (i, 0, 0, 0))
    return pl.pallas_call(
        _gla_kernel,
        out_shape=jax.ShapeDtypeStruct((b, s, GLA_DV), bf16),
        grid=(b,),
        in_specs=[tok(GLA_DK), tok(GLA_DK), tok(GLA_DV), tok(2 * GLA_DK), st_spec, st_spec],
        out_specs=tok(GLA_DV),
        scratch_shapes=[pltpu.VMEM((2 * GLA_HEADS, GLA_HK, GLA_HV), f32), pltpu.VMEM((s, GLA_DV), f32)],
        compiler_params=pltpu.CompilerParams(dimension_semantics=("arbitrary",),
                                             vmem_limit_bytes=V7X_SCOPED_VMEM_CAP),
        name="gla",
    )(q, k, v, la, s0f, s0b)


def _mix_kernel(x_ref, o_ref, mod_ref, n1_ref, n2_ref, wugg_ref, band_ref, icnt_ref, pwbd_ref, pscale_ref,
                wpool_ref, onorm_ref, wgla_ref, wo_ref, wr_ref, br_ref,
                x1_ref, h2_ref, ri_ref, rw_ref):
    x = x_ref[0]
    tm = x.shape[0]
    m = mod_ref[0]
    hx = _norm_mod(x, n1_ref[...], m[1:2], m[0:1]).astype(bf16)
    p = _dot(hx, wugg_ref[...])
    u = p[:, :POOL_WIDTH]
    g = p[:, POOL_WIDTH:POOL_WIDTH + GLA_DV]
    gates = p[:, POOL_WIDTH + GLA_DV:]

    ub = u.astype(bf16)
    slab = band_ref.shape[1]
    sums = []
    for s in range(tm // slab):
        rs = slice(s * slab, (s + 1) * slab)
        sums.append(jnp.concatenate(
            [_dot(band_ref[gi], ub[rs, gi * POOL_GROUP_DIM:(gi + 1) * POOL_GROUP_DIM])
             for gi in range(len(POOL_WINDOWS))], axis=1))
    pooled = jnp.concatenate(sums, axis=0) * icnt_ref[...] - u
    y_pool = _dot(pooled.astype(bf16), pwbd_ref[...]) * pscale_ref[...]
    y_pool = _dot(y_pool.astype(bf16), wpool_ref[...])

    o = o_ref[0].astype(f32)
    normed = []
    for h in range(GLA_HEADS):
        oh = o[:, h * GLA_HV:(h + 1) * GLA_HV]
        normed.append(oh * lax.rsqrt(jnp.mean(oh * oh, axis=-1, keepdims=True) + EPS))
    on = jnp.concatenate(normed, axis=1) * onorm_ref[...]
    y_gla = _dot((on * _silu(g)).astype(bf16), wgla_ref[...])

    mixed = _sigmoid(gates[:, :D_MODEL]) * y_pool + _sigmoid(gates[:, D_MODEL:]) * y_gla
    x1 = x + m[2:3] * _dot(mixed.astype(bf16), wo_ref[...])
    x1_ref[0] = x1

    h2f = _norm_mod(x1, n2_ref[...], m[4:5], m[3:4])
    _rows_to_tiles(h2_ref, h2f)
    h2 = h2f.astype(bf16)

    lt = (_dot(h2, wr_ref[...]) + br_ref[...]).T
    g0, g1, g2, g3 = lt[0:1], lt[1:2], lt[2:3], lt[3:4]
    gmax = jnp.maximum(jnp.maximum(g0, g1), jnp.maximum(g2, g3))
    gidx = jnp.where(g0 == gmax, 0, jnp.where(g1 == gmax, 1, jnp.where(g2 == gmax, 2, 3))).astype(i32)
    p_grp = 1.0 / (jnp.exp(g0 - gmax) + jnp.exp(g1 - gmax) + jnp.exp(g2 - gmax) + jnp.exp(g3 - gmax))
    e0 = V7X_SUBLANES
    ing = lt[e0 + 3 * EXPERTS_PER_GROUP:e0 + 4 * EXPERTS_PER_GROUP]
    for gi in (2, 1, 0):
        ing = jnp.where(gidx == gi, lt[e0 + gi * EXPERTS_PER_GROUP:e0 + (gi + 1) * EXPERTS_PER_GROUP], ing)
    rid = lax.broadcasted_iota(i32, ing.shape, 0)
    v1 = jnp.max(ing, axis=0, keepdims=True)
    i1 = jnp.min(jnp.where(ing == v1, rid, EXPERTS_PER_GROUP), axis=0, keepdims=True)
    rest = jnp.where(rid == i1, -jnp.inf, ing)
    v2 = jnp.max(rest, axis=0, keepdims=True)
    i2 = jnp.min(jnp.where(rest == v2, rid, EXPERTS_PER_GROUP), axis=0, keepdims=True)
    t = jnp.exp(v2 - v1)
    w1 = p_grp / (1.0 + t)
    w2 = p_grp * t / (1.0 + t)
    first_low = i1 < i2
    ilo = jnp.minimum(i1, i2)
    ihi = jnp.maximum(i1, i2)
    pair = ilo * (EXPERTS_PER_GROUP - 1) - ((ilo * (ilo - 1)) >> 1) + (ihi - ilo - 1)
    pid = gidx * (N_PAIRS // N_GROUPS) + pair
    ri_ref[...] = jnp.where(rid == 0, pid, 0)
    rw_ref[...] = jnp.where(rid == 0, jnp.where(first_low, w1, w2),
                            jnp.where(rid == 1, jnp.where(first_low, w2, w1), 0.0))


def _mix(x, o, mod3, n1, n2, wugg, band, icnt, pwbd, pscale, wpool, onorm, wgla, wo, wr, br):
    b, s, d = x.shape
    tm = TOK_TILE
    nj = s // tm
    t = b * s
    const2 = lambda i, j: (0, 0)
    const3 = lambda i, j: (0, 0, 0)
    tok = lambda w: pl.BlockSpec((1, tm, w), lambda i, j: (i, j, 0))
    lane = pl.BlockSpec((V7X_SUBLANES, tm), lambda i, j: (0, i * nj + j))
    return pl.pallas_call(
        _mix_kernel,
        out_shape=(jax.ShapeDtypeStruct((b, s, d), f32), jax.ShapeDtypeStruct((t * ROW_TILE, V7X_LANES), f32),
                   jax.ShapeDtypeStruct((V7X_SUBLANES, t), i32), jax.ShapeDtypeStruct((V7X_SUBLANES, t), f32)),
        grid=(b, nj),
        in_specs=[tok(d), tok(GLA_DV),
                  pl.BlockSpec((1, MOD_CHUNKS, d), lambda i, j: (i, 0, 0)),
                  pl.BlockSpec((1, d), const2), pl.BlockSpec((1, d), const2),
                  pl.BlockSpec(wugg.shape, const2),
                  pl.BlockSpec(band.shape, const3),
                  pl.BlockSpec(icnt.shape, const2),
                  pl.BlockSpec(pwbd.shape, const2),
                  pl.BlockSpec(pscale.shape, const2),
                  pl.BlockSpec(wpool.shape, const2),
                  pl.BlockSpec(onorm.shape, const2),
                  pl.BlockSpec(wgla.shape, const2),
                  pl.BlockSpec(wo.shape, const2),
                  pl.BlockSpec(wr.shape, const2),
                  pl.BlockSpec(br.shape, const2)],
        out_specs=(tok(d), pl.BlockSpec((tm * ROW_TILE, V7X_LANES), lambda i, j: (i * nj + j, 0)), lane, lane),
        compiler_params=pltpu.CompilerParams(dimension_semantics=("arbitrary", "arbitrary"),
                                             vmem_limit_bytes=V7X_SCOPED_VMEM_CAP),
        name="mix",
    )(x, o, mod3, n1, n2, wugg, band, icnt, pwbd, pscale, wpool, onorm, wgla, wo, wr, br)


def _moe_kernel(ta_ref, tb_ref, nv_ref, src0_ref, src1_ref, srcn0_ref, srcn1_ref, dstp0_ref, dstp1_ref, dstl_ref,
                wgt0_ref, wgt1_ref, h2_ref, w13a0_ref, w2a0_ref, w13b0_ref, w2b0_ref,
                w13a1_ref, w2a1_ref, w13b1_ref, w2b1_ref, y_ref, xbuf, ybuf, gsem, ssem):
    del ta_ref, tb_ref
    j = pl.program_id(0)
    last = pl.num_programs(0) - 1
    nv = nv_ref[0]
    tm = xbuf.shape[1] // ROW_TILE
    spare0 = y_ref.shape[0] - 2 * tm * ROW_TILE

    def token_tile(ref, off):
        return ref.at[pl.ds(pl.multiple_of(off, ROW_TILE), ROW_TILE)]

    def gather_row(off, s, r8):
        return pltpu.make_async_copy(token_tile(h2_ref, off), xbuf.at[s, pl.ds(r8, ROW_TILE)], gsem.at[s])

    def scatter_row(off, s, r8):
        return pltpu.make_async_copy(ybuf.at[s, pl.ds(r8, ROW_TILE)], token_tile(y_ref, off), ssem.at[s])

    def start_rows(make, idx_ref):
        def body(r, carry):
            make(idx_ref[0, 0, r], pl.multiple_of(r * ROW_TILE, ROW_TILE)).start()
            return carry
        lax.fori_loop(0, tm, body, 0, unroll=8)

    def gather_tile(s):
        return pltpu.make_async_copy(h2_ref.at[pl.ds(0, tm * ROW_TILE)], xbuf.at[s], gsem.at[s])

    def scatter_tile(s):
        return pltpu.make_async_copy(ybuf.at[s], y_ref.at[pl.ds(0, tm * ROW_TILE)], ssem.at[s])

    @pl.when(j == 0)
    def _():
        ybuf[...] = jnp.zeros(ybuf.shape, ybuf.dtype)
        for s in range(2):
            spare = pltpu.make_async_copy(ybuf.at[0], y_ref.at[pl.ds(spare0 + s * tm * ROW_TILE, tm * ROW_TILE)],
                                          ssem.at[0])
            spare.start()
            spare.wait()

        @pl.when(nv > 0)
        def _():
            start_rows(lambda off, r8: gather_row(off, 0, r8), src0_ref)
            start_rows(lambda off, r8: gather_row(off, 1, r8), src1_ref)

    n_pieces = 12
    per_piece = -(-tm // n_pieces)

    def tile_body(s, tile, srcn_ref, dstp_ref, wgt_ref, w13a_ref, w2a_ref, w13b_ref, w2b_ref):
        @pl.when(tile < nv)
        def _():
            gather_tile(s).wait()
            xt = _tiles_to_rows(xbuf.at[s]).astype(bf16)
            piece = [0]

            def issue_rows():
                lo = piece[0] * per_piece
                rows = range(lo, min(lo + per_piece, tm))
                src_rows = [srcn_ref[0, 0, r] for r in rows]
                dst_rows = [dstp_ref[0, 0, r] for r in rows]
                for r, sr, dr in zip(rows, src_rows, dst_rows):
                    gather_row(sr, s, r * ROW_TILE).start(priority=0)
                    scatter_row(dr, 1 - s, r * ROW_TILE).start(priority=1)
                piece[0] += 1

            w = wgt_ref[...]
            half = D_EXPERT // 2
            y = None
            for e, (w13_ref, w2_ref) in enumerate(((w13a_ref, w2a_ref), (w13b_ref, w2b_ref))):
                acc = None
                for c in range(2):
                    h1 = _dot(xt, w13_ref[0, :, c * half:(c + 1) * half])
                    issue_rows()
                    h3 = _dot(xt, w13_ref[0, :, D_EXPERT + c * half:D_EXPERT + (c + 1) * half])
                    issue_rows()
                    part = _dot((_silu(h1) * h3).astype(bf16), w2_ref[0, c * half:(c + 1) * half, :])
                    issue_rows()
                    acc = part if acc is None else acc + part
                y = w[:, e:e + 1] * acc if y is None else y + w[:, e:e + 1] * acc
            assert piece[0] == n_pieces

            @pl.when(tile >= 1)
            def _():
                scatter_tile(s).wait()

            _rows_to_tiles(ybuf.at[s], y)

    tile_body(0, 2 * j, srcn0_ref, dstp0_ref, wgt0_ref, w13a0_ref, w2a0_ref, w13b0_ref, w2b0_ref)
    tile_body(1, 2 * j + 1, srcn1_ref, dstp1_ref, wgt1_ref, w13a1_ref, w2a1_ref, w13b1_ref, w2b1_ref)

    @pl.when(jnp.logical_and(j == last, nv > 0))
    def _():
        sl = (nv - 1) % 2
        scatter_tile(1 - sl).wait()
        start_rows(lambda off, r8: scatter_row(off, sl, r8), dstl_ref)
        scatter_tile(sl).wait()
        gather_tile(0).wait()
        gather_tile(1).wait()


def _moe(tile_a, tile_b, nvalid, src, dst, wgt, h2, w13, w2):
    nt, _, tm = src.shape
    assert nt % 2 == 0 and dst.shape[0] == nt + 1
    rows, lanes = h2.shape
    d = ROW_TILE * lanes
    idx_blk = (1, 1, tm)
    smem_idx = lambda fn: pl.BlockSpec(idx_blk, fn, memory_space=pltpu.SMEM)
    w13_blk = (1, d, 2 * D_EXPERT)
    w2_blk = (1, D_EXPERT, d)

    def weights(s):
        return [pl.BlockSpec(w13_blk, lambda j, ta, tb, nv: (ta[2 * j + s], 0, 0)),
                pl.BlockSpec(w2_blk, lambda j, ta, tb, nv: (ta[2 * j + s], 0, 0)),
                pl.BlockSpec(w13_blk, lambda j, ta, tb, nv: (tb[2 * j + s], 0, 0)),
                pl.BlockSpec(w2_blk, lambda j, ta, tb, nv: (tb[2 * j + s], 0, 0))]

    grid_spec = pltpu.PrefetchScalarGridSpec(
        num_scalar_prefetch=3,
        grid=(nt // 2,),
        in_specs=[smem_idx(lambda j, ta, tb, nv: (0, 0, 0)),
                  smem_idx(lambda j, ta, tb, nv: (1, 0, 0)),
                  smem_idx(lambda j, ta, tb, nv: (jnp.minimum(2 * j + 2, nt - 1), 0, 0)),
                  smem_idx(lambda j, ta, tb, nv: (jnp.minimum(2 * j + 3, nt - 1), 0, 0)),
                  smem_idx(lambda j, ta, tb, nv: (jnp.where(j == 0, nt, 2 * j - 1), 0, 0)),
                  smem_idx(lambda j, ta, tb, nv: (2 * j, 0, 0)),
                  smem_idx(lambda j, ta, tb, nv: (jnp.maximum(nv[0] - 1, 0), 0, 0)),
                  pl.BlockSpec((tm, 2), lambda j, ta, tb, nv: (2 * j, 0)),
                  pl.BlockSpec((tm, 2), lambda j, ta, tb, nv: (2 * j + 1, 0)),
                  pl.BlockSpec(memory_space=pl.ANY)] + weights(0) + weights(1),
        out_specs=pl.BlockSpec(memory_space=pl.ANY),
        scratch_shapes=[pltpu.VMEM((2, tm * ROW_TILE, lanes), f32), pltpu.VMEM((2, tm * ROW_TILE, lanes), f32),
                        pltpu.SemaphoreType.DMA((2,)), pltpu.SemaphoreType.DMA((2,))],
    )
    return pl.pallas_call(
        _moe_kernel,
        out_shape=jax.ShapeDtypeStruct((rows + 2 * tm * ROW_TILE, lanes), f32),
        grid_spec=grid_spec,
        compiler_params=pltpu.CompilerParams(dimension_semantics=("arbitrary",),
                                             vmem_limit_bytes=_vmem_limit(48 * 1024 * 1024)),
        name="moe",
    )(tile_a, tile_b, nvalid, src, src, src, src, dst, dst, dst, wgt, wgt, h2, w13, w2, w13, w2, w13, w2, w13, w2)


def _pair_tables():
    a, b = [], []
    for g in range(N_GROUPS):
        for i in range(EXPERTS_PER_GROUP):
            for j in range(i + 1, EXPERTS_PER_GROUP):
                a.append(g * EXPERTS_PER_GROUP + i)
                b.append(g * EXPERTS_PER_GROUP + j)
    return np.asarray(a, np.int32), np.asarray(b, np.int32)


def _dispatch_plan(pid, wlo, whi, tm):
    t = pid.shape[0]
    nt = 2 * (-(-((t + N_PAIRS * (tm - 1)) // tm) // 2))
    order = jnp.argsort(pid, stable=True).astype(i32)
    starts = jnp.searchsorted(pid[order], jnp.arange(N_PAIRS + 1, dtype=i32), side="left").astype(i32)
    counts = starts[1:] - starts[:-1]
    tiles = (counts + tm - 1) // tm
    tile_end = jnp.cumsum(tiles)
    tile_start = tile_end - tiles
    nvalid = tile_end[-1:]
    tile = jnp.arange(nt, dtype=i32)
    grp = jnp.minimum(jnp.searchsorted(tile_end, tile, side="right"), N_PAIRS - 1).astype(i32)
    rank = (tile - tile_start[grp])[:, None] * tm + jnp.arange(tm, dtype=i32)[None, :]
    valid = jnp.logical_and(rank < counts[grp][:, None], (tile < nvalid[0])[:, None])
    tok = order[jnp.clip(starts[grp][:, None] + rank, 0, t - 1)]
    src = jnp.where(valid, tok, 0)
    row = jnp.arange(tm, dtype=i32)[None, :]
    dst = jnp.where(valid, tok, t + (tile % 2)[:, None] * tm + row)
    dst = jnp.concatenate([dst, t + tm + row], axis=0).reshape(nt + 1, 1, tm)
    wgt = jnp.stack([jnp.where(valid, wlo[src], 0.0), jnp.where(valid, whi[src], 0.0)], axis=-1).reshape(nt * tm, 2)
    pa, pb = _pair_tables()
    return (jnp.asarray(pa)[grp], jnp.asarray(pb)[grp], nvalid.astype(i32),
            src.reshape(nt, 1, tm) * ROW_TILE, dst * ROW_TILE, wgt)


def _final_kernel(x1_ref, y_ref, mod_ref, g_ref, out_ref):
    m = mod_ref[0]
    x2 = x1_ref[0] + m[5:6] * _tiles_to_rows(y_ref)
    ms = jnp.mean(x2 * x2, axis=-1, keepdims=True)
    out_ref[0] = x2 * lax.rsqrt(ms + EPS) * g_ref[...]


def _final(x1, y, mod3, g):
    b, s, d = x1.shape
    tm = TOK_TILE
    nj = s // tm
    tok = pl.BlockSpec((1, tm, d), lambda i, j: (i, j, 0))
    return pl.pallas_call(
        _final_kernel,
        out_shape=jax.ShapeDtypeStruct((b, s, d), f32),
        grid=(b, nj),
        in_specs=[tok,
                  pl.BlockSpec((tm * ROW_TILE, y.shape[1]), lambda i, j: (i * nj + j, 0)),
                  pl.BlockSpec((1, MOD_CHUNKS, d), lambda i, j: (i, 0, 0)),
                  pl.BlockSpec((1, d), lambda i, j: (0, 0))],
        out_specs=tok,
        compiler_params=pltpu.CompilerParams(dimension_semantics=("arbitrary", "arbitrary"),
                                             vmem_limit_bytes=_vmem_limit(32 * 1024 * 1024)),
        name="final",
    )(x1, y, mod3, g)


def _pool_constants(tm):
    slab = V7X_MXU_DIM
    band = np.zeros((len(POOL_WINDOWS), slab, slab), np.float32)
    icnt = np.zeros((tm, POOL_WIDTH), np.float32)
    for gi, w in enumerate(POOL_WINDOWS):
        for t in range(slab):
            base, pos = (t // GRID_W) * GRID_W, t % GRID_W
            lo, hi = max(pos - w // 2, 0), min(pos + w // 2, GRID_W)
            band[gi, t, base + lo:base + hi] = 1.0
        for t in range(tm):
            pos = t % GRID_W
            cnt = min(pos + w // 2, GRID_W) - max(pos - w // 2, 0)
            icnt[t, gi * POOL_GROUP_DIM:(gi + 1) * POOL_GROUP_DIM] = 1.0 / cnt
    return jnp.asarray(band, bf16), jnp.asarray(icnt, f32)


def _block_diag(blocks):
    n = len(blocks)
    r, c = blocks[0].shape
    out = jnp.zeros((n * r, n * c), blocks[0].dtype)
    for i, blk in enumerate(blocks):
        out = out.at[i * r:(i + 1) * r, i * c:(i + 1) * c].set(blk)
    return out


def kernel(x, c, ctx, c_ctx, w_mod, b_mod, norm1_g, norm2_g, w_in, gla_a2_f, gla_ab_f, gla_a2_b, gla_ab_b,
           gla_onorm_g, pool_w, pool_scale, w_pool_br, w_gla_br, w_o, router_grp_w, router_grp_b,
           router_exp_w, router_exp_b, moe_w1, moe_w3, moe_w2, final_norm_g):
    b, s, d = x.shape
    depth = w_mod.shape[0]
    assert depth == 1 and d == D_MODEL and s % TOK_TILE == 0 and s % (2 * GLA_CHUNK) == 0
    assert TOK_TILE % V7X_MXU_DIM == 0 and V7X_MXU_DIM % GRID_W == 0
    t = b * s

    cond = jnp.concatenate([c, c_ctx[None, :], jnp.zeros((COND_PAD - 1, d), f32)], axis=0)
    mod3 = _adaln(cond, w_mod[0], b_mod[0][None, :]).reshape(b + COND_PAD, MOD_CHUNKS, d)

    w_in_b = w_in[0].astype(bf16)
    wqkv = w_in_b[:, _Q0:_G0]
    wkv = w_in_b[:, _K0:_G0]
    wugg = jnp.concatenate([w_in_b[:, _U0:_Q0], w_in_b[:, _G0:_LR0]], axis=1)
    wlr = jnp.pad(w_in_b[:, _LR0:_IN_END], ((0, 0), (0, V7X_LANES - 2 * GLA_GATE_RANK)))
    a2bd = jnp.pad(_block_diag([gla_a2_f[0], gla_a2_b[0]]).astype(bf16),
                   ((0, V7X_LANES - 2 * GLA_GATE_RANK), (0, 0)))
    ab = jnp.concatenate([gla_ab_f[0], gla_ab_b[0]])[None, :]
    n1 = norm1_g[0][None, :]
    n2 = norm2_g[0][None, :]

    s0f, s0b = _ctx_states(ctx, mod3, b, n1, wkv, wlr, a2bd, ab)
    q, k, v, la = _qkv(x, mod3, n1, wqkv, wlr, a2bd, ab)
    o = _gla(q, k, v, la, s0f, s0b)

    band, icnt = _pool_constants(TOK_TILE)
    pwbd = _block_diag([pool_w[0, gi] for gi in range(len(POOL_WINDOWS))]).astype(bf16)
    onorm = jnp.tile(gla_onorm_g[0], GLA_HEADS)[None, :]
    wr = jnp.zeros((d, V7X_LANES), f32)
    wr = wr.at[:, :N_GROUPS].set(router_grp_w[0]).at[:, V7X_SUBLANES:V7X_SUBLANES + N_EXPERTS].set(router_exp_w[0])
    br = jnp.zeros((1, V7X_LANES), f32)
    br = br.at[0, :N_GROUPS].set(router_grp_b[0]).at[0, V7X_SUBLANES:V7X_SUBLANES + N_EXPERTS].set(router_exp_b[0])
    x1, h2, ri, rw = _mix(x, o, mod3, n1, n2, wugg, band, icnt, pwbd, pool_scale[0][None, :],
                           w_pool_br[0].astype(bf16), onorm, w_gla_br[0].astype(bf16), w_o[0].astype(bf16),
                           wr.astype(bf16), br)

    plan = _dispatch_plan(ri[0], rw[0], rw[1], MOE_TILE)
    w13 = jnp.concatenate([moe_w1[0], moe_w3[0]], axis=-1).astype(bf16)
    y = _moe(*plan, h2, w13, moe_w2[0].astype(bf16))
    return _final(x1, y, mod3, final_norm_g[None, :])
```

```python
import jax
import jax.numpy as jnp
import numpy as np
from jax import lax
from jax.experimental import pallas as pl
from jax.experimental.pallas import tpu as pltpu

f32 = jnp.float32
bf16 = jnp.bfloat16
i32 = jnp.int32
u32 = jnp.uint32

D_MODEL = 1024
GRID_W = 64
POOL_WINDOWS = (2, 4, 8, 16)
POOL_GROUP_DIM = 128
POOL_WIDTH = len(POOL_WINDOWS) * POOL_GROUP_DIM
GLA_HEADS = 4
GLA_DK = 512
GLA_DV = 1024
GLA_HK = GLA_DK // GLA_HEADS
GLA_HV = GLA_DV // GLA_HEADS
GLA_GATE_RANK = 16
GLA_GATE_NORMALIZER = 16.0
N_GROUPS = 4
EXPERTS_PER_GROUP = 8
N_EXPERTS = N_GROUPS * EXPERTS_PER_GROUP
D_EXPERT = 512
MOD_CHUNKS = 6
EPS = 1e-6

_U0, _Q0, _K0, _V0, _G0, _GT0, _LR0, _IN_END = 0, 512, 1024, 1536, 2560, 3584, 5632, 5664

V7X_LANES = 128
V7X_SUBLANES = 8
V7X_MXU_DIM = 256
V7X_SCOPED_VMEM_CAP = 60000 * 1024

GLA_CHUNK = 128
GLA_HALF = GLA_CHUNK // 2
TOK_TILE = 512
MOE_TILE = 128
COND_PAD = 8
N_PAIRS = N_GROUPS * (EXPERTS_PER_GROUP * (EXPERTS_PER_GROUP - 1) // 2)


def _vmem_limit(nbytes):
    return int(min(max(nbytes, 16 * 1024 * 1024), V7X_SCOPED_VMEM_CAP))


def _dot(a, b):
    return jnp.dot(a, b, preferred_element_type=f32)


def _dot_nt(a, b):
    return lax.dot_general(a, b, (((1,), (1,)), ((), ())), preferred_element_type=f32)


def _dot_tn(a, b):
    return lax.dot_general(a, b, (((0,), (0,)), ((), ())), preferred_element_type=f32)


def _sigmoid(x):
    return 1.0 / (1.0 + jnp.exp(-x))


def _silu(x):
    return x * _sigmoid(x)


def _log_sigmoid(z):
    return jnp.minimum(z, 0.0) - jnp.log1p(jnp.exp(-jnp.abs(z)))


def _norm_mod(x, g, scale, shift):
    ms = jnp.mean(x * x, axis=-1, keepdims=True)
    return (x * lax.rsqrt(ms + EPS) * g) * (1.0 + scale) + shift


def _tri_cumsum(tri, la):
    hi = la.astype(bf16)
    lo = (la - hi.astype(f32)).astype(bf16)
    return _dot(tri, hi) + _dot(tri, lo)


ROW_TILE = V7X_SUBLANES


def _rows_to_tiles(ref, val):
    n = val.shape[0]
    for a in range(ROW_TILE):
        ref[pl.ds(a, n, stride=ROW_TILE), :] = val[:, a * V7X_LANES:(a + 1) * V7X_LANES]


def _tiles_to_rows(ref):
    n = ref.shape[0] // ROW_TILE
    return jnp.concatenate([ref[pl.ds(a, n, stride=ROW_TILE), :] for a in range(ROW_TILE)], axis=1)


def _log_decays(hx, wlr_ref, a2_ref, ab_ref):
    lr = _dot(hx, wlr_ref[...])
    z = _dot(lr.astype(bf16), a2_ref[...]) + ab_ref[...]
    return _log_sigmoid(z) * (1.0 / GLA_GATE_NORMALIZER)


def _adaln_kernel(c_ref, w_ref, b_ref, o_ref):
    c = c_ref[...]
    o_ref[...] = _dot(_silu(c).astype(bf16), w_ref[...].astype(bf16)) + b_ref[...]


def _adaln(cond, w_mod, b_mod):
    rows, d = cond.shape
    n = w_mod.shape[1]
    tn = 512
    return pl.pallas_call(
        _adaln_kernel,
        out_shape=jax.ShapeDtypeStruct((rows, n), f32),
        grid=(n // tn,),
        in_specs=[pl.BlockSpec((rows, d), lambda j: (0, 0)),
                  pl.BlockSpec((d, tn), lambda j: (0, j)),
                  pl.BlockSpec((1, tn), lambda j: (0, j))],
        out_specs=pl.BlockSpec((rows, tn), lambda j: (0, j)),
        compiler_params=pltpu.CompilerParams(dimension_semantics=("arbitrary",)),
        name="adaln",
    )(cond, w_mod, b_mod)


def _ctx_kernel(ctx_ref, mod_ref, g_ref, wkv_ref, wlr_ref, a2_ref, ab_ref, sf_ref, sb_ref):
    x = ctx_ref[0]
    n = x.shape[0]
    m = mod_ref[0]
    hc = _norm_mod(x, g_ref[...], m[1:2], m[0:1]).astype(bf16)
    kv = _dot(hc, wkv_ref[...])
    la = _log_decays(hc, wlr_ref, a2_ref, ab_ref)
    row = lax.broadcasted_iota(i32, (n, n), 0)
    col = lax.broadcasted_iota(i32, (n, n), 1)
    lower = jnp.where(row >= col, 1.0, 0.0).astype(bf16)
    upper = jnp.where(col >= row, 1.0, 0.0).astype(bf16)
    cum_f = _tri_cumsum(lower, la[:, :GLA_DK])
    cum_b = _tri_cumsum(upper, la[:, GLA_DK:])
    k = kv[:, :GLA_DK]
    v = kv[:, GLA_DK:].astype(bf16)
    kr_f = (k * jnp.exp(cum_f[n - 1:n] - cum_f)).astype(bf16)
    kr_b = (k * jnp.exp(cum_b[0:1] - cum_b)).astype(bf16)
    for h in range(GLA_HEADS):
        ks = slice(h * GLA_HK, (h + 1) * GLA_HK)
        vs = slice(h * GLA_HV, (h + 1) * GLA_HV)
        sf_ref[0, h] = _dot_tn(kr_f[:, ks], v[:, vs])
        sb_ref[0, h] = _dot_tn(kr_b[:, ks], v[:, vs])


def _ctx_states(ctx, mod3, ctx_row, g, wkv, wlr, a2bd, ab):
    b, n, d = ctx.shape
    st = jax.ShapeDtypeStruct((b, GLA_HEADS, GLA_HK, GLA_HV), f32)
    const = lambda i: (0, 0)
    st_spec = pl.BlockSpec((1, GLA_HEADS, GLA_HK, GLA_HV), lambda i: (i, 0, 0, 0))
    return pl.pallas_call(
        _ctx_kernel,
        out_shape=(st, st),
        grid=(b,),
        in_specs=[pl.BlockSpec((1, n, d), lambda i: (i, 0, 0)),
                  pl.BlockSpec((1, MOD_CHUNKS, d), lambda i: (ctx_row, 0, 0)),
                  pl.BlockSpec((1, d), const),
                  pl.BlockSpec(wkv.shape, const),
                  pl.BlockSpec(wlr.shape, const),
                  pl.BlockSpec(a2bd.shape, const),
                  pl.BlockSpec(ab.shape, const)],
        out_specs=(st_spec, st_spec),
        compiler_params=pltpu.CompilerParams(dimension_semantics=("arbitrary",),
                                             vmem_limit_bytes=_vmem_limit(32 * 1024 * 1024)),
        name="ctx_states",
    )(ctx, mod3, g, wkv, wlr, a2bd, ab)


def _qkv_kernel(x_ref, mod_ref, g_ref, wqkv_ref, wlr_ref, a2_ref, ab_ref, q_ref, k_ref, v_ref, la_ref):
    m = mod_ref[0]
    half = x_ref.shape[1] // 2
    for rows in (slice(0, half), slice(half, 2 * half)):
        hx = _norm_mod(x_ref[0, rows, :], g_ref[...], m[1:2], m[0:1]).astype(bf16)
        p = _dot(hx, wqkv_ref[...])
        q_ref[0, rows, :] = p[:, :GLA_DK].astype(bf16)
        k_ref[0, rows, :] = p[:, GLA_DK:2 * GLA_DK].astype(bf16)
        v_ref[0, rows, :] = p[:, 2 * GLA_DK:].astype(bf16)
        la_ref[0, rows, :] = _log_decays(hx, wlr_ref, a2_ref, ab_ref)


def _qkv(x, mod3, g, wqkv, wlr, a2bd, ab):
    b, s, d = x.shape
    tm = TOK_TILE
    const = lambda i, j: (0, 0)
    tok = lambda w: pl.BlockSpec((1, tm, w), lambda i, j: (i, j, 0))
    return pl.pallas_call(
        _qkv_kernel,
        out_shape=(jax.ShapeDtypeStruct((b, s, GLA_DK), bf16), jax.ShapeDtypeStruct((b, s, GLA_DK), bf16),
                   jax.ShapeDtypeStruct((b, s, GLA_DV), bf16), jax.ShapeDtypeStruct((b, s, 2 * GLA_DK), f32)),
        grid=(b, s // tm),
        in_specs=[tok(d),
                  pl.BlockSpec((1, MOD_CHUNKS, d), lambda i, j: (i, 0, 0)),
                  pl.BlockSpec((1, d), const),
                  pl.BlockSpec(wqkv.shape, const),
                  pl.BlockSpec(wlr.shape, const),
                  pl.BlockSpec(a2bd.shape, const),
                  pl.BlockSpec(ab.shape, const)],
        out_specs=(tok(GLA_DK), tok(GLA_DK), tok(GLA_DV), tok(2 * GLA_DK)),
        compiler_params=pltpu.CompilerParams(dimension_semantics=("arbitrary", "arbitrary"),
                                             vmem_limit_bytes=_vmem_limit(48 * 1024 * 1024)),
        name="qkv",
    )(x, mod3, g, wqkv, wlr, a2bd, ab)


def _gla_kernel(q_ref, k_ref, v_ref, la_ref, s0f_ref, s0b_ref, o_ref, s_sc, oacc, a_sc, ops_a, ops_b, dec_a, dec_b):
    seq = q_ref.shape[1]
    nc = seq // GLA_CHUNK
    c_len = GLA_CHUNK
    s_sc[0:GLA_HEADS] = s0f_ref[0]
    s_sc[GLA_HEADS:2 * GLA_HEADS] = s0b_ref[0]
    row = lax.broadcasted_iota(i32, (c_len, c_len), 0)
    col = lax.broadcasted_iota(i32, (c_len, c_len), 1)
    lower = row >= col
    upper = col >= row
    tri_l = jnp.where(lower, 1.0, 0.0).astype(bf16)
    tri_u = jnp.where(upper, 1.0, 0.0).astype(bf16)
    scale = GLA_HK ** -0.5

    n_hd = 2 * GLA_HEADS
    QD, QS, KI, KR = range(4)

    def chunk_rows(c):
        return pl.ds(pl.multiple_of(c * c_len, c_len), c_len)

    def chunks_of(step):
        return jnp.minimum(step, nc - 1), jnp.maximum(nc - 1 - step, 0)

    def prepare(step, ops, dec):
        tots = []
        for d, c, tri, r_row, t_row in zip((0, 1), chunks_of(step), (tri_l, tri_u), (GLA_HALF - 1, GLA_HALF),
                                           (c_len - 1, 0)):
            rows = chunk_rows(c)
            cum = _tri_cumsum(tri, la_ref[0, rows, d * GLA_DK:(d + 1) * GLA_DK])
            r = cum[r_row:r_row + 1]
            tot = cum[t_row:t_row + 1]
            qd = q_ref[0, rows, :].astype(f32) * (jnp.exp(cum - r) * scale)
            ki = k_ref[0, rows, :].astype(f32) * jnp.exp(r - cum)
            ops[4 * d + QD] = qd.astype(bf16)
            ops[4 * d + QS] = (qd * jnp.exp(r)).astype(bf16)
            ops[4 * d + KI] = ki.astype(bf16)
            ops[4 * d + KR] = (ki * jnp.exp(tot - r)).astype(bf16)
            tots.extend(tot[:, h * GLA_HK:(h + 1) * GLA_HK] for h in range(GLA_HEADS))
        pad = jnp.zeros((c_len - n_hd, GLA_HK), f32)
        dec[...] = jnp.exp(jnp.concatenate(tots + [pad], axis=0).T)

    def matmuls(step, ops, dec, second_visit):
        rows = [chunk_rows(c) for c in chunks_of(step)]
        for d, mask in ((0, lower), (1, upper)):
            for h in range(GLA_HEADS):
                ks = slice(h * GLA_HK, (h + 1) * GLA_HK)
                a = _dot_nt(ops[4 * d + QD, :, ks], ops[4 * d + KI, :, ks])
                a_sc[d * GLA_HEADS + h] = jnp.where(mask, a, 0.0).astype(bf16)
        for d in (0, 1):
            for h in range(GLA_HEADS):
                j = d * GLA_HEADS + h
                ks = slice(h * GLA_HK, (h + 1) * GLA_HK)
                vs = slice(h * GLA_HV, (h + 1) * GLA_HV)
                o = _dot(jnp.concatenate([a_sc[j], ops[4 * d + QS, :, ks]], axis=1),
                         jnp.concatenate([v_ref[0, rows[d], vs], s_sc[j].astype(bf16)], axis=0))
                if second_visit:
                    o_ref[0, rows[d], vs] = (oacc[rows[d], vs] + o).astype(o_ref.dtype)
                else:
                    oacc[rows[d], vs] = o
        for d in (0, 1):
            for h in range(GLA_HEADS):
                j = d * GLA_HEADS + h
                ks = slice(h * GLA_HK, (h + 1) * GLA_HK)
                vs = slice(h * GLA_HV, (h + 1) * GLA_HV)
                s_sc[j] = dec[:, j:j + 1] * s_sc[j] + _dot_tn(ops[4 * d + KR, :, ks], v_ref[0, rows[d], vs])

    def step_pair(second_visit):
        def body(m, carry):
            prepare(2 * m + 1, ops_b, dec_b)
            matmuls(2 * m, ops_a, dec_a, second_visit)
            prepare(2 * m + 2, ops_a, dec_a)
            matmuls(2 * m + 1, ops_b, dec_b, second_visit)
            return carry
        return body

    prepare(0, ops_a, dec_a)
    lax.fori_loop(0, nc // 4, step_pair(False), 0)
    lax.fori_loop(nc // 4, nc // 2, step_pair(True), 0)


def _gla(q, k, v, la, s0f, s0b):
    b, s, _ = q.shape
    tok = lambda w: pl.BlockSpec((1, s, w), lambda i: (i, 0, 0))
    st_spec = pl.BlockSpec((1, GLA_HEADS, GLA_HK, GLA_HV), lambda i: (i, 0, 0, 0))
    return pl.pallas_call(
        _gla_kernel,
        out_shape=jax.ShapeDtypeStruct((b, s, GLA_DV), bf16),
        grid=(b,),
        in_specs=[tok(GLA_DK), tok(GLA_DK), tok(GLA_DV), tok(2 * GLA_DK), st_spec, st_spec],
        out_specs=tok(GLA_DV),
        scratch_shapes=[pltpu.VMEM((2 * GLA_HEADS, GLA_HK, GLA_HV), f32),
                        pltpu.VMEM((s, GLA_DV), f32),
                        pltpu.VMEM((2 * GLA_HEADS, GLA_CHUNK, GLA_CHUNK), bf16),
                        pltpu.VMEM((8, GLA_CHUNK, GLA_DK), bf16), pltpu.VMEM((8, GLA_CHUNK, GLA_DK), bf16),
                        pltpu.VMEM((GLA_CHUNK, GLA_HK), f32), pltpu.VMEM((GLA_CHUNK, GLA_HK), f32)],
        compiler_params=pltpu.CompilerParams(dimension_semantics=("arbitrary",),
                                             vmem_limit_bytes=V7X_SCOPED_VMEM_CAP),
        name="gla",
    )(q, k, v, la, s0f, s0b)


def _mix_kernel(x_ref, o_ref, mod_ref, n1_ref, n2_ref, wugg_ref, band_ref, icnt_ref, pwbd_ref, pscale_ref,
                wpool_ref, onorm_ref, wgla_ref, wo_ref, wr_ref, br_ref,
                x1_ref, h2_ref, ri_ref, rw_ref):
    x = x_ref[0]
    tm = x.shape[0]
    m = mod_ref[0]
    hx = _norm_mod(x, n1_ref[...], m[1:2], m[0:1]).astype(bf16)
    p = _dot(hx, wugg_ref[...])
    u = p[:, :POOL_WIDTH]
    g = p[:, POOL_WIDTH:POOL_WIDTH + GLA_DV]
    gates = p[:, POOL_WIDTH + GLA_DV:]

    ub = u.astype(bf16)
    slab = band_ref.shape[1]
    sums = []
    for s in range(tm // slab):
        rs = slice(s * slab, (s + 1) * slab)
        sums.append(jnp.concatenate(
            [_dot(band_ref[gi], ub[rs, gi * POOL_GROUP_DIM:(gi + 1) * POOL_GROUP_DIM])
             for gi in range(len(POOL_WINDOWS))], axis=1))
    pooled = jnp.concatenate(sums, axis=0) * icnt_ref[...] - u
    y_pool = _dot(pooled.astype(bf16), pwbd_ref[...]) * pscale_ref[...]
    y_pool = _dot(y_pool.astype(bf16), wpool_ref[...])

    o = o_ref[0].astype(f32)
    normed = []
    for h in range(GLA_HEADS):
        oh = o[:, h * GLA_HV:(h + 1) * GLA_HV]
        normed.append(oh * lax.rsqrt(jnp.mean(oh * oh, axis=-1, keepdims=True) + EPS))
    on = jnp.concatenate(normed, axis=1) * onorm_ref[...]
    y_gla = _dot((on * _silu(g)).astype(bf16), wgla_ref[...])

    mixed = _sigmoid(gates[:, :D_MODEL]) * y_pool + _sigmoid(gates[:, D_MODEL:]) * y_gla
    x1 = x + m[2:3] * _dot(mixed.astype(bf16), wo_ref[...])
    x1_ref[0] = x1

    h2f = _norm_mod(x1, n2_ref[...], m[4:5], m[3:4])
    _rows_to_tiles(h2_ref, h2f)
    h2 = h2f.astype(bf16)

    lt = (_dot(h2, wr_ref[...]) + br_ref[...]).T
    g0, g1, g2, g3 = lt[0:1], lt[1:2], lt[2:3], lt[3:4]
    gmax = jnp.maximum(jnp.maximum(g0, g1), jnp.maximum(g2, g3))
    gidx = jnp.where(g0 == gmax, 0, jnp.where(g1 == gmax, 1, jnp.where(g2 == gmax, 2, 3))).astype(i32)
    p_grp = 1.0 / (jnp.exp(g0 - gmax) + jnp.exp(g1 - gmax) + jnp.exp(g2 - gmax) + jnp.exp(g3 - gmax))
    e0 = V7X_SUBLANES
    ing = lt[e0 + 3 * EXPERTS_PER_GROUP:e0 + 4 * EXPERTS_PER_GROUP]
    for gi in (2, 1, 0):
        ing = jnp.where(gidx == gi, lt[e0 + gi * EXPERTS_PER_GROUP:e0 + (gi + 1) * EXPERTS_PER_GROUP], ing)
    rid = lax.broadcasted_iota(i32, ing.shape, 0)
    v1 = jnp.max(ing, axis=0, keepdims=True)
    i1 = jnp.min(jnp.where(ing == v1, rid, EXPERTS_PER_GROUP), axis=0, keepdims=True)
    rest = jnp.where(rid == i1, -jnp.inf, ing)
    v2 = jnp.max(rest, axis=0, keepdims=True)
    i2 = jnp.min(jnp.where(rest == v2, rid, EXPERTS_PER_GROUP), axis=0, keepdims=True)
    t = jnp.exp(v2 - v1)
    w1 = p_grp / (1.0 + t)
    w2 = p_grp * t / (1.0 + t)
    first_low = i1 < i2
    ilo = jnp.minimum(i1, i2)
    ihi = jnp.maximum(i1, i2)
    pair = ilo * (EXPERTS_PER_GROUP - 1) - ((ilo * (ilo - 1)) >> 1) + (ihi - ilo - 1)
    pid = gidx * (N_PAIRS // N_GROUPS) + pair
    ri_ref[...] = jnp.where(rid == 0, pid, 0)
    rw_ref[...] = jnp.where(rid == 0, jnp.where(first_low, w1, w2),
                            jnp.where(rid == 1, jnp.where(first_low, w2, w1), 0.0))


def _mix(x, o, mod3, n1, n2, wugg, band, icnt, pwbd, pscale, wpool, onorm, wgla, wo, wr, br):
    b, s, d = x.shape
    tm = TOK_TILE
    nj = s // tm
    t = b * s
    const2 = lambda i, j: (0, 0)
    const3 = lambda i, j: (0, 0, 0)
    tok = lambda w: pl.BlockSpec((1, tm, w), lambda i, j: (i, j, 0))
    lane = pl.BlockSpec((V7X_SUBLANES, tm), lambda i, j: (0, i * nj + j))
    return pl.pallas_call(
        _mix_kernel,
        out_shape=(jax.ShapeDtypeStruct((b, s, d), f32), jax.ShapeDtypeStruct((t * ROW_TILE, V7X_LANES), f32),
                   jax.ShapeDtypeStruct((V7X_SUBLANES, t), i32), jax.ShapeDtypeStruct((V7X_SUBLANES, t), f32)),
        grid=(b, nj),
        in_specs=[tok(d), tok(GLA_DV),
                  pl.BlockSpec((1, MOD_CHUNKS, d), lambda i, j: (i, 0, 0)),
                  pl.BlockSpec((1, d), const2), pl.BlockSpec((1, d), const2),
                  pl.BlockSpec(wugg.shape, const2),
                  pl.BlockSpec(band.shape, const3),
                  pl.BlockSpec(icnt.shape, const2),
                  pl.BlockSpec(pwbd.shape, const2),
                  pl.BlockSpec(pscale.shape, const2),
                  pl.BlockSpec(wpool.shape, const2),
                  pl.BlockSpec(onorm.shape, const2),
                  pl.BlockSpec(wgla.shape, const2),
                  pl.BlockSpec(wo.shape, const2),
                  pl.BlockSpec(wr.shape, const2),
                  pl.BlockSpec(br.shape, const2)],
        out_specs=(tok(d), pl.BlockSpec((tm * ROW_TILE, V7X_LANES), lambda i, j: (i * nj + j, 0)), lane, lane),
        compiler_params=pltpu.CompilerParams(dimension_semantics=("arbitrary", "arbitrary"),
                                             vmem_limit_bytes=V7X_SCOPED_VMEM_CAP),
        name="mix",
    )(x, o, mod3, n1, n2, wugg, band, icnt, pwbd, pscale, wpool, onorm, wgla, wo, wr, br)


def _moe_kernel(ta_ref, tb_ref, g0_ref, xs_ref, xg_ref, pa_ref, pb_ref,
                idx0_ref, idx1_ref, idxn0_ref, idxn1_ref, idxp0_ref, idxp1_ref, idxl_ref,
                wgt0_ref, wgt1_ref, h2_ref, w13_any, w2_any, w13a0_ref, w2a0_ref, w13b0_ref, w2b0_ref,
                w13a1_ref, w2a1_ref, w13b1_ref, w2b1_ref, y_ref,
                xbuf, ybuf, gsem, ssem, xt_sc, yacc, wx13, wx2, wsem):
    del ta_ref, tb_ref
    j = pl.program_id(0)
    last = pl.num_programs(0) - 1
    tm = xbuf.shape[1] // ROW_TILE

    def token_tile(ref, off):
        return ref.at[pl.ds(pl.multiple_of(off, ROW_TILE), ROW_TILE)]

    def gather_row(off, s, r8):
        return pltpu.make_async_copy(token_tile(h2_ref, off), xbuf.at[s, pl.ds(r8, ROW_TILE)], gsem.at[s])

    def scatter_row(off, s, r8):
        return pltpu.make_async_copy(ybuf.at[s, pl.ds(r8, ROW_TILE)], token_tile(y_ref, off), ssem.at[s])

    def start_rows(make, idx_ref):
        def body(r, carry):
            make(idx_ref[0, 0, r], pl.multiple_of(r * ROW_TILE, ROW_TILE)).start()
            return carry
        lax.fori_loop(0, tm, body, 0, unroll=8)

    def gather_tile(s):
        return pltpu.make_async_copy(h2_ref.at[pl.ds(0, tm * ROW_TILE)], xbuf.at[s], gsem.at[s])

    def scatter_tile(s):
        return pltpu.make_async_copy(ybuf.at[s], y_ref.at[pl.ds(0, tm * ROW_TILE)], ssem.at[s])

    @pl.when(j == 0)
    def _():
        ybuf[...] = jnp.zeros(ybuf.shape, ybuf.dtype)
        start_rows(lambda off, r8: gather_row(off, 0, r8), idx0_ref)
        start_rows(lambda off, r8: gather_row(off, 1, r8), idx1_ref)

    n_pieces = 12
    per_piece = -(-tm // n_pieces)
    half = D_EXPERT // 2

    def pair_pass(xt, w_first, w_second, weights, between):
        y = None
        for w_col, (w13, w2) in zip((w_first, w_second), weights):
            acc = None
            for c in range(2):
                h1 = _dot(xt, w13(c * half, (c + 1) * half))
                between()
                h3 = _dot(xt, w13(D_EXPERT + c * half, D_EXPERT + (c + 1) * half))
                between()
                part = _dot((_silu(h1) * h3).astype(bf16), w2(c * half, (c + 1) * half))
                between()
                acc = part if acc is None else acc + part
            y = w_col * acc if y is None else y + w_col * acc
        return y

    def pair_weights(wgt_ref, pair):
        w = wgt_ref[...]
        mine = w[:, 2:3] == pair.astype(f32)
        return jnp.where(mine, w[:, 0:1], 0.0), jnp.where(mine, w[:, 1:2], 0.0)

    def tile_body(s, tile, idxn_ref, idxp_ref, wgt_ref, w13a_ref, w2a_ref, w13b_ref, w2b_ref):
        gather_tile(s).wait()
        xt = _tiles_to_rows(xbuf.at[s]).astype(bf16)
        xt_sc[...] = xt
        piece = [0]

        def issue_rows():
            lo = piece[0] * per_piece
            rows = range(lo, min(lo + per_piece, tm))
            src_rows = [idxn_ref[0, 0, r] for r in rows]
            dst_rows = [idxp_ref[0, 0, r] for r in rows]
            for r, sr, dr in zip(rows, src_rows, dst_rows):
                gather_row(sr, s, r * ROW_TILE).start(priority=0)
                scatter_row(dr, 1 - s, r * ROW_TILE).start(priority=1)
            piece[0] += 1

        block = lambda w13_ref, w2_ref: (lambda c0, c1: w13_ref[0, :, c0:c1], lambda r0, r1: w2_ref[0, r0:r1, :])
        w_first, w_second = pair_weights(wgt_ref, g0_ref[tile])
        yacc[...] = pair_pass(xt, w_first, w_second, (block(w13a_ref, w2a_ref), block(w13b_ref, w2b_ref)), issue_rows)
        assert piece[0] == n_pieces

        first_extra = xs_ref[tile]

        def extra_pass(k, carry):
            pair = xg_ref[first_extra + k]
            copies = [pltpu.make_async_copy(src.at[e], dst.at[slot], wsem.at[n])
                      for n, (src, dst, e, slot) in enumerate(((w13_any, wx13, pa_ref[pair], 0),
                                                               (w2_any, wx2, pa_ref[pair], 0),
                                                               (w13_any, wx13, pb_ref[pair], 1),
                                                               (w2_any, wx2, pb_ref[pair], 1)))]
            for cp in copies:
                cp.start()
            for cp in copies:
                cp.wait()
            copied = lambda slot: (lambda c0, c1: wx13[slot, :, c0:c1], lambda r0, r1: wx2[slot, r0:r1, :])
            w_lo, w_hi = pair_weights(wgt_ref, pair)
            yacc[...] += pair_pass(xt_sc[...], w_lo, w_hi, (copied(0), copied(1)), lambda: None)
            return carry

        lax.fori_loop(0, xs_ref[tile + 1] - first_extra, extra_pass, 0)

        @pl.when(tile >= 1)
        def _():
            scatter_tile(s).wait()

        _rows_to_tiles(ybuf.at[s], yacc[...])

    tile_body(0, 2 * j, idxn0_ref, idxp0_ref, wgt0_ref, w13a0_ref, w2a0_ref, w13b0_ref, w2b0_ref)
    tile_body(1, 2 * j + 1, idxn1_ref, idxp1_ref, wgt1_ref, w13a1_ref, w2a1_ref, w13b1_ref, w2b1_ref)

    @pl.when(j == last)
    def _():
        scatter_tile(0).wait()
        start_rows(lambda off, r8: scatter_row(off, 1, r8), idxl_ref)
        scatter_tile(1).wait()
        gather_tile(0).wait()
        gather_tile(1).wait()


def _moe(tile_a, tile_b, first_pair, extra_start, extra_pair, pair_a, pair_b, idx, wgt, h2, w13, w2):
    nt, tm = idx.shape[0] - 1, idx.shape[2]
    assert nt % 2 == 0
    rows, lanes = h2.shape
    d = ROW_TILE * lanes
    idx_blk = (1, 1, tm)
    smem_idx = lambda fn: pl.BlockSpec(idx_blk, lambda j, *_: fn(j), memory_space=pltpu.SMEM)
    w13_blk = (1, d, 2 * D_EXPERT)
    w2_blk = (1, D_EXPERT, d)

    def weights(s):
        return [pl.BlockSpec(w13_blk, lambda j, ta, tb, *_: (ta[2 * j + s], 0, 0)),
                pl.BlockSpec(w2_blk, lambda j, ta, tb, *_: (ta[2 * j + s], 0, 0)),
                pl.BlockSpec(w13_blk, lambda j, ta, tb, *_: (tb[2 * j + s], 0, 0)),
                pl.BlockSpec(w2_blk, lambda j, ta, tb, *_: (tb[2 * j + s], 0, 0))]

    any_space = pl.BlockSpec(memory_space=pl.ANY)
    grid_spec = pltpu.PrefetchScalarGridSpec(
        num_scalar_prefetch=7,
        grid=(nt // 2,),
        in_specs=[smem_idx(lambda j: (0, 0, 0)),
                  smem_idx(lambda j: (1, 0, 0)),
                  smem_idx(lambda j: (jnp.minimum(2 * j + 2, nt - 1), 0, 0)),
                  smem_idx(lambda j: (jnp.minimum(2 * j + 3, nt - 1), 0, 0)),
                  smem_idx(lambda j: (jnp.where(j == 0, nt, 2 * j - 1), 0, 0)),
                  smem_idx(lambda j: (2 * j, 0, 0)),
                  smem_idx(lambda j: (nt - 1, 0, 0)),
                  pl.BlockSpec((tm, wgt.shape[1]), lambda j, *_: (2 * j, 0)),
                  pl.BlockSpec((tm, wgt.shape[1]), lambda j, *_: (2 * j + 1, 0)),
                  any_space, any_space, any_space] + weights(0) + weights(1),
        out_specs=any_space,
        scratch_shapes=[pltpu.VMEM((2, tm * ROW_TILE, lanes), f32), pltpu.VMEM((2, tm * ROW_TILE, lanes), f32),
                        pltpu.SemaphoreType.DMA((2,)), pltpu.SemaphoreType.DMA((2,)),
                        pltpu.VMEM((tm, d), bf16), pltpu.VMEM((tm, d), f32),
                        pltpu.VMEM((2, d, 2 * D_EXPERT), bf16), pltpu.VMEM((2, D_EXPERT, d), bf16),
                        pltpu.SemaphoreType.DMA((4,))],
    )
    return pl.pallas_call(
        _moe_kernel,
        out_shape=jax.ShapeDtypeStruct((rows + tm * ROW_TILE, lanes), f32),
        grid_spec=grid_spec,
        compiler_params=pltpu.CompilerParams(dimension_semantics=("arbitrary",),
                                             vmem_limit_bytes=_vmem_limit(48 * 1024 * 1024)),
        name="moe",
    )(tile_a, tile_b, first_pair, extra_start, extra_pair, pair_a, pair_b,
      idx, idx, idx, idx, idx, idx, idx, wgt, wgt, h2, w13, w2, w13, w2, w13, w2, w13, w2, w13, w2)


def _pair_tables():
    a, b = [], []
    for g in range(N_GROUPS):
        for i in range(EXPERTS_PER_GROUP):
            for j in range(i + 1, EXPERTS_PER_GROUP):
                a.append(g * EXPERTS_PER_GROUP + i)
                b.append(g * EXPERTS_PER_GROUP + j)
    return np.asarray(a, np.int32), np.asarray(b, np.int32)


def _dispatch_plan(pid, wlo, whi, tm):
    t = pid.shape[0]
    nt = t // tm
    assert t % (2 * tm) == 0
    pid_s, tok_s, wlo_s, whi_s = lax.sort((pid, jnp.arange(t, dtype=i32), wlo, whi), num_keys=1, is_stable=True)
    pairs = jnp.arange(N_PAIRS, dtype=i32)
    counts = jnp.sum((pid[:, None] == pairs[None, :]).astype(i32), axis=0)
    starts = jnp.cumsum(counts) - counts
    first_pair = pid_s.reshape(nt, tm)[:, 0]
    is_extra = jnp.logical_and(counts > 0, starts % tm != 0)
    extra_tile = starts // tm
    tile = jnp.arange(nt + 1, dtype=i32)
    extra_start = jnp.sum(jnp.logical_and(is_extra[None, :], extra_tile[None, :] < tile[:, None]).astype(i32), axis=1)
    extra_rank = jnp.cumsum(is_extra.astype(i32)) - is_extra.astype(i32)
    hit = jnp.logical_and(is_extra[None, :], extra_rank[None, :] == pairs[:, None])
    extra_pair = jnp.sum(jnp.where(hit, pairs[None, :], 0), axis=1)
    pa, pb = (jnp.asarray(tab) for tab in _pair_tables())
    is_first = (first_pair[:, None] == pairs[None, :]).astype(i32)
    row = jnp.arange(tm, dtype=i32)[None, :]
    idx = jnp.concatenate([tok_s.reshape(nt, tm), t + row], axis=0).reshape(nt + 1, 1, tm) * ROW_TILE
    wgt = jnp.stack([wlo_s, whi_s, pid_s.astype(f32), jnp.zeros_like(wlo_s)], axis=-1)
    return (jnp.sum(is_first * pa[None, :], axis=1), jnp.sum(is_first * pb[None, :], axis=1), first_pair,
            extra_start, extra_pair, pa, pb, idx, wgt)


def _final_kernel(x1_ref, y_ref, mod_ref, g_ref, out_ref):
    m = mod_ref[0]
    x2 = x1_ref[0] + m[5:6] * _tiles_to_rows(y_ref)
    ms = jnp.mean(x2 * x2, axis=-1, keepdims=True)
    out_ref[0] = x2 * lax.rsqrt(ms + EPS) * g_ref[...]


def _final(x1, y, mod3, g):
    b, s, d = x1.shape
    tm = TOK_TILE
    nj = s // tm
    tok = pl.BlockSpec((1, tm, d), lambda i, j: (i, j, 0))
    return pl.pallas_call(
        _final_kernel,
        out_shape=jax.ShapeDtypeStruct((b, s, d), f32),
        grid=(b, nj),
        in_specs=[tok,
                  pl.BlockSpec((tm * ROW_TILE, y.shape[1]), lambda i, j: (i * nj + j, 0)),
                  pl.BlockSpec((1, MOD_CHUNKS, d), lambda i, j: (i, 0, 0)),
                  pl.BlockSpec((1, d), lambda i, j: (0, 0))],
        out_specs=tok,
        compiler_params=pltpu.CompilerParams(dimension_semantics=("arbitrary", "arbitrary"),
                                             vmem_limit_bytes=_vmem_limit(32 * 1024 * 1024)),
        name="final",
    )(x1, y, mod3, g)


def _pool_constants(tm):
    slab = V7X_MXU_DIM
    band = np.zeros((len(POOL_WINDOWS), slab, slab), np.float32)
    icnt = np.zeros((tm, POOL_WIDTH), np.float32)
    for gi, w in enumerate(POOL_WINDOWS):
        for t in range(slab):
            base, pos = (t // GRID_W) * GRID_W, t % GRID_W
            lo, hi = max(pos - w // 2, 0), min(pos + w // 2, GRID_W)
            band[gi, t, base + lo:base + hi] = 1.0
        for t in range(tm):
            pos = t % GRID_W
            cnt = min(pos + w // 2, GRID_W) - max(pos - w // 2, 0)
            icnt[t, gi * POOL_GROUP_DIM:(gi + 1) * POOL_GROUP_DIM] = 1.0 / cnt
    return jnp.asarray(band, bf16), jnp.asarray(icnt, f32)


def _block_diag(blocks):
    n = len(blocks)
    r, c = blocks[0].shape
    out = jnp.zeros((n * r, n * c), blocks[0].dtype)
    for i, blk in enumerate(blocks):
        out = out.at[i * r:(i + 1) * r, i * c:(i + 1) * c].set(blk)
    return out


def kernel(x, c, ctx, c_ctx, w_mod, b_mod, norm1_g, norm2_g, w_in, gla_a2_f, gla_ab_f, gla_a2_b, gla_ab_b,
           gla_onorm_g, pool_w, pool_scale, w_pool_br, w_gla_br, w_o, router_grp_w, router_grp_b,
           router_exp_w, router_exp_b, moe_w1, moe_w3, moe_w2, final_norm_g):
    b, s, d = x.shape
    depth = w_mod.shape[0]
    assert depth == 1 and d == D_MODEL and s % TOK_TILE == 0 and s % (4 * GLA_CHUNK) == 0
    assert TOK_TILE % V7X_MXU_DIM == 0 and V7X_MXU_DIM % GRID_W == 0
    t = b * s

    cond = jnp.concatenate([c, c_ctx[None, :], jnp.zeros((COND_PAD - 1, d), f32)], axis=0)
    mod3 = _adaln(cond, w_mod[0], b_mod[0][None, :]).reshape(b + COND_PAD, MOD_CHUNKS, d)

    w_in_b = w_in[0].astype(bf16)
    wqkv = w_in_b[:, _Q0:_G0]
    wkv = w_in_b[:, _K0:_G0]
    wugg = jnp.concatenate([w_in_b[:, _U0:_Q0], w_in_b[:, _G0:_LR0]], axis=1)
    wlr = jnp.pad(w_in_b[:, _LR0:_IN_END], ((0, 0), (0, V7X_LANES - 2 * GLA_GATE_RANK)))
    a2bd = jnp.pad(_block_diag([gla_a2_f[0], gla_a2_b[0]]).astype(bf16),
                   ((0, V7X_LANES - 2 * GLA_GATE_RANK), (0, 0)))
    ab = jnp.concatenate([gla_ab_f[0], gla_ab_b[0]])[None, :]
    n1 = norm1_g[0][None, :]
    n2 = norm2_g[0][None, :]

    s0f, s0b = _ctx_states(ctx, mod3, b, n1, wkv, wlr, a2bd, ab)
    q, k, v, la = _qkv(x, mod3, n1, wqkv, wlr, a2bd, ab)
    o = _gla(q, k, v, la, s0f, s0b)

    band, icnt = _pool_constants(TOK_TILE)
    pwbd = _block_diag([pool_w[0, gi] for gi in range(len(POOL_WINDOWS))]).astype(bf16)
    onorm = jnp.tile(gla_onorm_g[0], GLA_HEADS)[None, :]
    wr = jnp.zeros((d, V7X_LANES), f32)
    wr = wr.at[:, :N_GROUPS].set(router_grp_w[0]).at[:, V7X_SUBLANES:V7X_SUBLANES + N_EXPERTS].set(router_exp_w[0])
    br = jnp.zeros((1, V7X_LANES), f32)
    br = br.at[0, :N_GROUPS].set(router_grp_b[0]).at[0, V7X_SUBLANES:V7X_SUBLANES + N_EXPERTS].set(router_exp_b[0])
    x1, h2, ri, rw = _mix(x, o, mod3, n1, n2, wugg, band, icnt, pwbd, pool_scale[0][None, :],
                           w_pool_br[0].astype(bf16), onorm, w_gla_br[0].astype(bf16), w_o[0].astype(bf16),
                           wr.astype(bf16), br)

    plan = _dispatch_plan(ri[0], rw[0], rw[1], MOE_TILE)
    w13 = jnp.concatenate([moe_w1[0], moe_w3[0]], axis=-1).astype(bf16)
    y = _moe(*plan, h2, w13, moe_w2[0].astype(bf16))
    return _final(x1, y, mod3, final_norm_g[None, :])
```

```python
import jax
import jax.numpy as jnp
import numpy as np
from jax import lax
from jax.experimental import pallas as pl
from jax.experimental.pallas import tpu as pltpu

f32 = jnp.float32
bf16 = jnp.bfloat16
i32 = jnp.int32
u32 = jnp.uint32

D_MODEL = 1024
GRID_W = 64
POOL_WINDOWS = (2, 4, 8, 16)
POOL_GROUP_DIM = 128
POOL_WIDTH = len(POOL_WINDOWS) * POOL_GROUP_DIM
GLA_HEADS = 4
GLA_DK = 512
GLA_DV = 1024
GLA_HK = GLA_DK // GLA_HEADS
GLA_HV = GLA_DV // GLA_HEADS
GLA_GATE_RANK = 16
GLA_GATE_NORMALIZER = 16.0
N_GROUPS = 4
EXPERTS_PER_GROUP = 8
N_EXPERTS = N_GROUPS * EXPERTS_PER_GROUP
D_EXPERT = 512
MOD_CHUNKS = 6
EPS = 1e-6

_U0, _Q0, _K0, _V0, _G0, _GT0, _LR0, _IN_END = 0, 512, 1024, 1536, 2560, 3584, 5632, 5664

V7X_LANES = 128
V7X_SUBLANES = 8
V7X_MXU_DIM = 256
V7X_SCOPED_VMEM_CAP = 60000 * 1024

GLA_CHUNK = 128
GLA_HALF = GLA_CHUNK // 2
TOK_TILE = 512
MOE_TILE = 128
COND_PAD = 8
N_PAIRS = N_GROUPS * (EXPERTS_PER_GROUP * (EXPERTS_PER_GROUP - 1) // 2)


def _vmem_limit(nbytes):
    return int(min(max(nbytes, 16 * 1024 * 1024), V7X_SCOPED_VMEM_CAP))


def _dot(a, b):
    return jnp.dot(a, b, preferred_element_type=f32)


def _dot_nt(a, b):
    return lax.dot_general(a, b, (((1,), (1,)), ((), ())), preferred_element_type=f32)


def _dot_tn(a, b):
    return lax.dot_general(a, b, (((0,), (0,)), ((), ())), preferred_element_type=f32)


def _sigmoid(x):
    return 1.0 / (1.0 + jnp.exp(-x))


def _silu(x):
    return x * _sigmoid(x)


def _log_sigmoid(z):
    return jnp.minimum(z, 0.0) - jnp.log1p(jnp.exp(-jnp.abs(z)))


def _norm_mod(x, g, scale, shift):
    ms = jnp.mean(x * x, axis=-1, keepdims=True)
    return (x * lax.rsqrt(ms + EPS) * g) * (1.0 + scale) + shift


def _tri_cumsum(tri, la):
    hi = la.astype(bf16)
    lo = (la - hi.astype(f32)).astype(bf16)
    return _dot(tri, hi) + _dot(tri, lo)


ROW_TILE = V7X_SUBLANES


def _rows_to_tiles(ref, val):
    n = val.shape[0]
    for a in range(ROW_TILE):
        ref[pl.ds(a, n, stride=ROW_TILE), :] = val[:, a * V7X_LANES:(a + 1) * V7X_LANES]


def _tiles_to_rows(ref):
    n = ref.shape[0] // ROW_TILE
    return jnp.concatenate([ref[pl.ds(a, n, stride=ROW_TILE), :] for a in range(ROW_TILE)], axis=1)


def _log_decays(hx, wlr_ref, a2_ref, ab_ref):
    lr = _dot(hx, wlr_ref[...])
    z = _dot(lr.astype(bf16), a2_ref[...]) + ab_ref[...]
    return _log_sigmoid(z) * (1.0 / GLA_GATE_NORMALIZER)


def _adaln_kernel(c_ref, w_ref, b_ref, o_ref):
    c = c_ref[...]
    o_ref[...] = _dot(_silu(c).astype(bf16), w_ref[...].astype(bf16)) + b_ref[...]


def _adaln(cond, w_mod, b_mod):
    rows, d = cond.shape
    n = w_mod.shape[1]
    tn = 512
    return pl.pallas_call(
        _adaln_kernel,
        out_shape=jax.ShapeDtypeStruct((rows, n), f32),
        grid=(n // tn,),
        in_specs=[pl.BlockSpec((rows, d), lambda j: (0, 0)),
                  pl.BlockSpec((d, tn), lambda j: (0, j)),
                  pl.BlockSpec((1, tn), lambda j: (0, j))],
        out_specs=pl.BlockSpec((rows, tn), lambda j: (0, j)),
        compiler_params=pltpu.CompilerParams(dimension_semantics=("arbitrary",)),
        name="adaln",
    )(cond, w_mod, b_mod)


def _ctx_kernel(ctx_ref, mod_ref, g_ref, wkv_ref, wlr_ref, a2_ref, ab_ref, sf_ref, sb_ref):
    x = ctx_ref[0]
    n = x.shape[0]
    m = mod_ref[0]
    hc = _norm_mod(x, g_ref[...], m[1:2], m[0:1]).astype(bf16)
    kv = _dot(hc, wkv_ref[...])
    la = _log_decays(hc, wlr_ref, a2_ref, ab_ref)
    row = lax.broadcasted_iota(i32, (n, n), 0)
    col = lax.broadcasted_iota(i32, (n, n), 1)
    lower = jnp.where(row >= col, 1.0, 0.0).astype(bf16)
    upper = jnp.where(col >= row, 1.0, 0.0).astype(bf16)
    cum_f = _tri_cumsum(lower, la[:, :GLA_DK])
    cum_b = _tri_cumsum(upper, la[:, GLA_DK:])
    k = kv[:, :GLA_DK]
    v = kv[:, GLA_DK:].astype(bf16)
    kr_f = (k * jnp.exp(cum_f[n - 1:n] - cum_f)).astype(bf16)
    kr_b = (k * jnp.exp(cum_b[0:1] - cum_b)).astype(bf16)
    for h in range(GLA_HEADS):
        ks = slice(h * GLA_HK, (h + 1) * GLA_HK)
        vs = slice(h * GLA_HV, (h + 1) * GLA_HV)
        sf_ref[0, h] = _dot_tn(kr_f[:, ks], v[:, vs])
        sb_ref[0, h] = _dot_tn(kr_b[:, ks], v[:, vs])


def _ctx_states(ctx, mod3, ctx_row, g, wkv, wlr, a2bd, ab):
    b, n, d = ctx.shape
    st = jax.ShapeDtypeStruct((b, GLA_HEADS, GLA_HK, GLA_HV), f32)
    const = lambda i: (0, 0)
    st_spec = pl.BlockSpec((1, GLA_HEADS, GLA_HK, GLA_HV), lambda i: (i, 0, 0, 0))
    return pl.pallas_call(
        _ctx_kernel,
        out_shape=(st, st),
        grid=(b,),
        in_specs=[pl.BlockSpec((1, n, d), lambda i: (i, 0, 0)),
                  pl.BlockSpec((1, MOD_CHUNKS, d), lambda i: (ctx_row, 0, 0)),
                  pl.BlockSpec((1, d), const),
                  pl.BlockSpec(wkv.shape, const),
                  pl.BlockSpec(wlr.shape, const),
                  pl.BlockSpec(a2bd.shape, const),
                  pl.BlockSpec(ab.shape, const)],
        out_specs=(st_spec, st_spec),
        compiler_params=pltpu.CompilerParams(dimension_semantics=("arbitrary",),
                                             vmem_limit_bytes=_vmem_limit(32 * 1024 * 1024)),
        name="ctx_states",
    )(ctx, mod3, g, wkv, wlr, a2bd, ab)


def _qkv_kernel(x_ref, mod_ref, g_ref, wqkv_ref, wlr_ref, a2_ref, ab_ref, q_ref, k_ref, v_ref, la_ref):
    m = mod_ref[0]
    hx = _norm_mod(x_ref[0], g_ref[...], m[1:2], m[0:1]).astype(bf16)
    p = _dot(hx, wqkv_ref[...])
    q_ref[0] = p[:, :GLA_DK].astype(bf16)
    k_ref[0] = p[:, GLA_DK:2 * GLA_DK].astype(bf16)
    v_ref[0] = p[:, 2 * GLA_DK:].astype(bf16)
    la_ref[0] = _log_decays(hx, wlr_ref, a2_ref, ab_ref)


def _qkv(x, mod3, g, wqkv, wlr, a2bd, ab):
    b, s, d = x.shape
    tm = TOK_TILE
    const = lambda i, j: (0, 0)
    tok = lambda w: pl.BlockSpec((1, tm, w), lambda i, j: (i, j, 0))
    return pl.pallas_call(
        _qkv_kernel,
        out_shape=(jax.ShapeDtypeStruct((b, s, GLA_DK), bf16), jax.ShapeDtypeStruct((b, s, GLA_DK), bf16),
                   jax.ShapeDtypeStruct((b, s, GLA_DV), bf16), jax.ShapeDtypeStruct((b, s, 2 * GLA_DK), f32)),
        grid=(b, s // tm),
        in_specs=[tok(d),
                  pl.BlockSpec((1, MOD_CHUNKS, d), lambda i, j: (i, 0, 0)),
                  pl.BlockSpec((1, d), const),
                  pl.BlockSpec(wqkv.shape, const),
                  pl.BlockSpec(wlr.shape, const),
                  pl.BlockSpec(a2bd.shape, const),
                  pl.BlockSpec(ab.shape, const)],
        out_specs=(tok(GLA_DK), tok(GLA_DK), tok(GLA_DV), tok(2 * GLA_DK)),
        compiler_params=pltpu.CompilerParams(dimension_semantics=("arbitrary", "arbitrary"),
                                             vmem_limit_bytes=_vmem_limit(48 * 1024 * 1024)),
        name="qkv",
    )(x, mod3, g, wqkv, wlr, a2bd, ab)


def _gla_kernel(q_ref, k_ref, v_ref, la_ref, s0f_ref, s0b_ref, o_ref, s_sc, oacc, a_sc, ops_a, ops_b, dec_a, dec_b):
    seq = q_ref.shape[1]
    nc = seq // GLA_CHUNK
    c_len = GLA_CHUNK
    s_sc[0:GLA_HEADS] = s0f_ref[0]
    s_sc[GLA_HEADS:2 * GLA_HEADS] = s0b_ref[0]
    row = lax.broadcasted_iota(i32, (c_len, c_len), 0)
    col = lax.broadcasted_iota(i32, (c_len, c_len), 1)
    lower = row >= col
    upper = col >= row
    tri_l = jnp.where(lower, 1.0, 0.0).astype(bf16)
    tri_u = jnp.where(upper, 1.0, 0.0).astype(bf16)
    scale = GLA_HK ** -0.5

    n_hd = 2 * GLA_HEADS
    QD, QS, KI, KR = range(4)

    def chunk_rows(c):
        return pl.ds(pl.multiple_of(c * c_len, c_len), c_len)

    def chunks_of(step):
        return jnp.minimum(step, nc - 1), jnp.maximum(nc - 1 - step, 0)

    def prepare(step, ops, dec):
        tots = []
        for d, c, tri, r_row, t_row in zip((0, 1), chunks_of(step), (tri_l, tri_u), (GLA_HALF - 1, GLA_HALF),
                                           (c_len - 1, 0)):
            rows = chunk_rows(c)
            cum = _tri_cumsum(tri, la_ref[0, rows, d * GLA_DK:(d + 1) * GLA_DK])
            r = cum[r_row:r_row + 1]
            tot = cum[t_row:t_row + 1]
            qd = q_ref[0, rows, :].astype(f32) * (jnp.exp(cum - r) * scale)
            ki = k_ref[0, rows, :].astype(f32) * jnp.exp(r - cum)
            ops[4 * d + QD] = qd.astype(bf16)
            ops[4 * d + QS] = (qd * jnp.exp(r)).astype(bf16)
            ops[4 * d + KI] = ki.astype(bf16)
            ops[4 * d + KR] = (ki * jnp.exp(tot - r)).astype(bf16)
            tots.extend(tot[:, h * GLA_HK:(h + 1) * GLA_HK] for h in range(GLA_HEADS))
        pad = jnp.zeros((c_len - n_hd, GLA_HK), f32)
        dec[...] = jnp.exp(jnp.concatenate(tots + [pad], axis=0).T)

    def matmuls(step, ops, dec, second_visit):
        rows = [chunk_rows(c) for c in chunks_of(step)]
        for d, mask in ((0, lower), (1, upper)):
            for h in range(GLA_HEADS):
                ks = slice(h * GLA_HK, (h + 1) * GLA_HK)
                a = _dot_nt(ops[4 * d + QD, :, ks], ops[4 * d + KI, :, ks])
                a_sc[d * GLA_HEADS + h] = jnp.where(mask, a, 0.0).astype(bf16)
        for d in (0, 1):
            for h in range(GLA_HEADS):
                j = d * GLA_HEADS + h
                ks = slice(h * GLA_HK, (h + 1) * GLA_HK)
                vs = slice(h * GLA_HV, (h + 1) * GLA_HV)
                o = _dot(jnp.concatenate([a_sc[j], ops[4 * d + QS, :, ks]], axis=1),
                         jnp.concatenate([v_ref[0, rows[d], vs], s_sc[j].astype(bf16)], axis=0))
                if second_visit:
                    o_ref[0, rows[d], vs] = (oacc[rows[d], vs] + o).astype(o_ref.dtype)
                else:
                    oacc[rows[d], vs] = o
        for d in (0, 1):
            for h in range(GLA_HEADS):
                j = d * GLA_HEADS + h
                ks = slice(h * GLA_HK, (h + 1) * GLA_HK)
                vs = slice(h * GLA_HV, (h + 1) * GLA_HV)
                s_sc[j] = dec[:, j:j + 1] * s_sc[j] + _dot_tn(ops[4 * d + KR, :, ks], v_ref[0, rows[d], vs])

    def step_pair(second_visit):
        def body(m, carry):
            prepare(2 * m + 1, ops_b, dec_b)
            matmuls(2 * m, ops_a, dec_a, second_visit)
            prepare(2 * m + 2, ops_a, dec_a)
            matmuls(2 * m + 1, ops_b, dec_b, second_visit)
            return carry
        return body

    prepare(0, ops_a, dec_a)
    lax.fori_loop(0, nc // 4, step_pair(False), 0)
    lax.fori_loop(nc // 4, nc // 2, step_pair(True), 0)


def _gla(q, k, v, la, s0f, s0b):
    b, s, _ = q.shape
    tok = lambda w: pl.BlockSpec((1, s, w), lambda i: (i, 0, 0))
    st_spec = pl.BlockSpec((1, GLA_HEADS, GLA_HK, GLA_HV), lambda i: (i, 0, 0, 0))
    return pl.pallas_call(
        _gla_kernel,
        out_shape=jax.ShapeDtypeStruct((b, s, GLA_DV), bf16),
        grid=(b,),
        in_specs=[tok(GLA_DK), tok(GLA_DK), tok(GLA_DV), tok(2 * GLA_DK), st_spec, st_spec],
        out_specs=tok(GLA_DV),
        scratch_shapes=[pltpu.VMEM((2 * GLA_HEADS, GLA_HK, GLA_HV), f32),
                        pltpu.VMEM((s, GLA_DV), f32),
                        pltpu.VMEM((2 * GLA_HEADS, GLA_CHUNK, GLA_CHUNK), bf16),
                        pltpu.VMEM((8, GLA_CHUNK, GLA_DK), bf16), pltpu.VMEM((8, GLA_CHUNK, GLA_DK), bf16),
                        pltpu.VMEM((GLA_CHUNK, GLA_HK), f32), pltpu.VMEM((GLA_CHUNK, GLA_HK), f32)],
        compiler_params=pltpu.CompilerParams(dimension_semantics=("arbitrary",),
                                             vmem_limit_bytes=V7X_SCOPED_VMEM_CAP),
        name="gla",
    )(q, k, v, la, s0f, s0b)


def _mix_kernel(x_ref, o_ref, mod_ref, n1_ref, n2_ref, wugg_ref, band_ref, icnt_ref, pwbd_ref, pscale_ref,
                wpool_ref, onorm_ref, wgla_ref, wo_ref, wr_ref, br_ref,
                x1_ref, h2_ref, ri_ref, rw_ref):
    x = x_ref[0]
    tm = x.shape[0]
    m = mod_ref[0]
    hx = _norm_mod(x, n1_ref[...], m[1:2], m[0:1]).astype(bf16)
    p = _dot(hx, wugg_ref[...])
    u = p[:, :POOL_WIDTH]
    g = p[:, POOL_WIDTH:POOL_WIDTH + GLA_DV]
    gates = p[:, POOL_WIDTH + GLA_DV:]

    ub = u.astype(bf16)
    slab = band_ref.shape[1]
    sums = []
    for s in range(tm // slab):
        rs = slice(s * slab, (s + 1) * slab)
        sums.append(jnp.concatenate(
            [_dot(band_ref[gi], ub[rs, gi * POOL_GROUP_DIM:(gi + 1) * POOL_GROUP_DIM])
             for gi in range(len(POOL_WINDOWS))], axis=1))
    pooled = jnp.concatenate(sums, axis=0) * icnt_ref[...] - u
    y_pool = _dot(pooled.astype(bf16), pwbd_ref[...]) * pscale_ref[...]
    y_pool = _dot(y_pool.astype(bf16), wpool_ref[...])

    o = o_ref[0].astype(f32)
    normed = []
    for h in range(GLA_HEADS):
        oh = o[:, h * GLA_HV:(h + 1) * GLA_HV]
        normed.append(oh * lax.rsqrt(jnp.mean(oh * oh, axis=-1, keepdims=True) + EPS))
    on = jnp.concatenate(normed, axis=1) * onorm_ref[...]
    y_gla = _dot((on * _silu(g)).astype(bf16), wgla_ref[...])

    mixed = _sigmoid(gates[:, :D_MODEL]) * y_pool + _sigmoid(gates[:, D_MODEL:]) * y_gla
    x1 = x + m[2:3] * _dot(mixed.astype(bf16), wo_ref[...])
    x1_ref[0] = x1

    h2f = _norm_mod(x1, n2_ref[...], m[4:5], m[3:4])
    _rows_to_tiles(h2_ref, h2f)
    h2 = h2f.astype(bf16)

    lt = (_dot(h2, wr_ref[...]) + br_ref[...]).T
    g0, g1, g2, g3 = lt[0:1], lt[1:2], lt[2:3], lt[3:4]
    gmax = jnp.maximum(jnp.maximum(g0, g1), jnp.maximum(g2, g3))
    gidx = jnp.where(g0 == gmax, 0, jnp.where(g1 == gmax, 1, jnp.where(g2 == gmax, 2, 3))).astype(i32)
    p_grp = 1.0 / (jnp.exp(g0 - gmax) + jnp.exp(g1 - gmax) + jnp.exp(g2 - gmax) + jnp.exp(g3 - gmax))
    e0 = V7X_SUBLANES
    ing = lt[e0 + 3 * EXPERTS_PER_GROUP:e0 + 4 * EXPERTS_PER_GROUP]
    for gi in (2, 1, 0):
        ing = jnp.where(gidx == gi, lt[e0 + gi * EXPERTS_PER_GROUP:e0 + (gi + 1) * EXPERTS_PER_GROUP], ing)
    rid = lax.broadcasted_iota(i32, ing.shape, 0)
    v1 = jnp.max(ing, axis=0, keepdims=True)
    i1 = jnp.min(jnp.where(ing == v1, rid, EXPERTS_PER_GROUP), axis=0, keepdims=True)
    rest = jnp.where(rid == i1, -jnp.inf, ing)
    v2 = jnp.max(rest, axis=0, keepdims=True)
    i2 = jnp.min(jnp.where(rest == v2, rid, EXPERTS_PER_GROUP), axis=0, keepdims=True)
    t = jnp.exp(v2 - v1)
    w1 = p_grp / (1.0 + t)
    w2 = p_grp * t / (1.0 + t)
    first_low = i1 < i2
    ilo = jnp.minimum(i1, i2)
    ihi = jnp.maximum(i1, i2)
    pair = ilo * (EXPERTS_PER_GROUP - 1) - ((ilo * (ilo - 1)) >> 1) + (ihi - ilo - 1)
    pid = gidx * (N_PAIRS // N_GROUPS) + pair
    ri_ref[...] = jnp.where(rid == 0, pid, 0)
    rw_ref[...] = jnp.where(rid == 0, jnp.where(first_low, w1, w2),
                            jnp.where(rid == 1, jnp.where(first_low, w2, w1), 0.0))


def _mix(x, o, mod3, n1, n2, wugg, band, icnt, pwbd, pscale, wpool, onorm, wgla, wo, wr, br):
    b, s, d = x.shape
    tm = TOK_TILE
    nj = s // tm
    t = b * s
    const2 = lambda i, j: (0, 0)
    const3 = lambda i, j: (0, 0, 0)
    tok = lambda w: pl.BlockSpec((1, tm, w), lambda i, j: (i, j, 0))
    lane = pl.BlockSpec((V7X_SUBLANES, tm), lambda i, j: (0, i * nj + j))
    return pl.pallas_call(
        _mix_kernel,
        out_shape=(jax.ShapeDtypeStruct((b, s, d), f32), jax.ShapeDtypeStruct((t * ROW_TILE, V7X_LANES), f32),
                   jax.ShapeDtypeStruct((V7X_SUBLANES, t), i32), jax.ShapeDtypeStruct((V7X_SUBLANES, t), f32)),
        grid=(b, nj),
        in_specs=[tok(d), tok(GLA_DV),
                  pl.BlockSpec((1, MOD_CHUNKS, d), lambda i, j: (i, 0, 0)),
                  pl.BlockSpec((1, d), const2), pl.BlockSpec((1, d), const2),
                  pl.BlockSpec(wugg.shape, const2),
                  pl.BlockSpec(band.shape, const3),
                  pl.BlockSpec(icnt.shape, const2),
                  pl.BlockSpec(pwbd.shape, const2),
                  pl.BlockSpec(pscale.shape, const2),
                  pl.BlockSpec(wpool.shape, const2),
                  pl.BlockSpec(onorm.shape, const2),
                  pl.BlockSpec(wgla.shape, const2),
                  pl.BlockSpec(wo.shape, const2),
                  pl.BlockSpec(wr.shape, const2),
                  pl.BlockSpec(br.shape, const2)],
        out_specs=(tok(d), pl.BlockSpec((tm * ROW_TILE, V7X_LANES), lambda i, j: (i * nj + j, 0)), lane, lane),
        compiler_params=pltpu.CompilerParams(dimension_semantics=("arbitrary", "arbitrary"),
                                             vmem_limit_bytes=V7X_SCOPED_VMEM_CAP),
        name="mix",
    )(x, o, mod3, n1, n2, wugg, band, icnt, pwbd, pscale, wpool, onorm, wgla, wo, wr, br)


def _moe_kernel(ta_ref, tb_ref, g0_ref, xs_ref, xg_ref, pa_ref, pb_ref,
                idx0_ref, idx1_ref, idxn0_ref, idxn1_ref, idxp0_ref, idxp1_ref, idxl_ref,
                wgt0_ref, wgt1_ref, h2_ref, w13_any, w2_any, w13a0_ref, w2a0_ref, w13b0_ref, w2b0_ref,
                w13a1_ref, w2a1_ref, w13b1_ref, w2b1_ref, y_ref,
                xbuf, ybuf, gsem, ssem, xt_sc, yacc, wx13, wx2, wsem):
    del ta_ref, tb_ref
    j = pl.program_id(0)
    last = pl.num_programs(0) - 1
    tm = xbuf.shape[1] // ROW_TILE

    def token_tile(ref, off):
        return ref.at[pl.ds(pl.multiple_of(off, ROW_TILE), ROW_TILE)]

    def gather_row(off, s, r8):
        return pltpu.make_async_copy(token_tile(h2_ref, off), xbuf.at[s, pl.ds(r8, ROW_TILE)], gsem.at[s])

    def scatter_row(off, s, r8):
        return pltpu.make_async_copy(ybuf.at[s, pl.ds(r8, ROW_TILE)], token_tile(y_ref, off), ssem.at[s])

    def start_rows(make, idx_ref):
        def body(r, carry):
            make(idx_ref[0, 0, r], pl.multiple_of(r * ROW_TILE, ROW_TILE)).start()
            return carry
        lax.fori_loop(0, tm, body, 0, unroll=8)

    def gather_tile(s):
        return pltpu.make_async_copy(h2_ref.at[pl.ds(0, tm * ROW_TILE)], xbuf.at[s], gsem.at[s])

    def scatter_tile(s):
        return pltpu.make_async_copy(ybuf.at[s], y_ref.at[pl.ds(0, tm * ROW_TILE)], ssem.at[s])

    @pl.when(j == 0)
    def _():
        ybuf[...] = jnp.zeros(ybuf.shape, ybuf.dtype)
        start_rows(lambda off, r8: gather_row(off, 0, r8), idx0_ref)
        start_rows(lambda off, r8: gather_row(off, 1, r8), idx1_ref)

    n_pieces = 12
    per_piece = -(-tm // n_pieces)
    half = D_EXPERT // 2

    def pair_pass(xt, w_first, w_second, weights, between):
        y = None
        for w_col, (w13, w2) in zip((w_first, w_second), weights):
            acc = None
            for c in range(2):
                h1 = _dot(xt, w13(c * half, (c + 1) * half))
                between()
                h3 = _dot(xt, w13(D_EXPERT + c * half, D_EXPERT + (c + 1) * half))
                between()
                part = _dot((_silu(h1) * h3).astype(bf16), w2(c * half, (c + 1) * half))
                between()
                acc = part if acc is None else acc + part
            y = w_col * acc if y is None else y + w_col * acc
        return y

    def pair_weights(wgt_ref, pair):
        w = wgt_ref[...]
        mine = w[:, 2:3] == pair.astype(f32)
        return jnp.where(mine, w[:, 0:1], 0.0), jnp.where(mine, w[:, 1:2], 0.0)

    def tile_body(s, tile, idxn_ref, idxp_ref, wgt_ref, w13a_ref, w2a_ref, w13b_ref, w2b_ref):
        first_extra = xs_ref[tile]
        n_extra = xs_ref[tile + 1] - first_extra

        def extra_weight_copies(k):
            pair = xg_ref[first_extra + k]
            return [pltpu.make_async_copy(src.at[e], dst.at[slot], wsem.at[n])
                    for n, (src, dst, e, slot) in enumerate(((w13_any, wx13, pa_ref[pair], 0),
                                                             (w2_any, wx2, pa_ref[pair], 0),
                                                             (w13_any, wx13, pb_ref[pair], 1),
                                                             (w2_any, wx2, pb_ref[pair], 1)))]

        @pl.when(n_extra > 0)
        def _():
            for cp in extra_weight_copies(0):
                cp.start()

        gather_tile(s).wait()
        xt = _tiles_to_rows(xbuf.at[s]).astype(bf16)
        xt_sc[...] = xt
        piece = [0]

        def issue_rows():
            lo = piece[0] * per_piece
            rows = range(lo, min(lo + per_piece, tm))
            src_rows = [idxn_ref[0, 0, r] for r in rows]
            dst_rows = [idxp_ref[0, 0, r] for r in rows]
            for r, sr, dr in zip(rows, src_rows, dst_rows):
                gather_row(sr, s, r * ROW_TILE).start(priority=0)
                scatter_row(dr, 1 - s, r * ROW_TILE).start(priority=1)
            piece[0] += 1

        block = lambda w13_ref, w2_ref: (lambda c0, c1: w13_ref[0, :, c0:c1], lambda r0, r1: w2_ref[0, r0:r1, :])
        w_first, w_second = pair_weights(wgt_ref, g0_ref[tile])
        yacc[...] = pair_pass(xt, w_first, w_second, (block(w13a_ref, w2a_ref), block(w13b_ref, w2b_ref)), issue_rows)
        assert piece[0] == n_pieces

        def extra_pass(k, carry):
            for cp in extra_weight_copies(k):
                cp.wait()
            copied = lambda slot: (lambda c0, c1: wx13[slot, :, c0:c1], lambda r0, r1: wx2[slot, r0:r1, :])
            w_lo, w_hi = pair_weights(wgt_ref, xg_ref[first_extra + k])
            yacc[...] += pair_pass(xt_sc[...], w_lo, w_hi, (copied(0), copied(1)), lambda: None)

            @pl.when(k + 1 < n_extra)
            def _():
                for cp in extra_weight_copies(k + 1):
                    cp.start()

            return carry

        lax.fori_loop(0, n_extra, extra_pass, 0)

        @pl.when(tile >= 1)
        def _():
            scatter_tile(s).wait()

        _rows_to_tiles(ybuf.at[s], yacc[...])

    tile_body(0, 2 * j, idxn0_ref, idxp0_ref, wgt0_ref, w13a0_ref, w2a0_ref, w13b0_ref, w2b0_ref)
    tile_body(1, 2 * j + 1, idxn1_ref, idxp1_ref, wgt1_ref, w13a1_ref, w2a1_ref, w13b1_ref, w2b1_ref)

    @pl.when(j == last)
    def _():
        scatter_tile(0).wait()
        start_rows(lambda off, r8: scatter_row(off, 1, r8), idxl_ref)
        scatter_tile(1).wait()
        gather_tile(0).wait()
        gather_tile(1).wait()


def _moe(tile_a, tile_b, first_pair, extra_start, extra_pair, pair_a, pair_b, idx, wgt, h2, w13, w2):
    nt, tm = idx.shape[0] - 1, idx.shape[2]
    assert nt % 2 == 0
    rows, lanes = h2.shape
    d = ROW_TILE * lanes
    idx_blk = (1, 1, tm)
    smem_idx = lambda fn: pl.BlockSpec(idx_blk, lambda j, *_: fn(j), memory_space=pltpu.SMEM)
    w13_blk = (1, d, 2 * D_EXPERT)
    w2_blk = (1, D_EXPERT, d)

    def weights(s):
        return [pl.BlockSpec(w13_blk, lambda j, ta, tb, *_: (ta[2 * j + s], 0, 0)),
                pl.BlockSpec(w2_blk, lambda j, ta, tb, *_: (ta[2 * j + s], 0, 0)),
                pl.BlockSpec(w13_blk, lambda j, ta, tb, *_: (tb[2 * j + s], 0, 0)),
                pl.BlockSpec(w2_blk, lambda j, ta, tb, *_: (tb[2 * j + s], 0, 0))]

    any_space = pl.BlockSpec(memory_space=pl.ANY)
    grid_spec = pltpu.PrefetchScalarGridSpec(
        num_scalar_prefetch=7,
        grid=(nt // 2,),
        in_specs=[smem_idx(lambda j: (0, 0, 0)),
                  smem_idx(lambda j: (1, 0, 0)),
                  smem_idx(lambda j: (jnp.minimum(2 * j + 2, nt - 1), 0, 0)),
                  smem_idx(lambda j: (jnp.minimum(2 * j + 3, nt - 1), 0, 0)),
                  smem_idx(lambda j: (jnp.where(j == 0, nt, 2 * j - 1), 0, 0)),
                  smem_idx(lambda j: (2 * j, 0, 0)),
                  smem_idx(lambda j: (nt - 1, 0, 0)),
                  pl.BlockSpec((tm, wgt.shape[1]), lambda j, *_: (2 * j, 0)),
                  pl.BlockSpec((tm, wgt.shape[1]), lambda j, *_: (2 * j + 1, 0)),
                  any_space, any_space, any_space] + weights(0) + weights(1),
        out_specs=any_space,
        scratch_shapes=[pltpu.VMEM((2, tm * ROW_TILE, lanes), f32), pltpu.VMEM((2, tm * ROW_TILE, lanes), f32),
                        pltpu.SemaphoreType.DMA((2,)), pltpu.SemaphoreType.DMA((2,)),
                        pltpu.VMEM((tm, d), bf16), pltpu.VMEM((tm, d), f32),
                        pltpu.VMEM((2, d, 2 * D_EXPERT), bf16), pltpu.VMEM((2, D_EXPERT, d), bf16),
                        pltpu.SemaphoreType.DMA((4,))],
    )
    return pl.pallas_call(
        _moe_kernel,
        out_shape=jax.ShapeDtypeStruct((rows + tm * ROW_TILE, lanes), f32),
        grid_spec=grid_spec,
        compiler_params=pltpu.CompilerParams(dimension_semantics=("arbitrary",),
                                             vmem_limit_bytes=_vmem_limit(48 * 1024 * 1024)),
        name="moe",
    )(tile_a, tile_b, first_pair, extra_start, extra_pair, pair_a, pair_b,
      idx, idx, idx, idx, idx, idx, idx, wgt, wgt, h2, w13, w2, w13, w2, w13, w2, w13, w2, w13, w2)


def _pair_tables():
    a, b = [], []
    for g in range(N_GROUPS):
        for i in range(EXPERTS_PER_GROUP):
            for j in range(i + 1, EXPERTS_PER_GROUP):
                a.append(g * EXPERTS_PER_GROUP + i)
                b.append(g * EXPERTS_PER_GROUP + j)
    return np.asarray(a, np.int32), np.asarray(b, np.int32)


def _dispatch_plan(pid, wlo, whi, tm):
    t = pid.shape[0]
    nt = t // tm
    assert t % (2 * tm) == 0
    pid_s, tok_s, wlo_s, whi_s = lax.sort((pid, jnp.arange(t, dtype=i32), wlo, whi), num_keys=1, is_stable=True)
    pairs = jnp.arange(N_PAIRS, dtype=i32)
    counts = jnp.sum((pid[:, None] == pairs[None, :]).astype(i32), axis=0)
    starts = jnp.cumsum(counts) - counts
    first_pair = pid_s.reshape(nt, tm)[:, 0]
    is_extra = jnp.logical_and(counts > 0, starts % tm != 0)
    extra_tile = starts // tm
    tile = jnp.arange(nt + 1, dtype=i32)
    extra_start = jnp.sum(jnp.logical_and(is_extra[None, :], extra_tile[None, :] < tile[:, None]).astype(i32), axis=1)
    extra_rank = jnp.cumsum(is_extra.astype(i32)) - is_extra.astype(i32)
    hit = jnp.logical_and(is_extra[None, :], extra_rank[None, :] == pairs[:, None])
    extra_pair = jnp.sum(jnp.where(hit, pairs[None, :], 0), axis=1)
    pa, pb = (jnp.asarray(tab) for tab in _pair_tables())
    is_first = (first_pair[:, None] == pairs[None, :]).astype(i32)
    row = jnp.arange(tm, dtype=i32)[None, :]
    idx = jnp.concatenate([tok_s.reshape(nt, tm), t + row], axis=0).reshape(nt + 1, 1, tm) * ROW_TILE
    wgt = jnp.stack([wlo_s, whi_s, pid_s.astype(f32), jnp.zeros_like(wlo_s)], axis=-1)
    return (jnp.sum(is_first * pa[None, :], axis=1), jnp.sum(is_first * pb[None, :], axis=1), first_pair,
            extra_start, extra_pair, pa, pb, idx, wgt)


def _final_kernel(x1_ref, y_ref, mod_ref, g_ref, out_ref):
    m = mod_ref[0]
    x2 = x1_ref[0] + m[5:6] * _tiles_to_rows(y_ref)
    ms = jnp.mean(x2 * x2, axis=-1, keepdims=True)
    out_ref[0] = x2 * lax.rsqrt(ms + EPS) * g_ref[...]


def _final(x1, y, mod3, g):
    b, s, d = x1.shape
    tm = TOK_TILE
    nj = s // tm
    tok = pl.BlockSpec((1, tm, d), lambda i, j: (i, j, 0))
    return pl.pallas_call(
        _final_kernel,
        out_shape=jax.ShapeDtypeStruct((b, s, d), f32),
        grid=(b, nj),
        in_specs=[tok,
                  pl.BlockSpec((tm * ROW_TILE, y.shape[1]), lambda i, j: (i * nj + j, 0)),
                  pl.BlockSpec((1, MOD_CHUNKS, d), lambda i, j: (i, 0, 0)),
                  pl.BlockSpec((1, d), lambda i, j: (0, 0))],
        out_specs=tok,
        compiler_params=pltpu.CompilerParams(dimension_semantics=("arbitrary", "arbitrary"),
                                             vmem_limit_bytes=_vmem_limit(32 * 1024 * 1024)),
        name="final",
    )(x1, y, mod3, g)


def _pool_constants(tm):
    slab = V7X_MXU_DIM
    band = np.zeros((len(POOL_WINDOWS), slab, slab), np.float32)
    icnt = np.zeros((tm, POOL_WIDTH), np.float32)
    for gi, w in enumerate(POOL_WINDOWS):
        for t in range(slab):
            base, pos = (t // GRID_W) * GRID_W, t % GRID_W
            lo, hi = max(pos - w // 2, 0), min(pos + w // 2, GRID_W)
            band[gi, t, base + lo:base + hi] = 1.0
        for t in range(tm):
            pos = t % GRID_W
            cnt = min(pos + w // 2, GRID_W) - max(pos - w // 2, 0)
            icnt[t, gi * POOL_GROUP_DIM:(gi + 1) * POOL_GROUP_DIM] = 1.0 / cnt
    return jnp.asarray(band, bf16), jnp.asarray(icnt, f32)


def _block_diag(blocks):
    n = len(blocks)
    r, c = blocks[0].shape
    out = jnp.zeros((n * r, n * c), blocks[0].dtype)
    for i, blk in enumerate(blocks):
        out = out.at[i * r:(i + 1) * r, i * c:(i + 1) * c].set(blk)
    return out


def kernel(x, c, ctx, c_ctx, w_mod, b_mod, norm1_g, norm2_g, w_in, gla_a2_f, gla_ab_f, gla_a2_b, gla_ab_b,
           gla_onorm_g, pool_w, pool_scale, w_pool_br, w_gla_br, w_o, router_grp_w, router_grp_b,
           router_exp_w, router_exp_b, moe_w1, moe_w3, moe_w2, final_norm_g):
    b, s, d = x.shape
    depth = w_mod.shape[0]
    assert depth == 1 and d == D_MODEL and s % TOK_TILE == 0 and s % (4 * GLA_CHUNK) == 0
    assert TOK_TILE % V7X_MXU_DIM == 0 and V7X_MXU_DIM % GRID_W == 0
    t = b * s

    cond = jnp.concatenate([c, c_ctx[None, :], jnp.zeros((COND_PAD - 1, d), f32)], axis=0)
    mod3 = _adaln(cond, w_mod[0], b_mod[0][None, :]).reshape(b + COND_PAD, MOD_CHUNKS, d)

    w_in_b = w_in[0].astype(bf16)
    wqkv = w_in_b[:, _Q0:_G0]
    wkv = w_in_b[:, _K0:_G0]
    wugg = jnp.concatenate([w_in_b[:, _U0:_Q0], w_in_b[:, _G0:_LR0]], axis=1)
    wlr = jnp.pad(w_in_b[:, _LR0:_IN_END], ((0, 0), (0, V7X_LANES - 2 * GLA_GATE_RANK)))
    a2bd = jnp.pad(_block_diag([gla_a2_f[0], gla_a2_b[0]]).astype(bf16),
                   ((0, V7X_LANES - 2 * GLA_GATE_RANK), (0, 0)))
    ab = jnp.concatenate([gla_ab_f[0], gla_ab_b[0]])[None, :]
    n1 = norm1_g[0][None, :]
    n2 = norm2_g[0][None, :]

    s0f, s0b = _ctx_states(ctx, mod3, b, n1, wkv, wlr, a2bd, ab)
    q, k, v, la = _qkv(x, mod3, n1, wqkv, wlr, a2bd, ab)
    o = _gla(q, k, v, la, s0f, s0b)

    band, icnt = _pool_constants(TOK_TILE)
    pwbd = _block_diag([pool_w[0, gi] for gi in range(len(POOL_WINDOWS))]).astype(bf16)
    onorm = jnp.tile(gla_onorm_g[0], GLA_HEADS)[None, :]
    wr = jnp.zeros((d, V7X_LANES), f32)
    wr = wr.at[:, :N_GROUPS].set(router_grp_w[0]).at[:, V7X_SUBLANES:V7X_SUBLANES + N_EXPERTS].set(router_exp_w[0])
    br = jnp.zeros((1, V7X_LANES), f32)
    br = br.at[0, :N_GROUPS].set(router_grp_b[0]).at[0, V7X_SUBLANES:V7X_SUBLANES + N_EXPERTS].set(router_exp_b[0])
    x1, h2, ri, rw = _mix(x, o, mod3, n1, n2, wugg, band, icnt, pwbd, pool_scale[0][None, :],
                           w_pool_br[0].astype(bf16), onorm, w_gla_br[0].astype(bf16), w_o[0].astype(bf16),
                           wr.astype(bf16), br)

    plan = _dispatch_plan(ri[0], rw[0], rw[1], MOE_TILE)
    w13 = jnp.concatenate([moe_w1[0], moe_w3[0]], axis=-1).astype(bf16)
    y = _moe(*plan, h2, w13, moe_w2[0].astype(bf16))
    return _final(x1, y, mod3, final_norm_g[None, :])
```

```python
import jax
import jax.numpy as jnp
import numpy as np
from jax import lax
from jax.experimental import pallas as pl
from jax.experimental.pallas import tpu as pltpu

f32 = jnp.float32
bf16 = jnp.bfloat16
i32 = jnp.int32
u32 = jnp.uint32

D_MODEL = 1024
GRID_W = 64
POOL_WINDOWS = (2, 4, 8, 16)
POOL_GROUP_DIM = 128
POOL_WIDTH = len(POOL_WINDOWS) * POOL_GROUP_DIM
GLA_HEADS = 4
GLA_DK = 512
GLA_DV = 1024
GLA_HK = GLA_DK // GLA_HEADS
GLA_HV = GLA_DV // GLA_HEADS
GLA_GATE_RANK = 16
GLA_GATE_NORMALIZER = 16.0
N_GROUPS = 4
EXPERTS_PER_GROUP = 8
N_EXPERTS = N_GROUPS * EXPERTS_PER_GROUP
D_EXPERT = 512
MOD_CHUNKS = 6
EPS = 1e-6

_U0, _Q0, _K0, _V0, _G0, _GT0, _LR0, _IN_END = 0, 512, 1024, 1536, 2560, 3584, 5632, 5664

V7X_LANES = 128
V7X_SUBLANES = 8
V7X_MXU_DIM = 256
V7X_SCOPED_VMEM_CAP = 60000 * 1024

GLA_CHUNK = 128
GLA_HALF = GLA_CHUNK // 2
TOK_TILE = 512
MOE_TILE = 256
COND_PAD = 8
N_PAIRS = N_GROUPS * (EXPERTS_PER_GROUP * (EXPERTS_PER_GROUP - 1) // 2)


def _vmem_limit(nbytes):
    return int(min(max(nbytes, 16 * 1024 * 1024), V7X_SCOPED_VMEM_CAP))


def _dot(a, b):
    return jnp.dot(a, b, preferred_element_type=f32)


def _dot_nt(a, b):
    return lax.dot_general(a, b, (((1,), (1,)), ((), ())), preferred_element_type=f32)


def _dot_tn(a, b):
    return lax.dot_general(a, b, (((0,), (0,)), ((), ())), preferred_element_type=f32)


def _sigmoid(x):
    return 1.0 / (1.0 + jnp.exp(-x))


def _silu(x):
    return x * _sigmoid(x)


def _log_sigmoid(z):
    return jnp.minimum(z, 0.0) - jnp.log1p(jnp.exp(-jnp.abs(z)))


def _norm_mod(x, g, scale, shift):
    ms = jnp.mean(x * x, axis=-1, keepdims=True)
    return (x * lax.rsqrt(ms + EPS) * g) * (1.0 + scale) + shift


def _tri_cumsum(tri, la):
    hi = la.astype(bf16)
    lo = (la - hi.astype(f32)).astype(bf16)
    return _dot(tri, hi) + _dot(tri, lo)


ROW_TILE = V7X_SUBLANES


def _rows_to_tiles(ref, val):
    n = val.shape[0]
    for a in range(ROW_TILE):
        ref[pl.ds(a, n, stride=ROW_TILE), :] = val[:, a * V7X_LANES:(a + 1) * V7X_LANES]


def _tiles_to_rows(ref):
    n = ref.shape[0] // ROW_TILE
    return jnp.concatenate([ref[pl.ds(a, n, stride=ROW_TILE), :] for a in range(ROW_TILE)], axis=1)


def _log_decays(hx, wlr_ref, a2_ref, ab_ref):
    lr = _dot(hx, wlr_ref[...])
    z = _dot(lr.astype(bf16), a2_ref[...]) + ab_ref[...]
    return _log_sigmoid(z) * (1.0 / GLA_GATE_NORMALIZER)


def _adaln_kernel(c_ref, w_ref, b_ref, o_ref):
    c = c_ref[...]
    o_ref[...] = _dot(_silu(c).astype(bf16), w_ref[...].astype(bf16)) + b_ref[...]


def _adaln(cond, w_mod, b_mod):
    rows, d = cond.shape
    n = w_mod.shape[1]
    tn = 512
    return pl.pallas_call(
        _adaln_kernel,
        out_shape=jax.ShapeDtypeStruct((rows, n), f32),
        grid=(n // tn,),
        in_specs=[pl.BlockSpec((rows, d), lambda j: (0, 0)),
                  pl.BlockSpec((d, tn), lambda j: (0, j)),
                  pl.BlockSpec((1, tn), lambda j: (0, j))],
        out_specs=pl.BlockSpec((rows, tn), lambda j: (0, j)),
        compiler_params=pltpu.CompilerParams(dimension_semantics=("arbitrary",)),
        name="adaln",
    )(cond, w_mod, b_mod)


def _ctx_kernel(ctx_ref, mod_ref, g_ref, wkv_ref, wlr_ref, a2_ref, ab_ref, sf_ref, sb_ref):
    x = ctx_ref[0]
    n = x.shape[0]
    m = mod_ref[0]
    hc = _norm_mod(x, g_ref[...], m[1:2], m[0:1]).astype(bf16)
    kv = _dot(hc, wkv_ref[...])
    la = _log_decays(hc, wlr_ref, a2_ref, ab_ref)
    row = lax.broadcasted_iota(i32, (n, n), 0)
    col = lax.broadcasted_iota(i32, (n, n), 1)
    lower = jnp.where(row >= col, 1.0, 0.0).astype(bf16)
    upper = jnp.where(col >= row, 1.0, 0.0).astype(bf16)
    cum_f = _tri_cumsum(lower, la[:, :GLA_DK])
    cum_b = _tri_cumsum(upper, la[:, GLA_DK:])
    k = kv[:, :GLA_DK]
    v = kv[:, GLA_DK:].astype(bf16)
    kr_f = (k * jnp.exp(cum_f[n - 1:n] - cum_f)).astype(bf16)
    kr_b = (k * jnp.exp(cum_b[0:1] - cum_b)).astype(bf16)
    for h in range(GLA_HEADS):
        ks = slice(h * GLA_HK, (h + 1) * GLA_HK)
        vs = slice(h * GLA_HV, (h + 1) * GLA_HV)
        sf_ref[0, h] = _dot_tn(kr_f[:, ks], v[:, vs])
        sb_ref[0, h] = _dot_tn(kr_b[:, ks], v[:, vs])


def _ctx_states(ctx, mod3, ctx_row, g, wkv, wlr, a2bd, ab):
    b, n, d = ctx.shape
    st = jax.ShapeDtypeStruct((b, GLA_HEADS, GLA_HK, GLA_HV), f32)
    const = lambda i: (0, 0)
    st_spec = pl.BlockSpec((1, GLA_HEADS, GLA_HK, GLA_HV), lambda i: (i, 0, 0, 0))
    return pl.pallas_call(
        _ctx_kernel,
        out_shape=(st, st),
        grid=(b,),
        in_specs=[pl.BlockSpec((1, n, d), lambda i: (i, 0, 0)),
                  pl.BlockSpec((1, MOD_CHUNKS, d), lambda i: (ctx_row, 0, 0)),
                  pl.BlockSpec((1, d), const),
                  pl.BlockSpec(wkv.shape, const),
                  pl.BlockSpec(wlr.shape, const),
                  pl.BlockSpec(a2bd.shape, const),
                  pl.BlockSpec(ab.shape, const)],
        out_specs=(st_spec, st_spec),
        compiler_params=pltpu.CompilerParams(dimension_semantics=("arbitrary",),
                                             vmem_limit_bytes=_vmem_limit(32 * 1024 * 1024)),
        name="ctx_states",
    )(ctx, mod3, g, wkv, wlr, a2bd, ab)


def _qkv_kernel(x_ref, mod_ref, g_ref, wqkv_ref, wlr_ref, a2_ref, ab_ref, q_ref, k_ref, v_ref, la_ref):
    m = mod_ref[0]
    hx = _norm_mod(x_ref[0], g_ref[...], m[1:2], m[0:1]).astype(bf16)
    p = _dot(hx, wqkv_ref[...])
    q_ref[0] = p[:, :GLA_DK].astype(bf16)
    k_ref[0] = p[:, GLA_DK:2 * GLA_DK].astype(bf16)
    v_ref[0] = p[:, 2 * GLA_DK:].astype(bf16)
    la_ref[0] = _log_decays(hx, wlr_ref, a2_ref, ab_ref)


def _qkv(x, mod3, g, wqkv, wlr, a2bd, ab):
    b, s, d = x.shape
    tm = TOK_TILE
    const = lambda i, j: (0, 0)
    tok = lambda w: pl.BlockSpec((1, tm, w), lambda i, j: (i, j, 0))
    return pl.pallas_call(
        _qkv_kernel,
        out_shape=(jax.ShapeDtypeStruct((b, s, GLA_DK), bf16), jax.ShapeDtypeStruct((b, s, GLA_DK), bf16),
                   jax.ShapeDtypeStruct((b, s, GLA_DV), bf16), jax.ShapeDtypeStruct((b, s, 2 * GLA_DK), f32)),
        grid=(b, s // tm),
        in_specs=[tok(d),
                  pl.BlockSpec((1, MOD_CHUNKS, d), lambda i, j: (i, 0, 0)),
                  pl.BlockSpec((1, d), const),
                  pl.BlockSpec(wqkv.shape, const),
                  pl.BlockSpec(wlr.shape, const),
                  pl.BlockSpec(a2bd.shape, const),
                  pl.BlockSpec(ab.shape, const)],
        out_specs=(tok(GLA_DK), tok(GLA_DK), tok(GLA_DV), tok(2 * GLA_DK)),
        compiler_params=pltpu.CompilerParams(dimension_semantics=("arbitrary", "arbitrary"),
                                             vmem_limit_bytes=_vmem_limit(48 * 1024 * 1024)),
        name="qkv",
    )(x, mod3, g, wqkv, wlr, a2bd, ab)


def _gla_kernel(q_ref, k_ref, v_ref, la_ref, s0f_ref, s0b_ref, o_ref, s_sc, oacc, a_sc, ops_a, ops_b, dec_a, dec_b):
    seq = q_ref.shape[1]
    nc = seq // GLA_CHUNK
    c_len = GLA_CHUNK
    s_sc[0:GLA_HEADS] = s0f_ref[0]
    s_sc[GLA_HEADS:2 * GLA_HEADS] = s0b_ref[0]
    row = lax.broadcasted_iota(i32, (c_len, c_len), 0)
    col = lax.broadcasted_iota(i32, (c_len, c_len), 1)
    lower = row >= col
    upper = col >= row
    tri_l = jnp.where(lower, 1.0, 0.0).astype(bf16)
    tri_u = jnp.where(upper, 1.0, 0.0).astype(bf16)
    scale = GLA_HK ** -0.5

    n_hd = 2 * GLA_HEADS
    QD, QS, KI, KR = range(4)

    def chunk_rows(c):
        return pl.ds(pl.multiple_of(c * c_len, c_len), c_len)

    def chunks_of(step):
        return jnp.minimum(step, nc - 1), jnp.maximum(nc - 1 - step, 0)

    def prepare(step, ops, dec):
        tots = []
        for d, c, tri, r_row, t_row in zip((0, 1), chunks_of(step), (tri_l, tri_u), (GLA_HALF - 1, GLA_HALF),
                                           (c_len - 1, 0)):
            rows = chunk_rows(c)
            cum = _tri_cumsum(tri, la_ref[0, rows, d * GLA_DK:(d + 1) * GLA_DK])
            r = cum[r_row:r_row + 1]
            tot = cum[t_row:t_row + 1]
            qd = q_ref[0, rows, :].astype(f32) * (jnp.exp(cum - r) * scale)
            ki = k_ref[0, rows, :].astype(f32) * jnp.exp(r - cum)
            ops[4 * d + QD] = qd.astype(bf16)
            ops[4 * d + QS] = (qd * jnp.exp(r)).astype(bf16)
            ops[4 * d + KI] = ki.astype(bf16)
            ops[4 * d + KR] = (ki * jnp.exp(tot - r)).astype(bf16)
            tots.extend(tot[:, h * GLA_HK:(h + 1) * GLA_HK] for h in range(GLA_HEADS))
        pad = jnp.zeros((c_len - n_hd, GLA_HK), f32)
        dec[...] = jnp.exp(jnp.concatenate(tots + [pad], axis=0).T)

    def matmuls(step, ops, dec, second_visit):
        rows = [chunk_rows(c) for c in chunks_of(step)]
        for d, mask in ((0, lower), (1, upper)):
            for h in range(GLA_HEADS):
                ks = slice(h * GLA_HK, (h + 1) * GLA_HK)
                a = _dot_nt(ops[4 * d + QD, :, ks], ops[4 * d + KI, :, ks])
                a_sc[d * GLA_HEADS + h] = jnp.where(mask, a, 0.0).astype(bf16)
        for d in (0, 1):
            for h in range(GLA_HEADS):
                j = d * GLA_HEADS + h
                ks = slice(h * GLA_HK, (h + 1) * GLA_HK)
                vs = slice(h * GLA_HV, (h + 1) * GLA_HV)
                o = _dot(jnp.concatenate([a_sc[j], ops[4 * d + QS, :, ks]], axis=1),
                         jnp.concatenate([v_ref[0, rows[d], vs], s_sc[j].astype(bf16)], axis=0))
                if second_visit:
                    o_ref[0, rows[d], vs] = (oacc[rows[d], vs] + o).astype(o_ref.dtype)
                else:
                    oacc[rows[d], vs] = o
        for d in (0, 1):
            for h in range(GLA_HEADS):
                j = d * GLA_HEADS + h
                ks = slice(h * GLA_HK, (h + 1) * GLA_HK)
                vs = slice(h * GLA_HV, (h + 1) * GLA_HV)
                s_sc[j] = dec[:, j:j + 1] * s_sc[j] + _dot_tn(ops[4 * d + KR, :, ks], v_ref[0, rows[d], vs])

    def step_pair(second_visit):
        def body(m, carry):
            prepare(2 * m + 1, ops_b, dec_b)
            matmuls(2 * m, ops_a, dec_a, second_visit)
            prepare(2 * m + 2, ops_a, dec_a)
            matmuls(2 * m + 1, ops_b, dec_b, second_visit)
            return carry
        return body

    prepare(0, ops_a, dec_a)
    lax.fori_loop(0, nc // 4, step_pair(False), 0)
    lax.fori_loop(nc // 4, nc // 2, step_pair(True), 0)


def _gla(q, k, v, la, s0f, s0b):
    b, s, _ = q.shape
    tok = lambda w: pl.BlockSpec((1, s, w), lambda i: (i, 0, 0))
    st_spec = pl.BlockSpec((1, GLA_HEADS, GLA_HK, GLA_HV), lambda i: (i, 0, 0, 0))
    return pl.pallas_call(
        _gla_kernel,
        out_shape=jax.ShapeDtypeStruct((b, s, GLA_DV), bf16),
        grid=(b,),
        in_specs=[tok(GLA_DK), tok(GLA_DK), tok(GLA_DV), tok(2 * GLA_DK), st_spec, st_spec],
        out_specs=tok(GLA_DV),
        scratch_shapes=[pltpu.VMEM((2 * GLA_HEADS, GLA_HK, GLA_HV), f32),
                        pltpu.VMEM((s, GLA_DV), f32),
                        pltpu.VMEM((2 * GLA_HEADS, GLA_CHUNK, GLA_CHUNK), bf16),
                        pltpu.VMEM((8, GLA_CHUNK, GLA_DK), bf16), pltpu.VMEM((8, GLA_CHUNK, GLA_DK), bf16),
                        pltpu.VMEM((GLA_CHUNK, GLA_HK), f32), pltpu.VMEM((GLA_CHUNK, GLA_HK), f32)],
        compiler_params=pltpu.CompilerParams(dimension_semantics=("arbitrary",),
                                             vmem_limit_bytes=V7X_SCOPED_VMEM_CAP),
        name="gla",
    )(q, k, v, la, s0f, s0b)


def _mix_kernel(x_ref, o_ref, mod_ref, n1_ref, n2_ref, wugg_ref, band_ref, icnt_ref, pwbd_ref, pscale_ref,
                wpool_ref, onorm_ref, wgla_ref, wo_ref, wr_ref, br_ref,
                x1_ref, h2_ref, ri_ref, rw_ref):
    x = x_ref[0]
    tm = x.shape[0]
    m = mod_ref[0]
    hx = _norm_mod(x, n1_ref[...], m[1:2], m[0:1]).astype(bf16)
    p = _dot(hx, wugg_ref[...])
    u = p[:, :POOL_WIDTH]
    g = p[:, POOL_WIDTH:POOL_WIDTH + GLA_DV]
    gates = p[:, POOL_WIDTH + GLA_DV:]

    ub = u.astype(bf16)
    slab = band_ref.shape[1]
    sums = []
    for s in range(tm // slab):
        rs = slice(s * slab, (s + 1) * slab)
        sums.append(jnp.concatenate(
            [_dot(band_ref[gi], ub[rs, gi * POOL_GROUP_DIM:(gi + 1) * POOL_GROUP_DIM])
             for gi in range(len(POOL_WINDOWS))], axis=1))
    pooled = jnp.concatenate(sums, axis=0) * icnt_ref[...] - u
    y_pool = _dot(pooled.astype(bf16), pwbd_ref[...]) * pscale_ref[...]
    y_pool = _dot(y_pool.astype(bf16), wpool_ref[...])

    o = o_ref[0].astype(f32)
    normed = []
    for h in range(GLA_HEADS):
        oh = o[:, h * GLA_HV:(h + 1) * GLA_HV]
        normed.append(oh * lax.rsqrt(jnp.mean(oh * oh, axis=-1, keepdims=True) + EPS))
    on = jnp.concatenate(normed, axis=1) * onorm_ref[...]
    y_gla = _dot((on * _silu(g)).astype(bf16), wgla_ref[...])

    mixed = _sigmoid(gates[:, :D_MODEL]) * y_pool + _sigmoid(gates[:, D_MODEL:]) * y_gla
    x1 = x + m[2:3] * _dot(mixed.astype(bf16), wo_ref[...])
    x1_ref[0] = x1

    h2f = _norm_mod(x1, n2_ref[...], m[4:5], m[3:4])
    _rows_to_tiles(h2_ref, h2f)
    h2 = h2f.astype(bf16)

    lt = (_dot(h2, wr_ref[...]) + br_ref[...]).T
    g0, g1, g2, g3 = lt[0:1], lt[1:2], lt[2:3], lt[3:4]
    gmax = jnp.maximum(jnp.maximum(g0, g1), jnp.maximum(g2, g3))
    gidx = jnp.where(g0 == gmax, 0, jnp.where(g1 == gmax, 1, jnp.where(g2 == gmax, 2, 3))).astype(i32)
    p_grp = 1.0 / (jnp.exp(g0 - gmax) + jnp.exp(g1 - gmax) + jnp.exp(g2 - gmax) + jnp.exp(g3 - gmax))
    e0 = V7X_SUBLANES
    ing = lt[e0 + 3 * EXPERTS_PER_GROUP:e0 + 4 * EXPERTS_PER_GROUP]
    for gi in (2, 1, 0):
        ing = jnp.where(gidx == gi, lt[e0 + gi * EXPERTS_PER_GROUP:e0 + (gi + 1) * EXPERTS_PER_GROUP], ing)
    rid = lax.broadcasted_iota(i32, ing.shape, 0)
    v1 = jnp.max(ing, axis=0, keepdims=True)
    i1 = jnp.min(jnp.where(ing == v1, rid, EXPERTS_PER_GROUP), axis=0, keepdims=True)
    rest = jnp.where(rid == i1, -jnp.inf, ing)
    v2 = jnp.max(rest, axis=0, keepdims=True)
    i2 = jnp.min(jnp.where(rest == v2, rid, EXPERTS_PER_GROUP), axis=0, keepdims=True)
    t = jnp.exp(v2 - v1)
    w1 = p_grp / (1.0 + t)
    w2 = p_grp * t / (1.0 + t)
    first_low = i1 < i2
    ilo = jnp.minimum(i1, i2)
    ihi = jnp.maximum(i1, i2)
    pair = ilo * (EXPERTS_PER_GROUP - 1) - ((ilo * (ilo - 1)) >> 1) + (ihi - ilo - 1)
    pid = gidx * (N_PAIRS // N_GROUPS) + pair
    ri_ref[...] = jnp.where(rid == 0, pid, 0)
    rw_ref[...] = jnp.where(rid == 0, jnp.where(first_low, w1, w2),
                            jnp.where(rid == 1, jnp.where(first_low, w2, w1), 0.0))


def _mix(x, o, mod3, n1, n2, wugg, band, icnt, pwbd, pscale, wpool, onorm, wgla, wo, wr, br):
    b, s, d = x.shape
    tm = TOK_TILE
    nj = s // tm
    t = b * s
    const2 = lambda i, j: (0, 0)
    const3 = lambda i, j: (0, 0, 0)
    tok = lambda w: pl.BlockSpec((1, tm, w), lambda i, j: (i, j, 0))
    lane = pl.BlockSpec((V7X_SUBLANES, tm), lambda i, j: (0, i * nj + j))
    return pl.pallas_call(
        _mix_kernel,
        out_shape=(jax.ShapeDtypeStruct((b, s, d), f32), jax.ShapeDtypeStruct((t * ROW_TILE, V7X_LANES), f32),
                   jax.ShapeDtypeStruct((V7X_SUBLANES, t), i32), jax.ShapeDtypeStruct((V7X_SUBLANES, t), f32)),
        grid=(b, nj),
        in_specs=[tok(d), tok(GLA_DV),
                  pl.BlockSpec((1, MOD_CHUNKS, d), lambda i, j: (i, 0, 0)),
                  pl.BlockSpec((1, d), const2), pl.BlockSpec((1, d), const2),
                  pl.BlockSpec(wugg.shape, const2),
                  pl.BlockSpec(band.shape, const3),
                  pl.BlockSpec(icnt.shape, const2),
                  pl.BlockSpec(pwbd.shape, const2),
                  pl.BlockSpec(pscale.shape, const2),
                  pl.BlockSpec(wpool.shape, const2),
                  pl.BlockSpec(onorm.shape, const2),
                  pl.BlockSpec(wgla.shape, const2),
                  pl.BlockSpec(wo.shape, const2),
                  pl.BlockSpec(wr.shape, const2),
                  pl.BlockSpec(br.shape, const2)],
        out_specs=(tok(d), pl.BlockSpec((tm * ROW_TILE, V7X_LANES), lambda i, j: (i * nj + j, 0)), lane, lane),
        compiler_params=pltpu.CompilerParams(dimension_semantics=("arbitrary", "arbitrary"),
                                             vmem_limit_bytes=V7X_SCOPED_VMEM_CAP),
        name="mix",
    )(x, o, mod3, n1, n2, wugg, band, icnt, pwbd, pscale, wpool, onorm, wgla, wo, wr, br)


def _moe_kernel(ta_ref, tb_ref, g0_ref, xs_ref, xg_ref, pa_ref, pb_ref,
                idx0_ref, idx1_ref, idxn0_ref, idxn1_ref, idxp0_ref, idxp1_ref, idxl_ref,
                wgt0_ref, wgt1_ref, h2_ref, w13_any, w2_any, w13a0_ref, w2a0_ref, w13b0_ref, w2b0_ref,
                w13a1_ref, w2a1_ref, w13b1_ref, w2b1_ref, y_ref,
                xbuf, ybuf, gsem, ssem, xt_sc, yacc, wx13, wx2, wsem):
    del ta_ref, tb_ref
    j = pl.program_id(0)
    last = pl.num_programs(0) - 1
    tm = xbuf.shape[1] // ROW_TILE

    def token_tile(ref, off):
        return ref.at[pl.ds(pl.multiple_of(off, ROW_TILE), ROW_TILE)]

    def gather_row(off, s, r8):
        return pltpu.make_async_copy(token_tile(h2_ref, off), xbuf.at[s, pl.ds(r8, ROW_TILE)], gsem.at[s])

    def scatter_row(off, s, r8):
        return pltpu.make_async_copy(ybuf.at[s, pl.ds(r8, ROW_TILE)], token_tile(y_ref, off), ssem.at[s])

    def start_rows(make, idx_ref):
        def body(r, carry):
            make(idx_ref[0, 0, r], pl.multiple_of(r * ROW_TILE, ROW_TILE)).start()
            return carry
        lax.fori_loop(0, tm, body, 0, unroll=8)

    def gather_tile(s):
        return pltpu.make_async_copy(h2_ref.at[pl.ds(0, tm * ROW_TILE)], xbuf.at[s], gsem.at[s])

    def scatter_tile(s):
        return pltpu.make_async_copy(ybuf.at[s], y_ref.at[pl.ds(0, tm * ROW_TILE)], ssem.at[s])

    @pl.when(j == 0)
    def _():
        ybuf[...] = jnp.zeros(ybuf.shape, ybuf.dtype)
        start_rows(lambda off, r8: gather_row(off, 0, r8), idx0_ref)
        start_rows(lambda off, r8: gather_row(off, 1, r8), idx1_ref)

    n_pieces = 12
    per_piece = -(-tm // n_pieces)
    half = D_EXPERT // 2

    def pair_pass(xt, w_first, w_second, weights, between):
        y = None
        for w_col, (w13, w2) in zip((w_first, w_second), weights):
            acc = None
            for c in range(2):
                h1 = _dot(xt, w13(c * half, (c + 1) * half))
                between()
                h3 = _dot(xt, w13(D_EXPERT + c * half, D_EXPERT + (c + 1) * half))
                between()
                part = _dot((_silu(h1) * h3).astype(bf16), w2(c * half, (c + 1) * half))
                between()
                acc = part if acc is None else acc + part
            y = w_col * acc if y is None else y + w_col * acc
        return y

    def pair_weights(wgt_ref, pair):
        w = wgt_ref[...]
        mine = w[:, 2:3] == pair.astype(f32)
        return jnp.where(mine, w[:, 0:1], 0.0), jnp.where(mine, w[:, 1:2], 0.0)

    def tile_body(s, tile, idxn_ref, idxp_ref, wgt_ref, w13a_ref, w2a_ref, w13b_ref, w2b_ref):
        first_extra = xs_ref[tile]
        n_extra = xs_ref[tile + 1] - first_extra

        def extra_weight_copies(k):
            pair = xg_ref[first_extra + k]
            return [pltpu.make_async_copy(src.at[e], dst.at[slot], wsem.at[n])
                    for n, (src, dst, e, slot) in enumerate(((w13_any, wx13, pa_ref[pair], 0),
                                                             (w2_any, wx2, pa_ref[pair], 0),
                                                             (w13_any, wx13, pb_ref[pair], 1),
                                                             (w2_any, wx2, pb_ref[pair], 1)))]

        @pl.when(n_extra > 0)
        def _():
            for cp in extra_weight_copies(0):
                cp.start()

        gather_tile(s).wait()
        xt = _tiles_to_rows(xbuf.at[s]).astype(bf16)
        xt_sc[...] = xt
        piece = [0]

        def issue_rows():
            lo = piece[0] * per_piece
            rows = range(lo, min(lo + per_piece, tm))
            src_rows = [idxn_ref[0, 0, r] for r in rows]
            dst_rows = [idxp_ref[0, 0, r] for r in rows]
            for r, sr, dr in zip(rows, src_rows, dst_rows):
                gather_row(sr, s, r * ROW_TILE).start(priority=0)
                scatter_row(dr, 1 - s, r * ROW_TILE).start(priority=1)
            piece[0] += 1

        block = lambda w13_ref, w2_ref: (lambda c0, c1: w13_ref[0, :, c0:c1], lambda r0, r1: w2_ref[0, r0:r1, :])
        w_first, w_second = pair_weights(wgt_ref, g0_ref[tile])
        yacc[...] = pair_pass(xt, w_first, w_second, (block(w13a_ref, w2a_ref), block(w13b_ref, w2b_ref)), issue_rows)
        assert piece[0] == n_pieces

        def extra_pass(k, carry):
            for cp in extra_weight_copies(k):
                cp.wait()
            copied = lambda slot: (lambda c0, c1: wx13[slot, :, c0:c1], lambda r0, r1: wx2[slot, r0:r1, :])
            w_lo, w_hi = pair_weights(wgt_ref, xg_ref[first_extra + k])
            yacc[...] += pair_pass(xt_sc[...], w_lo, w_hi, (copied(0), copied(1)), lambda: None)

            @pl.when(k + 1 < n_extra)
            def _():
                for cp in extra_weight_copies(k + 1):
                    cp.start()

            return carry

        lax.fori_loop(0, n_extra, extra_pass, 0)

        @pl.when(tile >= 1)
        def _():
            scatter_tile(s).wait()

        _rows_to_tiles(ybuf.at[s], yacc[...])

    tile_body(0, 2 * j, idxn0_ref, idxp0_ref, wgt0_ref, w13a0_ref, w2a0_ref, w13b0_ref, w2b0_ref)
    tile_body(1, 2 * j + 1, idxn1_ref, idxp1_ref, wgt1_ref, w13a1_ref, w2a1_ref, w13b1_ref, w2b1_ref)

    @pl.when(j == last)
    def _():
        scatter_tile(0).wait()
        start_rows(lambda off, r8: scatter_row(off, 1, r8), idxl_ref)
        scatter_tile(1).wait()
        gather_tile(0).wait()
        gather_tile(1).wait()


def _moe(tile_a, tile_b, first_pair, extra_start, extra_pair, pair_a, pair_b, idx, wgt, h2, w13, w2):
    nt, tm = idx.shape[0] - 1, idx.shape[2]
    assert nt % 2 == 0
    rows, lanes = h2.shape
    d = ROW_TILE * lanes
    idx_blk = (1, 1, tm)
    smem_idx = lambda fn: pl.BlockSpec(idx_blk, lambda j, *_: fn(j), memory_space=pltpu.SMEM)
    w13_blk = (1, d, 2 * D_EXPERT)
    w2_blk = (1, D_EXPERT, d)

    def weights(s):
        return [pl.BlockSpec(w13_blk, lambda j, ta, tb, *_: (ta[2 * j + s], 0, 0)),
                pl.BlockSpec(w2_blk, lambda j, ta, tb, *_: (ta[2 * j + s], 0, 0)),
                pl.BlockSpec(w13_blk, lambda j, ta, tb, *_: (tb[2 * j + s], 0, 0)),
                pl.BlockSpec(w2_blk, lambda j, ta, tb, *_: (tb[2 * j + s], 0, 0))]

    any_space = pl.BlockSpec(memory_space=pl.ANY)
    grid_spec = pltpu.PrefetchScalarGridSpec(
        num_scalar_prefetch=7,
        grid=(nt // 2,),
        in_specs=[smem_idx(lambda j: (0, 0, 0)),
                  smem_idx(lambda j: (1, 0, 0)),
                  smem_idx(lambda j: (jnp.minimum(2 * j + 2, nt - 1), 0, 0)),
                  smem_idx(lambda j: (jnp.minimum(2 * j + 3, nt - 1), 0, 0)),
                  smem_idx(lambda j: (jnp.where(j == 0, nt, 2 * j - 1), 0, 0)),
                  smem_idx(lambda j: (2 * j, 0, 0)),
                  smem_idx(lambda j: (nt - 1, 0, 0)),
                  pl.BlockSpec((tm, wgt.shape[1]), lambda j, *_: (2 * j, 0)),
                  pl.BlockSpec((tm, wgt.shape[1]), lambda j, *_: (2 * j + 1, 0)),
                  any_space, any_space, any_space] + weights(0) + weights(1),
        out_specs=any_space,
        scratch_shapes=[pltpu.VMEM((2, tm * ROW_TILE, lanes), f32), pltpu.VMEM((2, tm * ROW_TILE, lanes), f32),
                        pltpu.SemaphoreType.DMA((2,)), pltpu.SemaphoreType.DMA((2,)),
                        pltpu.VMEM((tm, d), bf16), pltpu.VMEM((tm, d), f32),
                        pltpu.VMEM((2, d, 2 * D_EXPERT), bf16), pltpu.VMEM((2, D_EXPERT, d), bf16),
                        pltpu.SemaphoreType.DMA((4,))],
    )
    return pl.pallas_call(
        _moe_kernel,
        out_shape=jax.ShapeDtypeStruct((rows + tm * ROW_TILE, lanes), f32),
        grid_spec=grid_spec,
        compiler_params=pltpu.CompilerParams(dimension_semantics=("arbitrary",),
                                             vmem_limit_bytes=_vmem_limit(48 * 1024 * 1024)),
        name="moe",
    )(tile_a, tile_b, first_pair, extra_start, extra_pair, pair_a, pair_b,
      idx, idx, idx, idx, idx, idx, idx, wgt, wgt, h2, w13, w2, w13, w2, w13, w2, w13, w2, w13, w2)


def _pair_tables():
    a, b = [], []
    for g in range(N_GROUPS):
        for i in range(EXPERTS_PER_GROUP):
            for j in range(i + 1, EXPERTS_PER_GROUP):
                a.append(g * EXPERTS_PER_GROUP + i)
                b.append(g * EXPERTS_PER_GROUP + j)
    return np.asarray(a, np.int32), np.asarray(b, np.int32)


def _dispatch_plan(pid, wlo, whi, tm):
    t = pid.shape[0]
    nt = t // tm
    assert t % (2 * tm) == 0
    pid_s, tok_s, wlo_s, whi_s = lax.sort((pid, jnp.arange(t, dtype=i32), wlo, whi), num_keys=1, is_stable=True)
    pairs = jnp.arange(N_PAIRS, dtype=i32)
    counts = jnp.sum((pid[:, None] == pairs[None, :]).astype(i32), axis=0)
    starts = jnp.cumsum(counts) - counts
    first_pair = pid_s.reshape(nt, tm)[:, 0]
    is_extra = jnp.logical_and(counts > 0, starts % tm != 0)
    extra_tile = starts // tm
    tile = jnp.arange(nt + 1, dtype=i32)
    extra_start = jnp.sum(jnp.logical_and(is_extra[None, :], extra_tile[None, :] < tile[:, None]).astype(i32), axis=1)
    extra_rank = jnp.cumsum(is_extra.astype(i32)) - is_extra.astype(i32)
    hit = jnp.logical_and(is_extra[None, :], extra_rank[None, :] == pairs[:, None])
    extra_pair = jnp.sum(jnp.where(hit, pairs[None, :], 0), axis=1)
    pa, pb = (jnp.asarray(tab) for tab in _pair_tables())
    is_first = (first_pair[:, None] == pairs[None, :]).astype(i32)
    row = jnp.arange(tm, dtype=i32)[None, :]
    idx = jnp.concatenate([tok_s.reshape(nt, tm), t + row], axis=0).reshape(nt + 1, 1, tm) * ROW_TILE
    wgt = jnp.stack([wlo_s, whi_s, pid_s.astype(f32), jnp.zeros_like(wlo_s)], axis=-1)
    return (jnp.sum(is_first * pa[None, :], axis=1), jnp.sum(is_first * pb[None, :], axis=1), first_pair,
            extra_start, extra_pair, pa, pb, idx, wgt)


def _final_kernel(x1_ref, y_ref, mod_ref, g_ref, out_ref):
    m = mod_ref[0]
    x2 = x1_ref[0] + m[5:6] * _tiles_to_rows(y_ref)
    ms = jnp.mean(x2 * x2, axis=-1, keepdims=True)
    out_ref[0] = x2 * lax.rsqrt(ms + EPS) * g_ref[...]


def _final(x1, y, mod3, g):
    b, s, d = x1.shape
    tm = TOK_TILE
    nj = s // tm
    tok = pl.BlockSpec((1, tm, d), lambda i, j: (i, j, 0))
    return pl.pallas_call(
        _final_kernel,
        out_shape=jax.ShapeDtypeStruct((b, s, d), f32),
        grid=(b, nj),
        in_specs=[tok,
                  pl.BlockSpec((tm * ROW_TILE, y.shape[1]), lambda i, j: (i * nj + j, 0)),
                  pl.BlockSpec((1, MOD_CHUNKS, d), lambda i, j: (i, 0, 0)),
                  pl.BlockSpec((1, d), lambda i, j: (0, 0))],
        out_specs=tok,
        compiler_params=pltpu.CompilerParams(dimension_semantics=("arbitrary", "arbitrary"),
                                             vmem_limit_bytes=_vmem_limit(32 * 1024 * 1024)),
        name="final",
    )(x1, y, mod3, g)


def _pool_constants(tm):
    slab = V7X_MXU_DIM
    band = np.zeros((len(POOL_WINDOWS), slab, slab), np.float32)
    icnt = np.zeros((tm, POOL_WIDTH), np.float32)
    for gi, w in enumerate(POOL_WINDOWS):
        for t in range(slab):
            base, pos = (t // GRID_W) * GRID_W, t % GRID_W
            lo, hi = max(pos - w // 2, 0), min(pos + w // 2, GRID_W)
            band[gi, t, base + lo:base + hi] = 1.0
        for t in range(tm):
            pos = t % GRID_W
            cnt = min(pos + w // 2, GRID_W) - max(pos - w // 2, 0)
            icnt[t, gi * POOL_GROUP_DIM:(gi + 1) * POOL_GROUP_DIM] = 1.0 / cnt
    return jnp.asarray(band, bf16), jnp.asarray(icnt, f32)


def _block_diag(blocks):
    n = len(blocks)
    r, c = blocks[0].shape
    out = jnp.zeros((n * r, n * c), blocks[0].dtype)
    for i, blk in enumerate(blocks):
        out = out.at[i * r:(i + 1) * r, i * c:(i + 1) * c].set(blk)
    return out


def kernel(x, c, ctx, c_ctx, w_mod, b_mod, norm1_g, norm2_g, w_in, gla_a2_f, gla_ab_f, gla_a2_b, gla_ab_b,
           gla_onorm_g, pool_w, pool_scale, w_pool_br, w_gla_br, w_o, router_grp_w, router_grp_b,
           router_exp_w, router_exp_b, moe_w1, moe_w3, moe_w2, final_norm_g):
    b, s, d = x.shape
    depth = w_mod.shape[0]
    assert depth == 1 and d == D_MODEL and s % TOK_TILE == 0 and s % (4 * GLA_CHUNK) == 0
    assert TOK_TILE % V7X_MXU_DIM == 0 and V7X_MXU_DIM % GRID_W == 0
    t = b * s

    cond = jnp.concatenate([c, c_ctx[None, :], jnp.zeros((COND_PAD - 1, d), f32)], axis=0)
    mod3 = _adaln(cond, w_mod[0], b_mod[0][None, :]).reshape(b + COND_PAD, MOD_CHUNKS, d)

    w_in_b = w_in[0].astype(bf16)
    wqkv = w_in_b[:, _Q0:_G0]
    wkv = w_in_b[:, _K0:_G0]
    wugg = jnp.concatenate([w_in_b[:, _U0:_Q0], w_in_b[:, _G0:_LR0]], axis=1)
    wlr = jnp.pad(w_in_b[:, _LR0:_IN_END], ((0, 0), (0, V7X_LANES - 2 * GLA_GATE_RANK)))
    a2bd = jnp.pad(_block_diag([gla_a2_f[0], gla_a2_b[0]]).astype(bf16),
                   ((0, V7X_LANES - 2 * GLA_GATE_RANK), (0, 0)))
    ab = jnp.concatenate([gla_ab_f[0], gla_ab_b[0]])[None, :]
    n1 = norm1_g[0][None, :]
    n2 = norm2_g[0][None, :]

    s0f, s0b = _ctx_states(ctx, mod3, b, n1, wkv, wlr, a2bd, ab)
    q, k, v, la = _qkv(x, mod3, n1, wqkv, wlr, a2bd, ab)
    o = _gla(q, k, v, la, s0f, s0b)

    band, icnt = _pool_constants(TOK_TILE)
    pwbd = _block_diag([pool_w[0, gi] for gi in range(len(POOL_WINDOWS))]).astype(bf16)
    onorm = jnp.tile(gla_onorm_g[0], GLA_HEADS)[None, :]
    wr = jnp.zeros((d, V7X_LANES), f32)
    wr = wr.at[:, :N_GROUPS].set(router_grp_w[0]).at[:, V7X_SUBLANES:V7X_SUBLANES + N_EXPERTS].set(router_exp_w[0])
    br = jnp.zeros((1, V7X_LANES), f32)
    br = br.at[0, :N_GROUPS].set(router_grp_b[0]).at[0, V7X_SUBLANES:V7X_SUBLANES + N_EXPERTS].set(router_exp_b[0])
    x1, h2, ri, rw = _mix(x, o, mod3, n1, n2, wugg, band, icnt, pwbd, pool_scale[0][None, :],
                           w_pool_br[0].astype(bf16), onorm, w_gla_br[0].astype(bf16), w_o[0].astype(bf16),
                           wr.astype(bf16), br)

    plan = _dispatch_plan(ri[0], rw[0], rw[1], MOE_TILE)
    w13 = jnp.concatenate([moe_w1[0], moe_w3[0]], axis=-1).astype(bf16)
    y = _moe(*plan, h2, w13, moe_w2[0].astype(bf16))
    return _final(x1, y, mod3, final_norm_g[None, :])
```

```python
import jax
import jax.numpy as jnp
import numpy as np
from jax import lax
from jax.experimental import pallas as pl
from jax.experimental.pallas import tpu as pltpu

f32 = jnp.float32
bf16 = jnp.bfloat16
i32 = jnp.int32
u32 = jnp.uint32

D_MODEL = 1024
GRID_W = 64
POOL_WINDOWS = (2, 4, 8, 16)
POOL_GROUP_DIM = 128
POOL_WIDTH = len(POOL_WINDOWS) * POOL_GROUP_DIM
GLA_HEADS = 4
GLA_DK = 512
GLA_DV = 1024
GLA_HK = GLA_DK // GLA_HEADS
GLA_HV = GLA_DV // GLA_HEADS
GLA_GATE_RANK = 16
GLA_GATE_NORMALIZER = 16.0
N_GROUPS = 4
EXPERTS_PER_GROUP = 8
N_EXPERTS = N_GROUPS * EXPERTS_PER_GROUP
D_EXPERT = 512
MOD_CHUNKS = 6
EPS = 1e-6

_U0, _Q0, _K0, _V0, _G0, _GT0, _LR0, _IN_END = 0, 512, 1024, 1536, 2560, 3584, 5632, 5664

V7X_LANES = 128
V7X_SUBLANES = 8
V7X_MXU_DIM = 256
V7X_SCOPED_VMEM_CAP = 60000 * 1024

GLA_CHUNK = 128
GLA_HALF = GLA_CHUNK // 2
TOK_TILE = 512
MOE_TILE = 256
COND_PAD = 8
N_PAIRS = N_GROUPS * (EXPERTS_PER_GROUP * (EXPERTS_PER_GROUP - 1) // 2)


def _vmem_limit(nbytes):
    return int(min(max(nbytes, 16 * 1024 * 1024), V7X_SCOPED_VMEM_CAP))


def _dot(a, b):
    return jnp.dot(a, b, preferred_element_type=f32)


def _dot_nt(a, b):
    return lax.dot_general(a, b, (((1,), (1,)), ((), ())), preferred_element_type=f32)


def _dot_tn(a, b):
    return lax.dot_general(a, b, (((0,), (0,)), ((), ())), preferred_element_type=f32)


def _sigmoid(x):
    return 1.0 / (1.0 + jnp.exp(-x))


def _silu(x):
    return x * _sigmoid(x)


def _log_sigmoid(z):
    return jnp.minimum(z, 0.0) - jnp.log1p(jnp.exp(-jnp.abs(z)))


def _norm_mod(x, g, scale, shift):
    ms = jnp.mean(x * x, axis=-1, keepdims=True)
    return (x * lax.rsqrt(ms + EPS) * g) * (1.0 + scale) + shift


def _tri_cumsum(tri, la):
    hi = la.astype(bf16)
    lo = (la - hi.astype(f32)).astype(bf16)
    return _dot(tri, hi) + _dot(tri, lo)


ROW_TILE = V7X_SUBLANES


def _rows_to_tiles(ref, val):
    n = val.shape[0]
    for a in range(ROW_TILE):
        ref[pl.ds(a, n, stride=ROW_TILE), :] = val[:, a * V7X_LANES:(a + 1) * V7X_LANES]


def _tiles_to_rows(ref):
    n = ref.shape[0] // ROW_TILE
    return jnp.concatenate([ref[pl.ds(a, n, stride=ROW_TILE), :] for a in range(ROW_TILE)], axis=1)


def _log_decays(hx, wlr_ref, a2_ref, ab_ref):
    lr = _dot(hx, wlr_ref[...])
    z = _dot(lr.astype(bf16), a2_ref[...]) + ab_ref[...]
    return _log_sigmoid(z) * (1.0 / GLA_GATE_NORMALIZER)


def _adaln_kernel(c_ref, w_ref, b_ref, o_ref):
    c = c_ref[...]
    o_ref[...] = _dot(_silu(c).astype(bf16), w_ref[...].astype(bf16)) + b_ref[...]


def _adaln(cond, w_mod, b_mod):
    rows, d = cond.shape
    n = w_mod.shape[1]
    tn = 512
    return pl.pallas_call(
        _adaln_kernel,
        out_shape=jax.ShapeDtypeStruct((rows, n), f32),
        grid=(n // tn,),
        in_specs=[pl.BlockSpec((rows, d), lambda j: (0, 0)),
                  pl.BlockSpec((d, tn), lambda j: (0, j)),
                  pl.BlockSpec((1, tn), lambda j: (0, j))],
        out_specs=pl.BlockSpec((rows, tn), lambda j: (0, j)),
        compiler_params=pltpu.CompilerParams(dimension_semantics=("arbitrary",)),
        name="adaln",
    )(cond, w_mod, b_mod)


def _ctx_kernel(ctx_ref, mod_ref, g_ref, wkv_ref, wlr_ref, a2_ref, ab_ref, sf_ref, sb_ref):
    x = ctx_ref[0]
    n = x.shape[0]
    m = mod_ref[0]
    hc = _norm_mod(x, g_ref[...], m[1:2], m[0:1]).astype(bf16)
    kv = _dot(hc, wkv_ref[...])
    la = _log_decays(hc, wlr_ref, a2_ref, ab_ref)
    row = lax.broadcasted_iota(i32, (n, n), 0)
    col = lax.broadcasted_iota(i32, (n, n), 1)
    lower = jnp.where(row >= col, 1.0, 0.0).astype(bf16)
    upper = jnp.where(col >= row, 1.0, 0.0).astype(bf16)
    cum_f = _tri_cumsum(lower, la[:, :GLA_DK])
    cum_b = _tri_cumsum(upper, la[:, GLA_DK:])
    k = kv[:, :GLA_DK]
    v = kv[:, GLA_DK:].astype(bf16)
    kr_f = (k * jnp.exp(cum_f[n - 1:n] - cum_f)).astype(bf16)
    kr_b = (k * jnp.exp(cum_b[0:1] - cum_b)).astype(bf16)
    for h in range(GLA_HEADS):
        ks = slice(h * GLA_HK, (h + 1) * GLA_HK)
        vs = slice(h * GLA_HV, (h + 1) * GLA_HV)
        sf_ref[0, h] = _dot_tn(kr_f[:, ks], v[:, vs])
        sb_ref[0, h] = _dot_tn(kr_b[:, ks], v[:, vs])


def _ctx_states(ctx, mod3, ctx_row, g, wkv, wlr, a2bd, ab):
    b, n, d = ctx.shape
    st = jax.ShapeDtypeStruct((b, GLA_HEADS, GLA_HK, GLA_HV), f32)
    const = lambda i: (0, 0)
    st_spec = pl.BlockSpec((1, GLA_HEADS, GLA_HK, GLA_HV), lambda i: (i, 0, 0, 0))
    return pl.pallas_call(
        _ctx_kernel,
        out_shape=(st, st),
        grid=(b,),
        in_specs=[pl.BlockSpec((1, n, d), lambda i: (i, 0, 0)),
                  pl.BlockSpec((1, MOD_CHUNKS, d), lambda i: (ctx_row, 0, 0)),
                  pl.BlockSpec((1, d), const),
                  pl.BlockSpec(wkv.shape, const),
                  pl.BlockSpec(wlr.shape, const),
                  pl.BlockSpec(a2bd.shape, const),
                  pl.BlockSpec(ab.shape, const)],
        out_specs=(st_spec, st_spec),
        compiler_params=pltpu.CompilerParams(dimension_semantics=("arbitrary",),
                                             vmem_limit_bytes=_vmem_limit(32 * 1024 * 1024)),
        name="ctx_states",
    )(ctx, mod3, g, wkv, wlr, a2bd, ab)


def _qkv_kernel(x_ref, mod_ref, g_ref, wqkv_ref, wlr_ref, a2_ref, ab_ref, q_ref, k_ref, v_ref, la_ref):
    m = mod_ref[0]
    hx = _norm_mod(x_ref[0], g_ref[...], m[1:2], m[0:1]).astype(bf16)
    p = _dot(hx, wqkv_ref[...])
    q_ref[0] = p[:, :GLA_DK].astype(bf16)
    k_ref[0] = p[:, GLA_DK:2 * GLA_DK].astype(bf16)
    v_ref[0] = p[:, 2 * GLA_DK:].astype(bf16)
    la_ref[0] = _log_decays(hx, wlr_ref, a2_ref, ab_ref)


def _qkv(x, mod3, g, wqkv, wlr, a2bd, ab):
    b, s, d = x.shape
    tm = TOK_TILE
    const = lambda i, j: (0, 0)
    tok = lambda w: pl.BlockSpec((1, tm, w), lambda i, j: (i, j, 0))
    return pl.pallas_call(
        _qkv_kernel,
        out_shape=(jax.ShapeDtypeStruct((b, s, GLA_DK), bf16), jax.ShapeDtypeStruct((b, s, GLA_DK), bf16),
                   jax.ShapeDtypeStruct((b, s, GLA_DV), bf16), jax.ShapeDtypeStruct((b, s, 2 * GLA_DK), f32)),
        grid=(b, s // tm),
        in_specs=[tok(d),
                  pl.BlockSpec((1, MOD_CHUNKS, d), lambda i, j: (i, 0, 0)),
                  pl.BlockSpec((1, d), const),
                  pl.BlockSpec(wqkv.shape, const),
                  pl.BlockSpec(wlr.shape, const),
                  pl.BlockSpec(a2bd.shape, const),
                  pl.BlockSpec(ab.shape, const)],
        out_specs=(tok(GLA_DK), tok(GLA_DK), tok(GLA_DV), tok(2 * GLA_DK)),
        compiler_params=pltpu.CompilerParams(dimension_semantics=("arbitrary", "arbitrary"),
                                             vmem_limit_bytes=_vmem_limit(48 * 1024 * 1024)),
        name="qkv",
    )(x, mod3, g, wqkv, wlr, a2bd, ab)


def _gla_kernel(q_ref, k_ref, v_ref, la_ref, s0f_ref, s0b_ref, o_ref, s_sc, oacc, a_sc, ops_a, ops_b, dec_a, dec_b):
    seq = q_ref.shape[1]
    nc = seq // GLA_CHUNK
    c_len = GLA_CHUNK
    s_sc[0:GLA_HEADS] = s0f_ref[0]
    s_sc[GLA_HEADS:2 * GLA_HEADS] = s0b_ref[0]
    row = lax.broadcasted_iota(i32, (c_len, c_len), 0)
    col = lax.broadcasted_iota(i32, (c_len, c_len), 1)
    lower = row >= col
    upper = col >= row
    tri_l = jnp.where(lower, 1.0, 0.0).astype(bf16)
    tri_u = jnp.where(upper, 1.0, 0.0).astype(bf16)
    scale = GLA_HK ** -0.5

    n_hd = 2 * GLA_HEADS
    QD, QS, KI, KR = range(4)

    def chunk_rows(c):
        return pl.ds(pl.multiple_of(c * c_len, c_len), c_len)

    def chunks_of(step):
        return jnp.minimum(step, nc - 1), jnp.maximum(nc - 1 - step, 0)

    def prepare(step, ops, dec):
        tots = []
        for d, c, tri, r_row, t_row in zip((0, 1), chunks_of(step), (tri_l, tri_u), (GLA_HALF - 1, GLA_HALF),
                                           (c_len - 1, 0)):
            rows = chunk_rows(c)
            cum = _tri_cumsum(tri, la_ref[0, rows, d * GLA_DK:(d + 1) * GLA_DK])
            r = cum[r_row:r_row + 1]
            tot = cum[t_row:t_row + 1]
            qd = q_ref[0, rows, :].astype(f32) * (jnp.exp(cum - r) * scale)
            ki = k_ref[0, rows, :].astype(f32) * jnp.exp(r - cum)
            ops[4 * d + QD] = qd.astype(bf16)
            ops[4 * d + QS] = (qd * jnp.exp(r)).astype(bf16)
            ops[4 * d + KI] = ki.astype(bf16)
            ops[4 * d + KR] = (ki * jnp.exp(tot - r)).astype(bf16)
            tots.extend(tot[:, h * GLA_HK:(h + 1) * GLA_HK] for h in range(GLA_HEADS))
        pad = jnp.zeros((c_len - n_hd, GLA_HK), f32)
        dec[...] = jnp.exp(jnp.concatenate(tots + [pad], axis=0).T)

    def matmuls(step, ops, dec, second_visit):
        rows = [chunk_rows(c) for c in chunks_of(step)]
        for d, mask in ((0, lower), (1, upper)):
            for h in range(GLA_HEADS):
                ks = slice(h * GLA_HK, (h + 1) * GLA_HK)
                a = _dot_nt(ops[4 * d + QD, :, ks], ops[4 * d + KI, :, ks])
                a_sc[d * GLA_HEADS + h] = jnp.where(mask, a, 0.0).astype(bf16)
        for d in (0, 1):
            for h in range(GLA_HEADS):
                j = d * GLA_HEADS + h
                ks = slice(h * GLA_HK, (h + 1) * GLA_HK)
                vs = slice(h * GLA_HV, (h + 1) * GLA_HV)
                o = _dot(jnp.concatenate([a_sc[j], ops[4 * d + QS, :, ks]], axis=1),
                         jnp.concatenate([v_ref[0, rows[d], vs], s_sc[j].astype(bf16)], axis=0))
                if second_visit:
                    o_ref[0, rows[d], vs] = (oacc[rows[d], vs] + o).astype(o_ref.dtype)
                else:
                    oacc[rows[d], vs] = o
        for d in (0, 1):
            for h in range(GLA_HEADS):
                j = d * GLA_HEADS + h
                ks = slice(h * GLA_HK, (h + 1) * GLA_HK)
                vs = slice(h * GLA_HV, (h + 1) * GLA_HV)
                s_sc[j] = dec[:, j:j + 1] * s_sc[j] + _dot_tn(ops[4 * d + KR, :, ks], v_ref[0, rows[d], vs])

    def step_pair(second_visit):
        def body(m, carry):
            prepare(2 * m + 1, ops_b, dec_b)
            matmuls(2 * m, ops_a, dec_a, second_visit)
            prepare(2 * m + 2, ops_a, dec_a)
            matmuls(2 * m + 1, ops_b, dec_b, second_visit)
            return carry
        return body

    prepare(0, ops_a, dec_a)
    lax.fori_loop(0, nc // 4, step_pair(False), 0)
    lax.fori_loop(nc // 4, nc // 2, step_pair(True), 0)


def _gla(q, k, v, la, s0f, s0b):
    b, s, _ = q.shape
    tok = lambda w: pl.BlockSpec((1, s, w), lambda i: (i, 0, 0))
    st_spec = pl.BlockSpec((1, GLA_HEADS, GLA_HK, GLA_HV), lambda i: (i, 0, 0, 0))
    return pl.pallas_call(
        _gla_kernel,
        out_shape=jax.ShapeDtypeStruct((b, s, GLA_DV), bf16),
        grid=(b,),
        in_specs=[tok(GLA_DK), tok(GLA_DK), tok(GLA_DV), tok(2 * GLA_DK), st_spec, st_spec],
        out_specs=tok(GLA_DV),
        scratch_shapes=[pltpu.VMEM((2 * GLA_HEADS, GLA_HK, GLA_HV), f32),
                        pltpu.VMEM((s, GLA_DV), f32),
                        pltpu.VMEM((2 * GLA_HEADS, GLA_CHUNK, GLA_CHUNK), bf16),
                        pltpu.VMEM((8, GLA_CHUNK, GLA_DK), bf16), pltpu.VMEM((8, GLA_CHUNK, GLA_DK), bf16),
                        pltpu.VMEM((GLA_CHUNK, GLA_HK), f32), pltpu.VMEM((GLA_CHUNK, GLA_HK), f32)],
        compiler_params=pltpu.CompilerParams(dimension_semantics=("arbitrary",),
                                             vmem_limit_bytes=V7X_SCOPED_VMEM_CAP),
        name="gla",
    )(q, k, v, la, s0f, s0b)


def _mix_kernel(x_ref, o_ref, mod_ref, n1_ref, n2_ref, wugg_ref, band_ref, icnt_ref, pwbd_ref, pscale_ref,
                wpool_ref, onorm_ref, wgla_ref, wo_ref, wr_ref, br_ref,
                x1_ref, h2_ref, ri_ref, rw_ref):
    x = x_ref[0]
    tm = x.shape[0]
    m = mod_ref[0]
    hx = _norm_mod(x, n1_ref[...], m[1:2], m[0:1]).astype(bf16)
    p = _dot(hx, wugg_ref[...])
    u = p[:, :POOL_WIDTH]
    g = p[:, POOL_WIDTH:POOL_WIDTH + GLA_DV]
    gates = p[:, POOL_WIDTH + GLA_DV:]

    ub = u.astype(bf16)
    slab = band_ref.shape[1]
    sums = []
    for s in range(tm // slab):
        rs = slice(s * slab, (s + 1) * slab)
        sums.append(jnp.concatenate(
            [_dot(band_ref[gi], ub[rs, gi * POOL_GROUP_DIM:(gi + 1) * POOL_GROUP_DIM])
             for gi in range(len(POOL_WINDOWS))], axis=1))
    pooled = jnp.concatenate(sums, axis=0) * icnt_ref[...] - u
    y_pool = _dot(pooled.astype(bf16), pwbd_ref[...]) * pscale_ref[...]
    y_pool = _dot(y_pool.astype(bf16), wpool_ref[...])

    o = o_ref[0].astype(f32)
    normed = []
    for h in range(GLA_HEADS):
        oh = o[:, h * GLA_HV:(h + 1) * GLA_HV]
        normed.append(oh * lax.rsqrt(jnp.mean(oh * oh, axis=-1, keepdims=True) + EPS))
    on = jnp.concatenate(normed, axis=1) * onorm_ref[...]
    y_gla = _dot((on * _silu(g)).astype(bf16), wgla_ref[...])

    mixed = _sigmoid(gates[:, :D_MODEL]) * y_pool + _sigmoid(gates[:, D_MODEL:]) * y_gla
    x1 = x + m[2:3] * _dot(mixed.astype(bf16), wo_ref[...])
    x1_ref[0] = x1

    h2f = _norm_mod(x1, n2_ref[...], m[4:5], m[3:4])
    _rows_to_tiles(h2_ref, h2f)
    h2 = h2f.astype(bf16)

    lt = (_dot(h2, wr_ref[...]) + br_ref[...]).T
    g0, g1, g2, g3 = lt[0:1], lt[1:2], lt[2:3], lt[3:4]
    gmax = jnp.maximum(jnp.maximum(g0, g1), jnp.maximum(g2, g3))
    gidx = jnp.where(g0 == gmax, 0, jnp.where(g1 == gmax, 1, jnp.where(g2 == gmax, 2, 3))).astype(i32)
    p_grp = 1.0 / (jnp.exp(g0 - gmax) + jnp.exp(g1 - gmax) + jnp.exp(g2 - gmax) + jnp.exp(g3 - gmax))
    e0 = V7X_SUBLANES
    ing = lt[e0 + 3 * EXPERTS_PER_GROUP:e0 + 4 * EXPERTS_PER_GROUP]
    for gi in (2, 1, 0):
        ing = jnp.where(gidx == gi, lt[e0 + gi * EXPERTS_PER_GROUP:e0 + (gi + 1) * EXPERTS_PER_GROUP], ing)
    rid = lax.broadcasted_iota(i32, ing.shape, 0)
    v1 = jnp.max(ing, axis=0, keepdims=True)
    i1 = jnp.min(jnp.where(ing == v1, rid, EXPERTS_PER_GROUP), axis=0, keepdims=True)
    rest = jnp.where(rid == i1, -jnp.inf, ing)
    v2 = jnp.max(rest, axis=0, keepdims=True)
    i2 = jnp.min(jnp.where(rest == v2, rid, EXPERTS_PER_GROUP), axis=0, keepdims=True)
    t = jnp.exp(v2 - v1)
    w1 = p_grp / (1.0 + t)
    w2 = p_grp * t / (1.0 + t)
    first_low = i1 < i2
    ilo = jnp.minimum(i1, i2)
    ihi = jnp.maximum(i1, i2)
    pair = ilo * (EXPERTS_PER_GROUP - 1) - ((ilo * (ilo - 1)) >> 1) + (ihi - ilo - 1)
    pid = gidx * (N_PAIRS // N_GROUPS) + pair
    ri_ref[...] = jnp.where(rid == 0, pid, 0)
    rw_ref[...] = jnp.where(rid == 0, jnp.where(first_low, w1, w2),
                            jnp.where(rid == 1, jnp.where(first_low, w2, w1), 0.0))


def _mix(x, o, mod3, n1, n2, wugg, band, icnt, pwbd, pscale, wpool, onorm, wgla, wo, wr, br):
    b, s, d = x.shape
    tm = TOK_TILE
    nj = s // tm
    t = b * s
    const2 = lambda i, j: (0, 0)
    const3 = lambda i, j: (0, 0, 0)
    tok = lambda w: pl.BlockSpec((1, tm, w), lambda i, j: (i, j, 0))
    lane = pl.BlockSpec((V7X_SUBLANES, tm), lambda i, j: (0, i * nj + j))
    return pl.pallas_call(
        _mix_kernel,
        out_shape=(jax.ShapeDtypeStruct((b, s, d), f32), jax.ShapeDtypeStruct((t * ROW_TILE, V7X_LANES), f32),
                   jax.ShapeDtypeStruct((V7X_SUBLANES, t), i32), jax.ShapeDtypeStruct((V7X_SUBLANES, t), f32)),
        grid=(b, nj),
        in_specs=[tok(d), tok(GLA_DV),
                  pl.BlockSpec((1, MOD_CHUNKS, d), lambda i, j: (i, 0, 0)),
                  pl.BlockSpec((1, d), const2), pl.BlockSpec((1, d), const2),
                  pl.BlockSpec(wugg.shape, const2),
                  pl.BlockSpec(band.shape, const3),
                  pl.BlockSpec(icnt.shape, const2),
                  pl.BlockSpec(pwbd.shape, const2),
                  pl.BlockSpec(pscale.shape, const2),
                  pl.BlockSpec(wpool.shape, const2),
                  pl.BlockSpec(onorm.shape, const2),
                  pl.BlockSpec(wgla.shape, const2),
                  pl.BlockSpec(wo.shape, const2),
                  pl.BlockSpec(wr.shape, const2),
                  pl.BlockSpec(br.shape, const2)],
        out_specs=(tok(d), pl.BlockSpec((tm * ROW_TILE, V7X_LANES), lambda i, j: (i * nj + j, 0)), lane, lane),
        compiler_params=pltpu.CompilerParams(dimension_semantics=("arbitrary", "arbitrary"),
                                             vmem_limit_bytes=V7X_SCOPED_VMEM_CAP),
        name="mix",
    )(x, o, mod3, n1, n2, wugg, band, icnt, pwbd, pscale, wpool, onorm, wgla, wo, wr, br)


def _moe_kernel(ta_ref, tb_ref, g0_ref, xs_ref, xg_ref, pa_ref, pb_ref,
                idx0_ref, idx1_ref, idxn0_ref, idxn1_ref, idxp0_ref, idxp1_ref, idxl_ref,
                wgt0_ref, wgt1_ref, h2_ref, w1_any, w3_any, w2_any,
                w1a0_ref, w3a0_ref, w2a0_ref, w1b0_ref, w3b0_ref, w2b0_ref,
                w1a1_ref, w3a1_ref, w2a1_ref, w1b1_ref, w3b1_ref, w2b1_ref, y_ref,
                xbuf, ybuf, gsem, ssem, xt_sc, yacc, wx1, wx3, wx2, wsem):
    del ta_ref, tb_ref
    j = pl.program_id(0)
    last = pl.num_programs(0) - 1
    tm = xbuf.shape[1] // ROW_TILE

    def token_tile(ref, off):
        return ref.at[pl.ds(pl.multiple_of(off, ROW_TILE), ROW_TILE)]

    def gather_row(off, s, r8):
        return pltpu.make_async_copy(token_tile(h2_ref, off), xbuf.at[s, pl.ds(r8, ROW_TILE)], gsem.at[s])

    def scatter_row(off, s, r8):
        return pltpu.make_async_copy(ybuf.at[s, pl.ds(r8, ROW_TILE)], token_tile(y_ref, off), ssem.at[s])

    def start_rows(make, idx_ref):
        def body(r, carry):
            make(idx_ref[0, 0, r], pl.multiple_of(r * ROW_TILE, ROW_TILE)).start()
            return carry
        lax.fori_loop(0, tm, body, 0, unroll=8)

    def gather_tile(s):
        return pltpu.make_async_copy(h2_ref.at[pl.ds(0, tm * ROW_TILE)], xbuf.at[s], gsem.at[s])

    def scatter_tile(s):
        return pltpu.make_async_copy(ybuf.at[s], y_ref.at[pl.ds(0, tm * ROW_TILE)], ssem.at[s])

    @pl.when(j == 0)
    def _():
        ybuf[...] = jnp.zeros(ybuf.shape, ybuf.dtype)
        start_rows(lambda off, r8: gather_row(off, 0, r8), idx0_ref)
        start_rows(lambda off, r8: gather_row(off, 1, r8), idx1_ref)

    n_pieces = 12
    per_piece = -(-tm // n_pieces)
    half = D_EXPERT // 2

    def pair_pass(xt, w_first, w_second, weights, between):
        y = None
        for w_col, (w1, w3, w2) in zip((w_first, w_second), weights):
            acc = None
            for c in range(2):
                h1 = _dot(xt, w1(c * half, (c + 1) * half))
                between()
                h3 = _dot(xt, w3(c * half, (c + 1) * half))
                between()
                part = _dot((_silu(h1) * h3).astype(bf16), w2(c * half, (c + 1) * half))
                between()
                acc = part if acc is None else acc + part
            y = w_col * acc if y is None else y + w_col * acc
        return y

    def pair_weights(wgt_ref, pair):
        w = wgt_ref[...]
        mine = w[:, 2:3] == pair.astype(f32)
        return jnp.where(mine, w[:, 0:1], 0.0), jnp.where(mine, w[:, 1:2], 0.0)

    def slices(w1, w3, w2, *lead):
        return (lambda c0, c1: w1[(*lead, slice(None), slice(c0, c1))],
                lambda c0, c1: w3[(*lead, slice(None), slice(c0, c1))],
                lambda r0, r1: w2[(*lead, slice(r0, r1), slice(None))])

    def tile_body(s, tile, idxn_ref, idxp_ref, wgt_ref, first_refs, second_refs):
        first_extra = xs_ref[tile]
        n_extra = xs_ref[tile + 1] - first_extra

        def extra_weight_copies(k):
            pair = xg_ref[first_extra + k]
            return [pltpu.make_async_copy(src.at[e], dst.at[slot], wsem.at[n])
                    for n, (src, dst, e, slot) in enumerate(((w1_any, wx1, pa_ref[pair], 0),
                                                             (w3_any, wx3, pa_ref[pair], 0),
                                                             (w2_any, wx2, pa_ref[pair], 0),
                                                             (w1_any, wx1, pb_ref[pair], 1),
                                                             (w3_any, wx3, pb_ref[pair], 1),
                                                             (w2_any, wx2, pb_ref[pair], 1)))]

        @pl.when(n_extra > 0)
        def _():
            for cp in extra_weight_copies(0):
                cp.start()

        gather_tile(s).wait()
        xt = _tiles_to_rows(xbuf.at[s]).astype(bf16)
        xt_sc[...] = xt
        piece = [0]

        def issue_rows():
            lo = piece[0] * per_piece
            rows = range(lo, min(lo + per_piece, tm))
            src_rows = [idxn_ref[0, 0, r] for r in rows]
            dst_rows = [idxp_ref[0, 0, r] for r in rows]
            for r, sr, dr in zip(rows, src_rows, dst_rows):
                gather_row(sr, s, r * ROW_TILE).start(priority=0)
                scatter_row(dr, 1 - s, r * ROW_TILE).start(priority=1)
            piece[0] += 1

        w_first, w_second = pair_weights(wgt_ref, g0_ref[tile])
        yacc[...] = pair_pass(xt, w_first, w_second, (slices(*first_refs, 0), slices(*second_refs, 0)), issue_rows)
        assert piece[0] == n_pieces

        def extra_pass(k, carry):
            for cp in extra_weight_copies(k):
                cp.wait()
            w_lo, w_hi = pair_weights(wgt_ref, xg_ref[first_extra + k])
            yacc[...] += pair_pass(xt_sc[...], w_lo, w_hi, (slices(wx1, wx3, wx2, 0), slices(wx1, wx3, wx2, 1)),
                                   lambda: None)

            @pl.when(k + 1 < n_extra)
            def _():
                for cp in extra_weight_copies(k + 1):
                    cp.start()

            return carry

        lax.fori_loop(0, n_extra, extra_pass, 0)

        @pl.when(tile >= 1)
        def _():
            scatter_tile(s).wait()

        _rows_to_tiles(ybuf.at[s], yacc[...])

    tile_body(0, 2 * j, idxn0_ref, idxp0_ref, wgt0_ref, (w1a0_ref, w3a0_ref, w2a0_ref), (w1b0_ref, w3b0_ref, w2b0_ref))
    tile_body(1, 2 * j + 1, idxn1_ref, idxp1_ref, wgt1_ref, (w1a1_ref, w3a1_ref, w2a1_ref),
              (w1b1_ref, w3b1_ref, w2b1_ref))

    @pl.when(j == last)
    def _():
        scatter_tile(0).wait()
        start_rows(lambda off, r8: scatter_row(off, 1, r8), idxl_ref)
        scatter_tile(1).wait()
        gather_tile(0).wait()
        gather_tile(1).wait()


def _moe(tile_a, tile_b, first_pair, extra_start, extra_pair, pair_a, pair_b, idx, wgt, h2, w1, w3, w2):
    nt, tm = idx.shape[0] - 1, idx.shape[2]
    assert nt % 2 == 0
    rows, lanes = h2.shape
    d = ROW_TILE * lanes
    idx_blk = (1, 1, tm)
    smem_idx = lambda fn: pl.BlockSpec(idx_blk, lambda j, *_: fn(j), memory_space=pltpu.SMEM)
    w13_blk = (1, d, D_EXPERT)
    w2_blk = (1, D_EXPERT, d)

    def weights(s):
        lower = lambda j, ta, tb, *_: (ta[2 * j + s], 0, 0)
        higher = lambda j, ta, tb, *_: (tb[2 * j + s], 0, 0)
        return [pl.BlockSpec(blk, fn) for fn in (lower, higher) for blk in (w13_blk, w13_blk, w2_blk)]

    any_space = pl.BlockSpec(memory_space=pl.ANY)
    grid_spec = pltpu.PrefetchScalarGridSpec(
        num_scalar_prefetch=7,
        grid=(nt // 2,),
        in_specs=[smem_idx(lambda j: (0, 0, 0)),
                  smem_idx(lambda j: (1, 0, 0)),
                  smem_idx(lambda j: (jnp.minimum(2 * j + 2, nt - 1), 0, 0)),
                  smem_idx(lambda j: (jnp.minimum(2 * j + 3, nt - 1), 0, 0)),
                  smem_idx(lambda j: (jnp.where(j == 0, nt, 2 * j - 1), 0, 0)),
                  smem_idx(lambda j: (2 * j, 0, 0)),
                  smem_idx(lambda j: (nt - 1, 0, 0)),
                  pl.BlockSpec((tm, wgt.shape[1]), lambda j, *_: (2 * j, 0)),
                  pl.BlockSpec((tm, wgt.shape[1]), lambda j, *_: (2 * j + 1, 0)),
                  any_space, any_space, any_space, any_space] + weights(0) + weights(1),
        out_specs=any_space,
        scratch_shapes=[pltpu.VMEM((2, tm * ROW_TILE, lanes), f32), pltpu.VMEM((2, tm * ROW_TILE, lanes), f32),
                        pltpu.SemaphoreType.DMA((2,)), pltpu.SemaphoreType.DMA((2,)),
                        pltpu.VMEM((tm, d), bf16), pltpu.VMEM((tm, d), f32),
                        pltpu.VMEM((2, d, D_EXPERT), bf16), pltpu.VMEM((2, d, D_EXPERT), bf16),
                        pltpu.VMEM((2, D_EXPERT, d), bf16), pltpu.SemaphoreType.DMA((6,))],
    )
    return pl.pallas_call(
        _moe_kernel,
        out_shape=jax.ShapeDtypeStruct((rows + tm * ROW_TILE, lanes), f32),
        grid_spec=grid_spec,
        compiler_params=pltpu.CompilerParams(dimension_semantics=("arbitrary",),
                                             vmem_limit_bytes=_vmem_limit(48 * 1024 * 1024)),
        name="moe",
    )(tile_a, tile_b, first_pair, extra_start, extra_pair, pair_a, pair_b,
      idx, idx, idx, idx, idx, idx, idx, wgt, wgt, h2, w1, w3, w2, *([w1, w3, w2] * 4))


def _pair_tables():
    a, b = [], []
    for g in range(N_GROUPS):
        for i in range(EXPERTS_PER_GROUP):
            for j in range(i + 1, EXPERTS_PER_GROUP):
                a.append(g * EXPERTS_PER_GROUP + i)
                b.append(g * EXPERTS_PER_GROUP + j)
    return np.asarray(a, np.int32), np.asarray(b, np.int32)


def _dispatch_plan(pid, wlo, whi, tm):
    t = pid.shape[0]
    nt = t // tm
    assert t % (2 * tm) == 0
    pid_s, tok_s, wlo_s, whi_s = lax.sort((pid, jnp.arange(t, dtype=i32), wlo, whi), num_keys=1, is_stable=True)
    pairs = jnp.arange(N_PAIRS, dtype=i32)
    counts = jnp.sum((pid[:, None] == pairs[None, :]).astype(i32), axis=0)
    starts = jnp.cumsum(counts) - counts
    first_pair = pid_s.reshape(nt, tm)[:, 0]
    is_extra = jnp.logical_and(counts > 0, starts % tm != 0)
    extra_tile = starts // tm
    tile = jnp.arange(nt + 1, dtype=i32)
    extra_start = jnp.sum(jnp.logical_and(is_extra[None, :], extra_tile[None, :] < tile[:, None]).astype(i32), axis=1)
    extra_rank = jnp.cumsum(is_extra.astype(i32)) - is_extra.astype(i32)
    hit = jnp.logical_and(is_extra[None, :], extra_rank[None, :] == pairs[:, None])
    extra_pair = jnp.sum(jnp.where(hit, pairs[None, :], 0), axis=1)
    pa, pb = (jnp.asarray(tab) for tab in _pair_tables())
    is_first = (first_pair[:, None] == pairs[None, :]).astype(i32)
    row = jnp.arange(tm, dtype=i32)[None, :]
    idx = jnp.concatenate([tok_s.reshape(nt, tm), t + row], axis=0).reshape(nt + 1, 1, tm) * ROW_TILE
    wgt = jnp.stack([wlo_s, whi_s, pid_s.astype(f32), jnp.zeros_like(wlo_s)], axis=-1)
    return (jnp.sum(is_first * pa[None, :], axis=1), jnp.sum(is_first * pb[None, :], axis=1), first_pair,
            extra_start, extra_pair, pa, pb, idx, wgt)


def _final_kernel(x1_ref, y_ref, mod_ref, g_ref, out_ref):
    m = mod_ref[0]
    x2 = x1_ref[0] + m[5:6] * _tiles_to_rows(y_ref)
    ms = jnp.mean(x2 * x2, axis=-1, keepdims=True)
    out_ref[0] = x2 * lax.rsqrt(ms + EPS) * g_ref[...]


def _final(x1, y, mod3, g):
    b, s, d = x1.shape
    tm = TOK_TILE
    nj = s // tm
    tok = pl.BlockSpec((1, tm, d), lambda i, j: (i, j, 0))
    return pl.pallas_call(
        _final_kernel,
        out_shape=jax.ShapeDtypeStruct((b, s, d), f32),
        grid=(b, nj),
        in_specs=[tok,
                  pl.BlockSpec((tm * ROW_TILE, y.shape[1]), lambda i, j: (i * nj + j, 0)),
                  pl.BlockSpec((1, MOD_CHUNKS, d), lambda i, j: (i, 0, 0)),
                  pl.BlockSpec((1, d), lambda i, j: (0, 0))],
        out_specs=tok,
        compiler_params=pltpu.CompilerParams(dimension_semantics=("arbitrary", "arbitrary"),
                                             vmem_limit_bytes=_vmem_limit(32 * 1024 * 1024)),
        name="final",
    )(x1, y, mod3, g)


def _pool_constants(tm):
    slab = V7X_MXU_DIM
    band = np.zeros((len(POOL_WINDOWS), slab, slab), np.float32)
    icnt = np.zeros((tm, POOL_WIDTH), np.float32)
    for gi, w in enumerate(POOL_WINDOWS):
        for t in range(slab):
            base, pos = (t // GRID_W) * GRID_W, t % GRID_W
            lo, hi = max(pos - w // 2, 0), min(pos + w // 2, GRID_W)
            band[gi, t, base + lo:base + hi] = 1.0
        for t in range(tm):
            pos = t % GRID_W
            cnt = min(pos + w // 2, GRID_W) - max(pos - w // 2, 0)
            icnt[t, gi * POOL_GROUP_DIM:(gi + 1) * POOL_GROUP_DIM] = 1.0 / cnt
    return jnp.asarray(band, bf16), jnp.asarray(icnt, f32)


def _block_diag(blocks):
    n = len(blocks)
    r, c = blocks[0].shape
    out = jnp.zeros((n * r, n * c), blocks[0].dtype)
    for i, blk in enumerate(blocks):
        out = out.at[i * r:(i + 1) * r, i * c:(i + 1) * c].set(blk)
    return out


def kernel(x, c, ctx, c_ctx, w_mod, b_mod, norm1_g, norm2_g, w_in, gla_a2_f, gla_ab_f, gla_a2_b, gla_ab_b,
           gla_onorm_g, pool_w, pool_scale, w_pool_br, w_gla_br, w_o, router_grp_w, router_grp_b,
           router_exp_w, router_exp_b, moe_w1, moe_w3, moe_w2, final_norm_g):
    b, s, d = x.shape
    depth = w_mod.shape[0]
    assert depth == 1 and d == D_MODEL and s % TOK_TILE == 0 and s % (4 * GLA_CHUNK) == 0
    assert TOK_TILE % V7X_MXU_DIM == 0 and V7X_MXU_DIM % GRID_W == 0
    t = b * s

    cond = jnp.concatenate([c, c_ctx[None, :], jnp.zeros((COND_PAD - 1, d), f32)], axis=0)
    mod3 = _adaln(cond, w_mod[0], b_mod[0][None, :]).reshape(b + COND_PAD, MOD_CHUNKS, d)

    w_in_b = w_in[0].astype(bf16)
    wqkv = w_in_b[:, _Q0:_G0]
    wkv = w_in_b[:, _K0:_G0]
    wugg = jnp.concatenate([w_in_b[:, _U0:_Q0], w_in_b[:, _G0:_LR0]], axis=1)
    wlr = jnp.pad(w_in_b[:, _LR0:_IN_END], ((0, 0), (0, V7X_LANES - 2 * GLA_GATE_RANK)))
    a2bd = jnp.pad(_block_diag([gla_a2_f[0], gla_a2_b[0]]).astype(bf16),
                   ((0, V7X_LANES - 2 * GLA_GATE_RANK), (0, 0)))
    ab = jnp.concatenate([gla_ab_f[0], gla_ab_b[0]])[None, :]
    n1 = norm1_g[0][None, :]
    n2 = norm2_g[0][None, :]

    s0f, s0b = _ctx_states(ctx, mod3, b, n1, wkv, wlr, a2bd, ab)
    q, k, v, la = _qkv(x, mod3, n1, wqkv, wlr, a2bd, ab)
    o = _gla(q, k, v, la, s0f, s0b)

    band, icnt = _pool_constants(TOK_TILE)
    pwbd = _block_diag([pool_w[0, gi] for gi in range(len(POOL_WINDOWS))]).astype(bf16)
    onorm = jnp.tile(gla_onorm_g[0], GLA_HEADS)[None, :]
    wr = jnp.zeros((d, V7X_LANES), f32)
    wr = wr.at[:, :N_GROUPS].set(router_grp_w[0]).at[:, V7X_SUBLANES:V7X_SUBLANES + N_EXPERTS].set(router_exp_w[0])
    br = jnp.zeros((1, V7X_LANES), f32)
    br = br.at[0, :N_GROUPS].set(router_grp_b[0]).at[0, V7X_SUBLANES:V7X_SUBLANES + N_EXPERTS].set(router_exp_b[0])
    x1, h2, ri, rw = _mix(x, o, mod3, n1, n2, wugg, band, icnt, pwbd, pool_scale[0][None, :],
                           w_pool_br[0].astype(bf16), onorm, w_gla_br[0].astype(bf16), w_o[0].astype(bf16),
                           wr.astype(bf16), br)

    plan = _dispatch_plan(ri[0], rw[0], rw[1], MOE_TILE)
    y = _moe(*plan, h2, moe_w1[0].astype(bf16), moe_w3[0].astype(bf16), moe_w2[0].astype(bf16))
    return _final(x1, y, mod3, final_norm_g[None, :])
```

```python
import jax
import jax.numpy as jnp
import numpy as np
from jax import lax
from jax.experimental import pallas as pl
from jax.experimental.pallas import tpu as pltpu

f32 = jnp.float32
bf16 = jnp.bfloat16
i32 = jnp.int32

D_MODEL = 1024
GRID_W = 64
POOL_WINDOWS = (2, 4, 8, 16)
POOL_GROUP_DIM = 128
POOL_WIDTH = len(POOL_WINDOWS) * POOL_GROUP_DIM
GLA_HEADS = 4
GLA_DK = 512
GLA_DV = 1024
GLA_HK = GLA_DK // GLA_HEADS
GLA_HV = GLA_DV // GLA_HEADS
GLA_GATE_RANK = 16
GLA_GATE_NORMALIZER = 16.0
N_GROUPS = 4
EXPERTS_PER_GROUP = 8
N_EXPERTS = N_GROUPS * EXPERTS_PER_GROUP
D_EXPERT = 512
MOD_CHUNKS = 6
EPS = 1e-6

_U0, _Q0, _K0, _G0, _LR0, _IN_END = 0, 512, 1024, 2560, 5632, 5664

V7X_LANES = 128
V7X_SUBLANES = 8
V7X_MXU_DIM = 256
V7X_SCOPED_VMEM_CAP = 60000 * 1024

GLA_CHUNK = 128
GLA_HALF = GLA_CHUNK // 2
TOK_TILE = 512
FINAL_TILE = 1024
MOE_TILE = 256
COND_PAD = 8
N_PAIRS = N_GROUPS * (EXPERTS_PER_GROUP * (EXPERTS_PER_GROUP - 1) // 2)


def _vmem_limit(nbytes):
    return int(min(max(nbytes, 16 * 1024 * 1024), V7X_SCOPED_VMEM_CAP))


def _dot(a, b):
    return jnp.dot(a, b, preferred_element_type=f32)


def _dot_nt(a, b):
    return lax.dot_general(a, b, (((1,), (1,)), ((), ())), preferred_element_type=f32)


def _dot_tn(a, b):
    return lax.dot_general(a, b, (((0,), (0,)), ((), ())), preferred_element_type=f32)


def _sigmoid(x):
    return 1.0 / (1.0 + jnp.exp(-x))


def _silu(x):
    return x * _sigmoid(x)


def _log_sigmoid(z):
    return jnp.minimum(z, 0.0) - jnp.log1p(jnp.exp(-jnp.abs(z)))


def _norm_mod(x, g, scale, shift):
    ms = jnp.mean(x * x, axis=-1, keepdims=True)
    return (x * lax.rsqrt(ms + EPS) * g) * (1.0 + scale) + shift


def _tri_cumsum(tri, la):
    hi = la.astype(bf16)
    lo = (la - hi.astype(f32)).astype(bf16)
    return _dot(tri, hi) + _dot(tri, lo)


ROW_TILE = V7X_SUBLANES


def _rows_to_tiles(ref, val):
    n = val.shape[0]
    for a in range(ROW_TILE):
        ref[pl.ds(a, n, stride=ROW_TILE), :] = val[:, a * V7X_LANES:(a + 1) * V7X_LANES]


def _tiles_to_rows(ref):
    n = ref.shape[0] // ROW_TILE
    return jnp.concatenate([ref[pl.ds(a, n, stride=ROW_TILE), :] for a in range(ROW_TILE)], axis=1)


def _log_decays(lr, a2_ref, ab_ref):
    z = _dot(lr.astype(bf16), a2_ref[...]) + ab_ref[...]
    return _log_sigmoid(z) * (1.0 / GLA_GATE_NORMALIZER)


def _adaln_kernel(c_ref, w_ref, b_ref, o_ref):
    c = c_ref[...]
    o_ref[...] = _dot(_silu(c).astype(bf16), w_ref[...].astype(bf16)) + b_ref[...]


def _adaln(cond, w_mod, b_mod):
    rows, d = cond.shape
    n = w_mod.shape[1]
    tn = 512
    return pl.pallas_call(
        _adaln_kernel,
        out_shape=jax.ShapeDtypeStruct((rows, n), f32),
        grid=(n // tn,),
        in_specs=[pl.BlockSpec((rows, d), lambda j: (0, 0)),
                  pl.BlockSpec((d, tn), lambda j: (0, j)),
                  pl.BlockSpec((1, tn), lambda j: (0, j))],
        out_specs=pl.BlockSpec((rows, tn), lambda j: (0, j)),
        compiler_params=pltpu.CompilerParams(dimension_semantics=("arbitrary",)),
        name="adaln",
    )(cond, w_mod, b_mod)


def _ctx_kernel(ctx_ref, mod_ref, g_ref, wkv_ref, wlr_ref, a2_ref, ab_ref, sf_ref, sb_ref):
    x = ctx_ref[0]
    n = x.shape[0]
    m = mod_ref[0]
    hc = _norm_mod(x, g_ref[...], m[1:2], m[0:1]).astype(bf16)
    kv = _dot(hc, wkv_ref[...])
    la = _log_decays(_dot(hc, wlr_ref[...]), a2_ref, ab_ref)
    row = lax.broadcasted_iota(i32, (n, n), 0)
    col = lax.broadcasted_iota(i32, (n, n), 1)
    lower = jnp.where(row >= col, 1.0, 0.0).astype(bf16)
    upper = jnp.where(col >= row, 1.0, 0.0).astype(bf16)
    cum_f = _tri_cumsum(lower, la[:, :GLA_DK])
    cum_b = _tri_cumsum(upper, la[:, GLA_DK:])
    k = kv[:, :GLA_DK]
    v = kv[:, GLA_DK:].astype(bf16)
    kr_f = (k * jnp.exp(cum_f[n - 1:n] - cum_f)).astype(bf16)
    kr_b = (k * jnp.exp(cum_b[0:1] - cum_b)).astype(bf16)
    for h in range(GLA_HEADS):
        ks = slice(h * GLA_HK, (h + 1) * GLA_HK)
        vs = slice(h * GLA_HV, (h + 1) * GLA_HV)
        sf_ref[0, h] = _dot_tn(kr_f[:, ks], v[:, vs])
        sb_ref[0, h] = _dot_tn(kr_b[:, ks], v[:, vs])


def _ctx_states(ctx, mod3, ctx_row, g, wkv, wlr, a2bd, ab):
    b, n, d = ctx.shape
    st = jax.ShapeDtypeStruct((b, GLA_HEADS, GLA_HK, GLA_HV), f32)
    const = lambda i: (0, 0)
    st_spec = pl.BlockSpec((1, GLA_HEADS, GLA_HK, GLA_HV), lambda i: (i, 0, 0, 0))
    return pl.pallas_call(
        _ctx_kernel,
        out_shape=(st, st),
        grid=(b,),
        in_specs=[pl.BlockSpec((1, n, d), lambda i: (i, 0, 0)),
                  pl.BlockSpec((1, MOD_CHUNKS, d), lambda i: (ctx_row, 0, 0)),
                  pl.BlockSpec((1, d), const),
                  pl.BlockSpec(wkv.shape, const),
                  pl.BlockSpec(wlr.shape, const),
                  pl.BlockSpec(a2bd.shape, const),
                  pl.BlockSpec(ab.shape, const)],
        out_specs=(st_spec, st_spec),
        compiler_params=pltpu.CompilerParams(dimension_semantics=("arbitrary",),
                                             vmem_limit_bytes=_vmem_limit(32 * 1024 * 1024)),
        name="ctx_states",
    )(ctx, mod3, g, wkv, wlr, a2bd, ab)


def _qkv_kernel(x_ref, mod_ref, g_ref, w_ref, a2_ref, ab_ref, q_ref, k_ref, v_ref, la_ref):
    m = mod_ref[0]
    hx = _norm_mod(x_ref[0], g_ref[...], m[1:2], m[0:1]).astype(bf16)
    p = _dot(hx, w_ref[...])
    q_ref[0] = p[:, :GLA_DK].astype(bf16)
    k_ref[0] = p[:, GLA_DK:2 * GLA_DK].astype(bf16)
    v_ref[0] = p[:, 2 * GLA_DK:2 * GLA_DK + GLA_DV].astype(bf16)
    la_ref[0] = _log_decays(p[:, 2 * GLA_DK + GLA_DV:], a2_ref, ab_ref)


def _qkv(x, mod3, g, wqkv, a2bd, ab):
    b, s, d = x.shape
    tm = TOK_TILE
    const = lambda i, j: (0, 0)
    tok = lambda w: pl.BlockSpec((1, tm, w), lambda i, j: (i, j, 0))
    return pl.pallas_call(
        _qkv_kernel,
        out_shape=(jax.ShapeDtypeStruct((b, s, GLA_DK), bf16), jax.ShapeDtypeStruct((b, s, GLA_DK), bf16),
                   jax.ShapeDtypeStruct((b, s, GLA_DV), bf16), jax.ShapeDtypeStruct((b, s, 2 * GLA_DK), f32)),
        grid=(b, s // tm),
        in_specs=[tok(d),
                  pl.BlockSpec((1, MOD_CHUNKS, d), lambda i, j: (i, 0, 0)),
                  pl.BlockSpec((1, d), const),
                  pl.BlockSpec(wqkv.shape, const),
                  pl.BlockSpec(a2bd.shape, const),
                  pl.BlockSpec(ab.shape, const)],
        out_specs=(tok(GLA_DK), tok(GLA_DK), tok(GLA_DV), tok(2 * GLA_DK)),
        compiler_params=pltpu.CompilerParams(dimension_semantics=("arbitrary", "arbitrary"),
                                             vmem_limit_bytes=_vmem_limit(48 * 1024 * 1024)),
        name="qkv",
    )(x, mod3, g, wqkv, a2bd, ab)


def _gla_kernel(q_ref, k_ref, v_ref, la_ref, s0f_ref, s0b_ref, o_ref, s_sc, oacc, a_sc, ops_a, ops_b, dec_a, dec_b):
    seq = q_ref.shape[1]
    nc = seq // GLA_CHUNK
    c_len = GLA_CHUNK
    s_sc[0:GLA_HEADS] = s0f_ref[0]
    s_sc[GLA_HEADS:2 * GLA_HEADS] = s0b_ref[0]
    row = lax.broadcasted_iota(i32, (c_len, c_len), 0)
    col = lax.broadcasted_iota(i32, (c_len, c_len), 1)
    lower = row >= col
    upper = col >= row
    tri_l = jnp.where(lower, 1.0, 0.0).astype(bf16)
    tri_u = jnp.where(upper, 1.0, 0.0).astype(bf16)
    scale = GLA_HK ** -0.5

    n_hd = 2 * GLA_HEADS
    QD, QS, KI, KR = range(4)

    def chunk_rows(c):
        return pl.ds(pl.multiple_of(c * c_len, c_len), c_len)

    def chunks_of(step):
        return jnp.minimum(step, nc - 1), jnp.maximum(nc - 1 - step, 0)

    def prepare(step, ops, dec):
        tots = []
        for d, c, tri, r_row, t_row in zip((0, 1), chunks_of(step), (tri_l, tri_u), (GLA_HALF - 1, GLA_HALF),
                                           (c_len - 1, 0)):
            rows = chunk_rows(c)
            cum = _tri_cumsum(tri, la_ref[0, rows, d * GLA_DK:(d + 1) * GLA_DK])
            r = cum[r_row:r_row + 1]
            tot = cum[t_row:t_row + 1]
            qd = q_ref[0, rows, :].astype(f32) * (jnp.exp(cum - r) * scale)
            ki = k_ref[0, rows, :].astype(f32) * jnp.exp(r - cum)
            ops[4 * d + QD] = qd.astype(bf16)
            ops[4 * d + QS] = (qd * jnp.exp(r)).astype(bf16)
            ops[4 * d + KI] = ki.astype(bf16)
            ops[4 * d + KR] = (ki * jnp.exp(tot - r)).astype(bf16)
            tots.extend(tot[:, h * GLA_HK:(h + 1) * GLA_HK] for h in range(GLA_HEADS))
        pad = jnp.zeros((c_len - n_hd, GLA_HK), f32)
        dec[...] = jnp.exp(jnp.concatenate(tots + [pad], axis=0).T)

    def matmuls(step, ops, dec, second_visit):
        rows = [chunk_rows(c) for c in chunks_of(step)]
        for d, mask in ((0, lower), (1, upper)):
            for h in range(GLA_HEADS):
                ks = slice(h * GLA_HK, (h + 1) * GLA_HK)
                a = _dot_nt(ops[4 * d + QD, :, ks], ops[4 * d + KI, :, ks])
                a_sc[d * GLA_HEADS + h] = jnp.where(mask, a, 0.0).astype(bf16)
        for d in (0, 1):
            for h in range(GLA_HEADS):
                j = d * GLA_HEADS + h
                ks = slice(h * GLA_HK, (h + 1) * GLA_HK)
                vs = slice(h * GLA_HV, (h + 1) * GLA_HV)
                o = _dot(jnp.concatenate([a_sc[j], ops[4 * d + QS, :, ks]], axis=1),
                         jnp.concatenate([v_ref[0, rows[d], vs], s_sc[j].astype(bf16)], axis=0))
                if second_visit:
                    o_ref[0, rows[d], vs] = (oacc[rows[d], vs] + o).astype(o_ref.dtype)
                else:
                    oacc[rows[d], vs] = o
        for d in (0, 1):
            for h in range(GLA_HEADS):
                j = d * GLA_HEADS + h
                ks = slice(h * GLA_HK, (h + 1) * GLA_HK)
                vs = slice(h * GLA_HV, (h + 1) * GLA_HV)
                s_sc[j] = dec[:, j:j + 1] * s_sc[j] + _dot_tn(ops[4 * d + KR, :, ks], v_ref[0, rows[d], vs])

    def step_pair(second_visit):
        def body(m, carry):
            prepare(2 * m + 1, ops_b, dec_b)
            matmuls(2 * m, ops_a, dec_a, second_visit)
            prepare(2 * m + 2, ops_a, dec_a)
            matmuls(2 * m + 1, ops_b, dec_b, second_visit)
            return carry
        return body

    prepare(0, ops_a, dec_a)
    lax.fori_loop(0, nc // 4, step_pair(False), 0)
    lax.fori_loop(nc // 4, nc // 2, step_pair(True), 0)


def _gla(q, k, v, la, s0f, s0b):
    b, s, _ = q.shape
    tok = lambda w: pl.BlockSpec((1, s, w), lambda i: (i, 0, 0))
    st_spec = pl.BlockSpec((1, GLA_HEADS, GLA_HK, GLA_HV), lambda i: (i, 0, 0, 0))
    return pl.pallas_call(
        _gla_kernel,
        out_shape=jax.ShapeDtypeStruct((b, s, GLA_DV), bf16),
        grid=(b,),
        in_specs=[tok(GLA_DK), tok(GLA_DK), tok(GLA_DV), tok(2 * GLA_DK), st_spec, st_spec],
        out_specs=tok(GLA_DV),
        scratch_shapes=[pltpu.VMEM((2 * GLA_HEADS, GLA_HK, GLA_HV), f32),
                        pltpu.VMEM((s, GLA_DV), f32),
                        pltpu.VMEM((2 * GLA_HEADS, GLA_CHUNK, GLA_CHUNK), bf16),
                        pltpu.VMEM((8, GLA_CHUNK, GLA_DK), bf16), pltpu.VMEM((8, GLA_CHUNK, GLA_DK), bf16),
                        pltpu.VMEM((GLA_CHUNK, GLA_HK), f32), pltpu.VMEM((GLA_CHUNK, GLA_HK), f32)],
        compiler_params=pltpu.CompilerParams(dimension_semantics=("arbitrary",),
                                             vmem_limit_bytes=V7X_SCOPED_VMEM_CAP),
        name="gla",
    )(q, k, v, la, s0f, s0b)


def _mix_kernel(x_ref, o_ref, mod_ref, n1_ref, n2_ref, wugg_ref, band_ref, icnt_ref, pwbd_ref, pscale_ref,
                wpool_ref, onorm_ref, wgla_ref, wo_ref, wr_ref, br_ref,
                x1_ref, h2_ref, ri_ref, rw_ref):
    x = x_ref[0]
    tm = x.shape[0]
    m = mod_ref[0]
    hx = _norm_mod(x, n1_ref[...], m[1:2], m[0:1]).astype(bf16)
    p = _dot(hx, wugg_ref[...])
    u = p[:, :POOL_WIDTH]
    g = p[:, POOL_WIDTH:POOL_WIDTH + GLA_DV]
    gates = p[:, POOL_WIDTH + GLA_DV:]

    ub = u.astype(bf16)
    slab = band_ref.shape[1]
    sums = []
    for s in range(tm // slab):
        rs = slice(s * slab, (s + 1) * slab)
        sums.append(jnp.concatenate(
            [_dot(band_ref[gi], ub[rs, gi * POOL_GROUP_DIM:(gi + 1) * POOL_GROUP_DIM])
             for gi in range(len(POOL_WINDOWS))], axis=1))
    pooled = jnp.concatenate(sums, axis=0) * icnt_ref[...] - u
    y_pool = _dot(pooled.astype(bf16), pwbd_ref[...]) * pscale_ref[...]
    y_pool = _dot(y_pool.astype(bf16), wpool_ref[...])

    o = o_ref[0].astype(f32)
    normed = []
    for h in range(GLA_HEADS):
        oh = o[:, h * GLA_HV:(h + 1) * GLA_HV]
        normed.append(oh * lax.rsqrt(jnp.mean(oh * oh, axis=-1, keepdims=True) + EPS))
    on = jnp.concatenate(normed, axis=1) * onorm_ref[...]
    y_gla = _dot((on * _silu(g)).astype(bf16), wgla_ref[...])

    mixed = _sigmoid(gates[:, :D_MODEL]) * y_pool + _sigmoid(gates[:, D_MODEL:]) * y_gla
    x1 = x + m[2:3] * _dot(mixed.astype(bf16), wo_ref[...])
    x1_ref[0] = x1

    h2f = _norm_mod(x1, n2_ref[...], m[4:5], m[3:4])
    _rows_to_tiles(h2_ref, h2f)
    h2 = h2f.astype(bf16)

    lt = (_dot(h2, wr_ref[...]) + br_ref[...]).T
    g0, g1, g2, g3 = lt[0:1], lt[1:2], lt[2:3], lt[3:4]
    gmax = jnp.maximum(jnp.maximum(g0, g1), jnp.maximum(g2, g3))
    gidx = jnp.where(g0 == gmax, 0, jnp.where(g1 == gmax, 1, jnp.where(g2 == gmax, 2, 3))).astype(i32)
    p_grp = 1.0 / (jnp.exp(g0 - gmax) + jnp.exp(g1 - gmax) + jnp.exp(g2 - gmax) + jnp.exp(g3 - gmax))
    e0 = V7X_SUBLANES
    ing = lt[e0 + 3 * EXPERTS_PER_GROUP:e0 + 4 * EXPERTS_PER_GROUP]
    for gi in (2, 1, 0):
        ing = jnp.where(gidx == gi, lt[e0 + gi * EXPERTS_PER_GROUP:e0 + (gi + 1) * EXPERTS_PER_GROUP], ing)
    rid = lax.broadcasted_iota(i32, ing.shape, 0)
    v1 = jnp.max(ing, axis=0, keepdims=True)
    i1 = jnp.min(jnp.where(ing == v1, rid, EXPERTS_PER_GROUP), axis=0, keepdims=True)
    rest = jnp.where(rid == i1, -jnp.inf, ing)
    v2 = jnp.max(rest, axis=0, keepdims=True)
    i2 = jnp.min(jnp.where(rest == v2, rid, EXPERTS_PER_GROUP), axis=0, keepdims=True)
    t = jnp.exp(v2 - v1)
    w1 = p_grp / (1.0 + t)
    w2 = p_grp * t / (1.0 + t)
    first_low = i1 < i2
    ilo = jnp.minimum(i1, i2)
    ihi = jnp.maximum(i1, i2)
    pair = ilo * (EXPERTS_PER_GROUP - 1) - ((ilo * (ilo - 1)) >> 1) + (ihi - ilo - 1)
    pid = gidx * (N_PAIRS // N_GROUPS) + pair
    ri_ref[...] = jnp.where(rid == 0, pid, 0)
    rw_ref[...] = jnp.where(rid == 0, jnp.where(first_low, w1, w2),
                            jnp.where(rid == 1, jnp.where(first_low, w2, w1), 0.0))


def _mix(x, o, mod3, n1, n2, wugg, band, icnt, pwbd, pscale, wpool, onorm, wgla, wo, wr, br):
    b, s, d = x.shape
    tm = TOK_TILE
    nj = s // tm
    t = b * s
    const2 = lambda i, j: (0, 0)
    const3 = lambda i, j: (0, 0, 0)
    tok = lambda w: pl.BlockSpec((1, tm, w), lambda i, j: (i, j, 0))
    lane = pl.BlockSpec((V7X_SUBLANES, tm), lambda i, j: (0, i * nj + j))
    return pl.pallas_call(
        _mix_kernel,
        out_shape=(jax.ShapeDtypeStruct((b, s, d), f32), jax.ShapeDtypeStruct((t * ROW_TILE, V7X_LANES), f32),
                   jax.ShapeDtypeStruct((V7X_SUBLANES, t), i32), jax.ShapeDtypeStruct((V7X_SUBLANES, t), f32)),
        grid=(b, nj),
        in_specs=[tok(d), tok(GLA_DV),
                  pl.BlockSpec((1, MOD_CHUNKS, d), lambda i, j: (i, 0, 0)),
                  pl.BlockSpec((1, d), const2), pl.BlockSpec((1, d), const2),
                  pl.BlockSpec(wugg.shape, const2),
                  pl.BlockSpec(band.shape, const3),
                  pl.BlockSpec(icnt.shape, const2),
                  pl.BlockSpec(pwbd.shape, const2),
                  pl.BlockSpec(pscale.shape, const2),
                  pl.BlockSpec(wpool.shape, const2),
                  pl.BlockSpec(onorm.shape, const2),
                  pl.BlockSpec(wgla.shape, const2),
                  pl.BlockSpec(wo.shape, const2),
                  pl.BlockSpec(wr.shape, const2),
                  pl.BlockSpec(br.shape, const2)],
        out_specs=(tok(d), pl.BlockSpec((tm * ROW_TILE, V7X_LANES), lambda i, j: (i * nj + j, 0)), lane, lane),
        compiler_params=pltpu.CompilerParams(dimension_semantics=("arbitrary", "arbitrary"),
                                             vmem_limit_bytes=V7X_SCOPED_VMEM_CAP),
        name="mix",
    )(x, o, mod3, n1, n2, wugg, band, icnt, pwbd, pscale, wpool, onorm, wgla, wo, wr, br)


def _moe_kernel(ta_ref, tb_ref, g0_ref, xs_ref, xg_ref, pa_ref, pb_ref,
                idx0_ref, idx1_ref, idxn0_ref, idxn1_ref, idxp0_ref, idxp1_ref, idxl_ref,
                wgt0_ref, wgt1_ref, h2_ref, w1_any, w3_any, w2_any,
                w1a0_ref, w3a0_ref, w2a0_ref, w1b0_ref, w3b0_ref, w2b0_ref,
                w1a1_ref, w3a1_ref, w2a1_ref, w1b1_ref, w3b1_ref, w2b1_ref, y_ref,
                xbuf, ybuf, gsem, ssem, xt_sc, yacc, wx1, wx3, wx2, wsem):
    del ta_ref, tb_ref
    j = pl.program_id(0)
    last = pl.num_programs(0) - 1
    tm = xbuf.shape[1] // ROW_TILE

    def token_tile(ref, off):
        return ref.at[pl.ds(pl.multiple_of(off, ROW_TILE), ROW_TILE)]

    def gather_row(off, s, r8):
        return pltpu.make_async_copy(token_tile(h2_ref, off), xbuf.at[s, pl.ds(r8, ROW_TILE)], gsem.at[s])

    def scatter_row(off, s, r8):
        return pltpu.make_async_copy(ybuf.at[s, pl.ds(r8, ROW_TILE)], token_tile(y_ref, off), ssem.at[s])

    def start_rows(make, idx_ref):
        def body(r, carry):
            make(idx_ref[0, 0, r], pl.multiple_of(r * ROW_TILE, ROW_TILE)).start()
            return carry
        lax.fori_loop(0, tm, body, 0, unroll=8)

    def gather_tile(s):
        return pltpu.make_async_copy(h2_ref.at[pl.ds(0, tm * ROW_TILE)], xbuf.at[s], gsem.at[s])

    def scatter_tile(s):
        return pltpu.make_async_copy(ybuf.at[s], y_ref.at[pl.ds(0, tm * ROW_TILE)], ssem.at[s])

    @pl.when(j == 0)
    def _():
        ybuf[...] = jnp.zeros(ybuf.shape, ybuf.dtype)
        start_rows(lambda off, r8: gather_row(off, 0, r8), idx0_ref)
        start_rows(lambda off, r8: gather_row(off, 1, r8), idx1_ref)

    n_pieces = 12
    per_piece = -(-tm // n_pieces)
    half = D_EXPERT // 2

    def pair_pass(xt, w_first, w_second, weights, between):
        y = None
        for w_col, (w1, w3, w2) in zip((w_first, w_second), weights):
            acc = None
            for c in range(2):
                h1 = _dot(xt, w1(c * half, (c + 1) * half))
                between()
                h3 = _dot(xt, w3(c * half, (c + 1) * half))
                between()
                part = _dot((_silu(h1) * h3).astype(bf16), w2(c * half, (c + 1) * half))
                between()
                acc = part if acc is None else acc + part
            y = w_col * acc if y is None else y + w_col * acc
        return y

    def pair_weights(wgt_ref, pair):
        w = wgt_ref[...]
        mine = w[:, 2:3] == pair.astype(f32)
        return jnp.where(mine, w[:, 0:1], 0.0), jnp.where(mine, w[:, 1:2], 0.0)

    def slices(w1, w3, w2, *lead):
        return (lambda c0, c1: w1[(*lead, slice(None), slice(c0, c1))],
                lambda c0, c1: w3[(*lead, slice(None), slice(c0, c1))],
                lambda r0, r1: w2[(*lead, slice(r0, r1), slice(None))])

    def tile_body(s, tile, idxn_ref, idxp_ref, wgt_ref, first_refs, second_refs):
        first_extra = xs_ref[tile]
        n_extra = xs_ref[tile + 1] - first_extra

        def extra_weight_copies(k):
            pair = xg_ref[first_extra + k]
            return [pltpu.make_async_copy(src.at[e], dst.at[slot], wsem.at[n])
                    for n, (src, dst, e, slot) in enumerate(((w1_any, wx1, pa_ref[pair], 0),
                                                             (w3_any, wx3, pa_ref[pair], 0),
                                                             (w2_any, wx2, pa_ref[pair], 0),
                                                             (w1_any, wx1, pb_ref[pair], 1),
                                                             (w3_any, wx3, pb_ref[pair], 1),
                                                             (w2_any, wx2, pb_ref[pair], 1)))]

        @pl.when(n_extra > 0)
        def _():
            for cp in extra_weight_copies(0):
                cp.start()

        gather_tile(s).wait()
        xt = _tiles_to_rows(xbuf.at[s]).astype(bf16)
        xt_sc[...] = xt
        piece = [0]

        def issue_rows():
            lo = piece[0] * per_piece
            rows = range(lo, min(lo + per_piece, tm))
            src_rows = [idxn_ref[0, 0, r] for r in rows]
            dst_rows = [idxp_ref[0, 0, r] for r in rows]
            for r, sr, dr in zip(rows, src_rows, dst_rows):
                gather_row(sr, s, r * ROW_TILE).start(priority=0)
                scatter_row(dr, 1 - s, r * ROW_TILE).start(priority=1)
            piece[0] += 1

        w_first, w_second = pair_weights(wgt_ref, g0_ref[tile])
        yacc[...] = pair_pass(xt, w_first, w_second, (slices(*first_refs, 0), slices(*second_refs, 0)), issue_rows)
        assert piece[0] == n_pieces

        def extra_pass(k, carry):
            for cp in extra_weight_copies(k):
                cp.wait()
            w_lo, w_hi = pair_weights(wgt_ref, xg_ref[first_extra + k])
            yacc[...] += pair_pass(xt_sc[...], w_lo, w_hi, (slices(wx1, wx3, wx2, 0), slices(wx1, wx3, wx2, 1)),
                                   lambda: None)

            @pl.when(k + 1 < n_extra)
            def _():
                for cp in extra_weight_copies(k + 1):
                    cp.start()

            return carry

        lax.fori_loop(0, n_extra, extra_pass, 0)

        @pl.when(tile >= 1)
        def _():
            scatter_tile(s).wait()

        _rows_to_tiles(ybuf.at[s], yacc[...])

    tile_body(0, 2 * j, idxn0_ref, idxp0_ref, wgt0_ref, (w1a0_ref, w3a0_ref, w2a0_ref), (w1b0_ref, w3b0_ref, w2b0_ref))
    tile_body(1, 2 * j + 1, idxn1_ref, idxp1_ref, wgt1_ref, (w1a1_ref, w3a1_ref, w2a1_ref),
              (w1b1_ref, w3b1_ref, w2b1_ref))

    @pl.when(j == last)
    def _():
        scatter_tile(0).wait()
        start_rows(lambda off, r8: scatter_row(off, 1, r8), idxl_ref)
        scatter_tile(1).wait()
        gather_tile(0).wait()
        gather_tile(1).wait()


def _moe(tile_a, tile_b, first_pair, extra_start, extra_pair, pair_a, pair_b, idx, wgt, h2, w1, w3, w2):
    nt, tm = idx.shape[0] - 1, idx.shape[2]
    assert nt % 2 == 0
    rows, lanes = h2.shape
    d = ROW_TILE * lanes
    idx_blk = (1, 1, tm)
    smem_idx = lambda fn: pl.BlockSpec(idx_blk, lambda j, *_: fn(j), memory_space=pltpu.SMEM)
    w13_blk = (1, d, D_EXPERT)
    w2_blk = (1, D_EXPERT, d)

    def weights(s):
        lower = lambda j, ta, tb, *_: (ta[2 * j + s], 0, 0)
        higher = lambda j, ta, tb, *_: (tb[2 * j + s], 0, 0)
        return [pl.BlockSpec(blk, fn) for fn in (lower, higher) for blk in (w13_blk, w13_blk, w2_blk)]

    any_space = pl.BlockSpec(memory_space=pl.ANY)
    grid_spec = pltpu.PrefetchScalarGridSpec(
        num_scalar_prefetch=7,
        grid=(nt // 2,),
        in_specs=[smem_idx(lambda j: (0, 0, 0)),
                  smem_idx(lambda j: (1, 0, 0)),
                  smem_idx(lambda j: (jnp.minimum(2 * j + 2, nt - 1), 0, 0)),
                  smem_idx(lambda j: (jnp.minimum(2 * j + 3, nt - 1), 0, 0)),
                  smem_idx(lambda j: (jnp.where(j == 0, nt, 2 * j - 1), 0, 0)),
                  smem_idx(lambda j: (2 * j, 0, 0)),
                  smem_idx(lambda j: (nt - 1, 0, 0)),
                  pl.BlockSpec((tm, wgt.shape[1]), lambda j, *_: (2 * j, 0)),
                  pl.BlockSpec((tm, wgt.shape[1]), lambda j, *_: (2 * j + 1, 0)),
                  any_space, any_space, any_space, any_space] + weights(0) + weights(1),
        out_specs=any_space,
        scratch_shapes=[pltpu.VMEM((2, tm * ROW_TILE, lanes), f32), pltpu.VMEM((2, tm * ROW_TILE, lanes), f32),
                        pltpu.SemaphoreType.DMA((2,)), pltpu.SemaphoreType.DMA((2,)),
                        pltpu.VMEM((tm, d), bf16), pltpu.VMEM((tm, d), f32),
                        pltpu.VMEM((2, d, D_EXPERT), bf16), pltpu.VMEM((2, d, D_EXPERT), bf16),
                        pltpu.VMEM((2, D_EXPERT, d), bf16), pltpu.SemaphoreType.DMA((6,))],
    )
    return pl.pallas_call(
        _moe_kernel,
        out_shape=jax.ShapeDtypeStruct((rows + tm * ROW_TILE, lanes), f32),
        grid_spec=grid_spec,
        compiler_params=pltpu.CompilerParams(dimension_semantics=("arbitrary",),
                                             vmem_limit_bytes=_vmem_limit(48 * 1024 * 1024)),
        name="moe",
    )(tile_a, tile_b, first_pair, extra_start, extra_pair, pair_a, pair_b,
      idx, idx, idx, idx, idx, idx, idx, wgt, wgt, h2, w1, w3, w2, *([w1, w3, w2] * 4))


def _pair_tables():
    a, b = [], []
    for g in range(N_GROUPS):
        for i in range(EXPERTS_PER_GROUP):
            for j in range(i + 1, EXPERTS_PER_GROUP):
                a.append(g * EXPERTS_PER_GROUP + i)
                b.append(g * EXPERTS_PER_GROUP + j)
    return np.asarray(a, np.int32), np.asarray(b, np.int32)


def _dispatch_plan(pid, wlo, whi, tm):
    t = pid.shape[0]
    nt = t // tm
    assert t % (2 * tm) == 0
    pid_s, tok_s, wlo_s, whi_s = lax.sort((pid, jnp.arange(t, dtype=i32), wlo, whi), num_keys=1, is_stable=True)
    pairs = jnp.arange(N_PAIRS, dtype=i32)
    counts = jnp.sum((pid[:, None] == pairs[None, :]).astype(i32), axis=0)
    starts = jnp.cumsum(counts) - counts
    first_pair = pid_s.reshape(nt, tm)[:, 0]
    is_extra = jnp.logical_and(counts > 0, starts % tm != 0)
    extra_tile = starts // tm
    tile = jnp.arange(nt + 1, dtype=i32)
    extra_start = jnp.sum(jnp.logical_and(is_extra[None, :], extra_tile[None, :] < tile[:, None]).astype(i32), axis=1)
    extra_rank = jnp.cumsum(is_extra.astype(i32)) - is_extra.astype(i32)
    hit = jnp.logical_and(is_extra[None, :], extra_rank[None, :] == pairs[:, None])
    extra_pair = jnp.sum(jnp.where(hit, pairs[None, :], 0), axis=1)
    pa, pb = (jnp.asarray(tab) for tab in _pair_tables())
    is_first = (first_pair[:, None] == pairs[None, :]).astype(i32)
    row = jnp.arange(tm, dtype=i32)[None, :]
    idx = jnp.concatenate([tok_s.reshape(nt, tm), t + row], axis=0).reshape(nt + 1, 1, tm) * ROW_TILE
    wgt = jnp.stack([wlo_s, whi_s, pid_s.astype(f32), jnp.zeros_like(wlo_s)], axis=-1)
    return (jnp.sum(is_first * pa[None, :], axis=1), jnp.sum(is_first * pb[None, :], axis=1), first_pair,
            extra_start, extra_pair, pa, pb, idx, wgt)


def _final_kernel(x1_ref, y_ref, mod_ref, g_ref, out_ref):
    m = mod_ref[0]
    x2 = x1_ref[0] + m[5:6] * _tiles_to_rows(y_ref)
    ms = jnp.mean(x2 * x2, axis=-1, keepdims=True)
    out_ref[0] = x2 * lax.rsqrt(ms + EPS) * g_ref[...]


def _final(x1, y, mod3, g):
    b, s, d = x1.shape
    tm = FINAL_TILE
    nj = s // tm
    tok = pl.BlockSpec((1, tm, d), lambda i, j: (i, j, 0))
    return pl.pallas_call(
        _final_kernel,
        out_shape=jax.ShapeDtypeStruct((b, s, d), f32),
        grid=(b, nj),
        in_specs=[tok,
                  pl.BlockSpec((tm * ROW_TILE, y.shape[1]), lambda i, j: (i * nj + j, 0)),
                  pl.BlockSpec((1, MOD_CHUNKS, d), lambda i, j: (i, 0, 0)),
                  pl.BlockSpec((1, d), lambda i, j: (0, 0))],
        out_specs=tok,
        compiler_params=pltpu.CompilerParams(dimension_semantics=("arbitrary", "arbitrary"),
                                             vmem_limit_bytes=_vmem_limit(48 * 1024 * 1024)),
        name="final",
    )(x1, y, mod3, g)


def _pool_constants(tm):
    slab = V7X_MXU_DIM
    band = np.zeros((len(POOL_WINDOWS), slab, slab), np.float32)
    icnt = np.zeros((tm, POOL_WIDTH), np.float32)
    for gi, w in enumerate(POOL_WINDOWS):
        for t in range(slab):
            base, pos = (t // GRID_W) * GRID_W, t % GRID_W
            lo, hi = max(pos - w // 2, 0), min(pos + w // 2, GRID_W)
            band[gi, t, base + lo:base + hi] = 1.0
        for t in range(tm):
            pos = t % GRID_W
            cnt = min(pos + w // 2, GRID_W) - max(pos - w // 2, 0)
            icnt[t, gi * POOL_GROUP_DIM:(gi + 1) * POOL_GROUP_DIM] = 1.0 / cnt
    return jnp.asarray(band, bf16), jnp.asarray(icnt, f32)


def _block_diag(blocks):
    n = len(blocks)
    r, c = blocks[0].shape
    out = jnp.zeros((n * r, n * c), blocks[0].dtype)
    for i, blk in enumerate(blocks):
        out = out.at[i * r:(i + 1) * r, i * c:(i + 1) * c].set(blk)
    return out


def kernel(x, c, ctx, c_ctx, w_mod, b_mod, norm1_g, norm2_g, w_in, gla_a2_f, gla_ab_f, gla_a2_b, gla_ab_b,
           gla_onorm_g, pool_w, pool_scale, w_pool_br, w_gla_br, w_o, router_grp_w, router_grp_b,
           router_exp_w, router_exp_b, moe_w1, moe_w3, moe_w2, final_norm_g):
    b, s, d = x.shape
    depth = w_mod.shape[0]
    assert depth == 1 and d == D_MODEL and s % TOK_TILE == 0 and s % FINAL_TILE == 0 and s % (4 * GLA_CHUNK) == 0
    assert TOK_TILE % V7X_MXU_DIM == 0 and V7X_MXU_DIM % GRID_W == 0
    t = b * s

    cond = jnp.concatenate([c, c_ctx[None, :], jnp.zeros((COND_PAD - 1, d), f32)], axis=0)
    mod3 = _adaln(cond, w_mod[0], b_mod[0][None, :]).reshape(b + COND_PAD, MOD_CHUNKS, d)

    w_in_b = w_in[0].astype(bf16)
    wqkv = w_in_b[:, _Q0:_G0]
    wkv = w_in_b[:, _K0:_G0]
    wugg = jnp.concatenate([w_in_b[:, _U0:_Q0], w_in_b[:, _G0:_LR0]], axis=1)
    wlr = jnp.pad(w_in_b[:, _LR0:_IN_END], ((0, 0), (0, V7X_LANES - 2 * GLA_GATE_RANK)))
    a2bd = jnp.pad(_block_diag([gla_a2_f[0], gla_a2_b[0]]).astype(bf16),
                   ((0, V7X_LANES - 2 * GLA_GATE_RANK), (0, 0)))
    ab = jnp.concatenate([gla_ab_f[0], gla_ab_b[0]])[None, :]
    n1 = norm1_g[0][None, :]
    n2 = norm2_g[0][None, :]

    s0f, s0b = _ctx_states(ctx, mod3, b, n1, wkv, wlr, a2bd, ab)
    q, k, v, la = _qkv(x, mod3, n1, jnp.concatenate([wqkv, wlr], axis=1), a2bd, ab)
    o = _gla(q, k, v, la, s0f, s0b)

    band, icnt = _pool_constants(TOK_TILE)
    pwbd = _block_diag([pool_w[0, gi] for gi in range(len(POOL_WINDOWS))]).astype(bf16)
    onorm = jnp.tile(gla_onorm_g[0], GLA_HEADS)[None, :]
    wr = jnp.zeros((d, V7X_LANES), f32)
    wr = wr.at[:, :N_GROUPS].set(router_grp_w[0]).at[:, V7X_SUBLANES:V7X_SUBLANES + N_EXPERTS].set(router_exp_w[0])
    br = jnp.zeros((1, V7X_LANES), f32)
    br = br.at[0, :N_GROUPS].set(router_grp_b[0]).at[0, V7X_SUBLANES:V7X_SUBLANES + N_EXPERTS].set(router_exp_b[0])
    x1, h2, ri, rw = _mix(x, o, mod3, n1, n2, wugg, band, icnt, pwbd, pool_scale[0][None, :],
                           w_pool_br[0].astype(bf16), onorm, w_gla_br[0].astype(bf16), w_o[0].astype(bf16),
                           wr.astype(bf16), br)

    plan = _dispatch_plan(ri[0], rw[0], rw[1], MOE_TILE)
    y = _moe(*plan, h2, moe_w1[0].astype(bf16), moe_w3[0].astype(bf16), moe_w2[0].astype(bf16))
    return _final(x1, y, mod3, final_norm_g[None, :])
```

```python
import jax
import jax.numpy as jnp
import numpy as np
from jax import lax
from jax.experimental import pallas as pl
from jax.experimental.pallas import tpu as pltpu

f32 = jnp.float32
bf16 = jnp.bfloat16
i32 = jnp.int32

D_MODEL = 1024
GRID_W = 64
POOL_WINDOWS = (2, 4, 8, 16)
POOL_GROUP_DIM = 128
POOL_WIDTH = len(POOL_WINDOWS) * POOL_GROUP_DIM
GLA_HEADS = 4
GLA_DK = 512
GLA_DV = 1024
GLA_HK = GLA_DK // GLA_HEADS
GLA_HV = GLA_DV // GLA_HEADS
GLA_GATE_RANK = 16
GLA_GATE_NORMALIZER = 16.0
N_GROUPS = 4
EXPERTS_PER_GROUP = 8
N_EXPERTS = N_GROUPS * EXPERTS_PER_GROUP
D_EXPERT = 512
MOD_CHUNKS = 6
EPS = 1e-6

_U0, _Q0, _K0, _G0, _LR0, _IN_END = 0, 512, 1024, 2560, 5632, 5664

V7X_LANES = 128
V7X_SUBLANES = 8
V7X_MXU_DIM = 256
V7X_SCOPED_VMEM_CAP = 60000 * 1024

GLA_CHUNK = 128
GLA_HALF = GLA_CHUNK // 2
TOK_TILE = 512
FINAL_TILE = 1024
MOE_TILE = 256
COND_PAD = 8
N_PAIRS = N_GROUPS * (EXPERTS_PER_GROUP * (EXPERTS_PER_GROUP - 1) // 2)


def _vmem_limit(nbytes):
    return int(min(max(nbytes, 16 * 1024 * 1024), V7X_SCOPED_VMEM_CAP))


def _dot(a, b):
    return jnp.dot(a, b, preferred_element_type=f32)


def _dot_nt(a, b):
    return lax.dot_general(a, b, (((1,), (1,)), ((), ())), preferred_element_type=f32)


def _dot_tn(a, b):
    return lax.dot_general(a, b, (((0,), (0,)), ((), ())), preferred_element_type=f32)


def _sigmoid(x):
    return 1.0 / (1.0 + jnp.exp(-x))


def _silu(x):
    return x * _sigmoid(x)


def _log_sigmoid(z):
    return jnp.minimum(z, 0.0) - jnp.log1p(jnp.exp(-jnp.abs(z)))


def _norm_mod(x, g, scale, shift):
    ms = jnp.mean(x * x, axis=-1, keepdims=True)
    return (x * lax.rsqrt(ms + EPS) * g) * (1.0 + scale) + shift


def _tri_cumsum(tri, la):
    hi = la.astype(bf16)
    lo = (la - hi.astype(f32)).astype(bf16)
    return _dot(tri, hi) + _dot(tri, lo)


ROW_TILE = V7X_SUBLANES


def _rows_to_tiles(ref, val):
    n = val.shape[0]
    for a in range(ROW_TILE):
        ref[pl.ds(a, n, stride=ROW_TILE), :] = val[:, a * V7X_LANES:(a + 1) * V7X_LANES]


def _tiles_to_rows(ref):
    n = ref.shape[0] // ROW_TILE
    return jnp.concatenate([ref[pl.ds(a, n, stride=ROW_TILE), :] for a in range(ROW_TILE)], axis=1)


def _log_decays(lr, a2_ref, ab_ref):
    z = _dot(lr.astype(bf16), a2_ref[...]) + ab_ref[...]
    return _log_sigmoid(z) * (1.0 / GLA_GATE_NORMALIZER)


def _adaln_kernel(c_ref, w_ref, b_ref, o_ref):
    c = c_ref[...]
    o_ref[...] = _dot(_silu(c).astype(bf16), w_ref[...].astype(bf16)) + b_ref[...]


def _adaln(cond, w_mod, b_mod):
    rows, d = cond.shape
    n = w_mod.shape[1]
    tn = 512
    return pl.pallas_call(
        _adaln_kernel,
        out_shape=jax.ShapeDtypeStruct((rows, n), f32),
        grid=(n // tn,),
        in_specs=[pl.BlockSpec((rows, d), lambda j: (0, 0)),
                  pl.BlockSpec((d, tn), lambda j: (0, j)),
                  pl.BlockSpec((1, tn), lambda j: (0, j))],
        out_specs=pl.BlockSpec((rows, tn), lambda j: (0, j)),
        compiler_params=pltpu.CompilerParams(dimension_semantics=("arbitrary",)),
        name="adaln",
    )(cond, w_mod, b_mod)


def _ctx_kernel(ctx_ref, mod_ref, g_ref, wkv_ref, wlr_ref, a2_ref, ab_ref, sf_ref, sb_ref):
    x = ctx_ref[0]
    n = x.shape[0]
    m = mod_ref[0]
    hc = _norm_mod(x, g_ref[...], m[1:2], m[0:1]).astype(bf16)
    kv = _dot(hc, wkv_ref[...])
    la = _log_decays(_dot(hc, wlr_ref[...]), a2_ref, ab_ref)
    row = lax.broadcasted_iota(i32, (n, n), 0)
    col = lax.broadcasted_iota(i32, (n, n), 1)
    lower = jnp.where(row >= col, 1.0, 0.0).astype(bf16)
    upper = jnp.where(col >= row, 1.0, 0.0).astype(bf16)
    cum_f = _tri_cumsum(lower, la[:, :GLA_DK])
    cum_b = _tri_cumsum(upper, la[:, GLA_DK:])
    k = kv[:, :GLA_DK]
    v = kv[:, GLA_DK:].astype(bf16)
    kr_f = (k * jnp.exp(cum_f[n - 1:n] - cum_f)).astype(bf16)
    kr_b = (k * jnp.exp(cum_b[0:1] - cum_b)).astype(bf16)
    for h in range(GLA_HEADS):
        ks = slice(h * GLA_HK, (h + 1) * GLA_HK)
        vs = slice(h * GLA_HV, (h + 1) * GLA_HV)
        sf_ref[0, h] = _dot_tn(kr_f[:, ks], v[:, vs])
        sb_ref[0, h] = _dot_tn(kr_b[:, ks], v[:, vs])


def _ctx_states(ctx, mod3, ctx_row, g, wkv, wlr, a2bd, ab):
    b, n, d = ctx.shape
    st = jax.ShapeDtypeStruct((b, GLA_HEADS, GLA_HK, GLA_HV), f32)
    const = lambda i: (0, 0)
    st_spec = pl.BlockSpec((1, GLA_HEADS, GLA_HK, GLA_HV), lambda i: (i, 0, 0, 0))
    return pl.pallas_call(
        _ctx_kernel,
        out_shape=(st, st),
        grid=(b,),
        in_specs=[pl.BlockSpec((1, n, d), lambda i: (i, 0, 0)),
                  pl.BlockSpec((1, MOD_CHUNKS, d), lambda i: (ctx_row, 0, 0)),
                  pl.BlockSpec((1, d), const),
                  pl.BlockSpec(wkv.shape, const),
                  pl.BlockSpec(wlr.shape, const),
                  pl.BlockSpec(a2bd.shape, const),
                  pl.BlockSpec(ab.shape, const)],
        out_specs=(st_spec, st_spec),
        compiler_params=pltpu.CompilerParams(dimension_semantics=("arbitrary",),
                                             vmem_limit_bytes=_vmem_limit(32 * 1024 * 1024)),
        name="ctx_states",
    )(ctx, mod3, g, wkv, wlr, a2bd, ab)


def _qkv_kernel(x_ref, mod_ref, g_ref, w_ref, a2_ref, ab_ref, q_ref, k_ref, v_ref, la_ref):
    m = mod_ref[0]
    hx = _norm_mod(x_ref[0], g_ref[...], m[1:2], m[0:1]).astype(bf16)
    p = _dot(hx, w_ref[...])
    q_ref[0] = p[:, :GLA_DK].astype(bf16)
    k_ref[0] = p[:, GLA_DK:2 * GLA_DK].astype(bf16)
    v_ref[0] = p[:, 2 * GLA_DK:2 * GLA_DK + GLA_DV].astype(bf16)
    la_ref[0] = _log_decays(p[:, 2 * GLA_DK + GLA_DV:], a2_ref, ab_ref)


def _qkv(x, mod3, g, wqkv, a2bd, ab):
    b, s, d = x.shape
    tm = TOK_TILE
    const = lambda i, j: (0, 0)
    tok = lambda w: pl.BlockSpec((1, tm, w), lambda i, j: (i, j, 0))
    return pl.pallas_call(
        _qkv_kernel,
        out_shape=(jax.ShapeDtypeStruct((b, s, GLA_DK), bf16), jax.ShapeDtypeStruct((b, s, GLA_DK), bf16),
                   jax.ShapeDtypeStruct((b, s, GLA_DV), bf16), jax.ShapeDtypeStruct((b, s, 2 * GLA_DK), f32)),
        grid=(b, s // tm),
        in_specs=[tok(d),
                  pl.BlockSpec((1, MOD_CHUNKS, d), lambda i, j: (i, 0, 0)),
                  pl.BlockSpec((1, d), const),
                  pl.BlockSpec(wqkv.shape, const),
                  pl.BlockSpec(a2bd.shape, const),
                  pl.BlockSpec(ab.shape, const)],
        out_specs=(tok(GLA_DK), tok(GLA_DK), tok(GLA_DV), tok(2 * GLA_DK)),
        compiler_params=pltpu.CompilerParams(dimension_semantics=("arbitrary", "arbitrary"),
                                             vmem_limit_bytes=_vmem_limit(48 * 1024 * 1024)),
        name="qkv",
    )(x, mod3, g, wqkv, a2bd, ab)


def _gla_kernel(q_ref, k_ref, v_ref, la_ref, s0f_ref, s0b_ref, o_ref, s_sc, oacc, a_sc, ops_a, ops_b, dec_a, dec_b):
    seq = q_ref.shape[1]
    nc = seq // GLA_CHUNK
    c_len = GLA_CHUNK
    s_sc[0:GLA_HEADS] = s0f_ref[0]
    s_sc[GLA_HEADS:2 * GLA_HEADS] = s0b_ref[0]
    row = lax.broadcasted_iota(i32, (c_len, c_len), 0)
    col = lax.broadcasted_iota(i32, (c_len, c_len), 1)
    lower = row >= col
    upper = col >= row
    tri_l = jnp.where(lower, 1.0, 0.0).astype(bf16)
    tri_u = jnp.where(upper, 1.0, 0.0).astype(bf16)
    scale = GLA_HK ** -0.5

    n_hd = 2 * GLA_HEADS
    QD, QS, KI, KR = range(4)

    def chunk_rows(c):
        return pl.ds(pl.multiple_of(c * c_len, c_len), c_len)

    def chunks_of(step):
        return jnp.minimum(step, nc - 1), jnp.maximum(nc - 1 - step, 0)

    def prepare(step, ops, dec):
        tots = []
        for d, c, tri, r_row, t_row in zip((0, 1), chunks_of(step), (tri_l, tri_u), (GLA_HALF - 1, GLA_HALF),
                                           (c_len - 1, 0)):
            rows = chunk_rows(c)
            cum = _tri_cumsum(tri, la_ref[0, rows, d * GLA_DK:(d + 1) * GLA_DK])
            r = cum[r_row:r_row + 1]
            tot = cum[t_row:t_row + 1]
            qd = q_ref[0, rows, :].astype(f32) * (jnp.exp(cum - r) * scale)
            ki = k_ref[0, rows, :].astype(f32) * jnp.exp(r - cum)
            ops[4 * d + QD] = qd.astype(bf16)
            ops[4 * d + QS] = (qd * jnp.exp(r)).astype(bf16)
            ops[4 * d + KI] = ki.astype(bf16)
            ops[4 * d + KR] = (ki * jnp.exp(tot - r)).astype(bf16)
            tots.extend(tot[:, h * GLA_HK:(h + 1) * GLA_HK] for h in range(GLA_HEADS))
        pad = jnp.zeros((c_len - n_hd, GLA_HK), f32)
        dec[...] = jnp.exp(jnp.concatenate(tots + [pad], axis=0).T)

    def matmuls(step, ops, dec, second_visit):
        rows = [chunk_rows(c) for c in chunks_of(step)]
        for d, mask in ((0, lower), (1, upper)):
            for h in range(GLA_HEADS):
                ks = slice(h * GLA_HK, (h + 1) * GLA_HK)
                a = _dot_nt(ops[4 * d + QD, :, ks], ops[4 * d + KI, :, ks])
                a_sc[d * GLA_HEADS + h] = jnp.where(mask, a, 0.0).astype(bf16)
        for d in (0, 1):
            for h in range(GLA_HEADS):
                j = d * GLA_HEADS + h
                ks = slice(h * GLA_HK, (h + 1) * GLA_HK)
                vs = slice(h * GLA_HV, (h + 1) * GLA_HV)
                o = _dot(jnp.concatenate([a_sc[j], ops[4 * d + QS, :, ks]], axis=1),
                         jnp.concatenate([v_ref[0, rows[d], vs], s_sc[j].astype(bf16)], axis=0))
                if second_visit:
                    o_ref[0, rows[d], vs] = (oacc[rows[d], vs] + o).astype(o_ref.dtype)
                else:
                    oacc[rows[d], vs] = o
        for d in (0, 1):
            for h in range(GLA_HEADS):
                j = d * GLA_HEADS + h
                ks = slice(h * GLA_HK, (h + 1) * GLA_HK)
                vs = slice(h * GLA_HV, (h + 1) * GLA_HV)
                s_sc[j] = dec[:, j:j + 1] * s_sc[j] + _dot_tn(ops[4 * d + KR, :, ks], v_ref[0, rows[d], vs])

    def step_pair(second_visit):
        def body(m, carry):
            prepare(2 * m + 1, ops_b, dec_b)
            matmuls(2 * m, ops_a, dec_a, second_visit)
            prepare(2 * m + 2, ops_a, dec_a)
            matmuls(2 * m + 1, ops_b, dec_b, second_visit)
            return carry
        return body

    prepare(0, ops_a, dec_a)
    lax.fori_loop(0, nc // 4, step_pair(False), 0)
    lax.fori_loop(nc // 4, nc // 2, step_pair(True), 0)


def _gla(q, k, v, la, s0f, s0b):
    b, s, _ = q.shape
    tok = lambda w: pl.BlockSpec((1, s, w), lambda i: (i, 0, 0))
    st_spec = pl.BlockSpec((1, GLA_HEADS, GLA_HK, GLA_HV), lambda i: (i, 0, 0, 0))
    return pl.pallas_call(
        _gla_kernel,
        out_shape=jax.ShapeDtypeStruct((b, s, GLA_DV), bf16),
        grid=(b,),
        in_specs=[tok(GLA_DK), tok(GLA_DK), tok(GLA_DV), tok(2 * GLA_DK), st_spec, st_spec],
        out_specs=tok(GLA_DV),
        scratch_shapes=[pltpu.VMEM((2 * GLA_HEADS, GLA_HK, GLA_HV), f32),
                        pltpu.VMEM((s, GLA_DV), f32),
                        pltpu.VMEM((2 * GLA_HEADS, GLA_CHUNK, GLA_CHUNK), bf16),
                        pltpu.VMEM((8, GLA_CHUNK, GLA_DK), bf16), pltpu.VMEM((8, GLA_CHUNK, GLA_DK), bf16),
                        pltpu.VMEM((GLA_CHUNK, GLA_HK), f32), pltpu.VMEM((GLA_CHUNK, GLA_HK), f32)],
        compiler_params=pltpu.CompilerParams(dimension_semantics=("arbitrary",),
                                             vmem_limit_bytes=V7X_SCOPED_VMEM_CAP),
        name="gla",
    )(q, k, v, la, s0f, s0b)


def _mix_kernel(x_ref, o_ref, mod_ref, n1_ref, n2_ref, wugg_ref, band_ref, icnt_ref, pwbd_ref, pscale_ref,
                wpool_ref, onorm_ref, wgla_ref, wo_ref, wr_ref, br_ref,
                x1_ref, h2_ref, ri_ref, rw_ref):
    x = x_ref[0]
    tm = x.shape[0]
    m = mod_ref[0]
    hx = _norm_mod(x, n1_ref[...], m[1:2], m[0:1]).astype(bf16)
    p = _dot(hx, wugg_ref[...])
    u = p[:, :POOL_WIDTH]
    g = p[:, POOL_WIDTH:POOL_WIDTH + GLA_DV]
    gates = p[:, POOL_WIDTH + GLA_DV:]

    ub = u.astype(bf16)
    slab = band_ref.shape[1]
    sums = []
    for s in range(tm // slab):
        rs = slice(s * slab, (s + 1) * slab)
        sums.append(jnp.concatenate(
            [_dot(band_ref[gi], ub[rs, gi * POOL_GROUP_DIM:(gi + 1) * POOL_GROUP_DIM])
             for gi in range(len(POOL_WINDOWS))], axis=1))
    pooled = jnp.concatenate(sums, axis=0) * icnt_ref[...] - u
    y_pool = _dot(pooled.astype(bf16), pwbd_ref[...]) * pscale_ref[...]
    y_pool = _dot(y_pool.astype(bf16), wpool_ref[...])

    o = o_ref[0].astype(f32)
    normed = []
    for h in range(GLA_HEADS):
        oh = o[:, h * GLA_HV:(h + 1) * GLA_HV]
        normed.append(oh * lax.rsqrt(jnp.mean(oh * oh, axis=-1, keepdims=True) + EPS))
    on = jnp.concatenate(normed, axis=1) * onorm_ref[...]
    y_gla = _dot((on * _silu(g)).astype(bf16), wgla_ref[...])

    mixed = _sigmoid(gates[:, :D_MODEL]) * y_pool + _sigmoid(gates[:, D_MODEL:]) * y_gla
    x1 = x + m[2:3] * _dot(mixed.astype(bf16), wo_ref[...])
    x1_ref[0] = x1

    h2f = _norm_mod(x1, n2_ref[...], m[4:5], m[3:4])
    _rows_to_tiles(h2_ref, h2f)
    h2 = h2f.astype(bf16)

    lt = (_dot(h2, wr_ref[...]) + br_ref[...]).T
    g0, g1, g2, g3 = lt[0:1], lt[1:2], lt[2:3], lt[3:4]
    gmax = jnp.maximum(jnp.maximum(g0, g1), jnp.maximum(g2, g3))
    gidx = jnp.where(g0 == gmax, 0, jnp.where(g1 == gmax, 1, jnp.where(g2 == gmax, 2, 3))).astype(i32)
    p_grp = 1.0 / (jnp.exp(g0 - gmax) + jnp.exp(g1 - gmax) + jnp.exp(g2 - gmax) + jnp.exp(g3 - gmax))
    e0 = V7X_SUBLANES
    ing = lt[e0 + 3 * EXPERTS_PER_GROUP:e0 + 4 * EXPERTS_PER_GROUP]
    for gi in (2, 1, 0):
        ing = jnp.where(gidx == gi, lt[e0 + gi * EXPERTS_PER_GROUP:e0 + (gi + 1) * EXPERTS_PER_GROUP], ing)
    rid = lax.broadcasted_iota(i32, ing.shape, 0)
    v1 = jnp.max(ing, axis=0, keepdims=True)
    i1 = jnp.min(jnp.where(ing == v1, rid, EXPERTS_PER_GROUP), axis=0, keepdims=True)
    rest = jnp.where(rid == i1, -jnp.inf, ing)
    v2 = jnp.max(rest, axis=0, keepdims=True)
    i2 = jnp.min(jnp.where(rest == v2, rid, EXPERTS_PER_GROUP), axis=0, keepdims=True)
    t = jnp.exp(v2 - v1)
    w1 = p_grp / (1.0 + t)
    w2 = p_grp * t / (1.0 + t)
    first_low = i1 < i2
    ilo = jnp.minimum(i1, i2)
    ihi = jnp.maximum(i1, i2)
    pair = ilo * (EXPERTS_PER_GROUP - 1) - ((ilo * (ilo - 1)) >> 1) + (ihi - ilo - 1)
    pid = gidx * (N_PAIRS // N_GROUPS) + pair
    ri_ref[...] = jnp.where(rid == 0, pid, jnp.where(rid == 1, gidx * EXPERTS_PER_GROUP + ilo, 0))
    rw_ref[...] = jnp.where(rid == 0, jnp.where(first_low, w1, w2),
                            jnp.where(rid == 1, jnp.where(first_low, w2, w1), 0.0))


def _mix(x, o, mod3, n1, n2, wugg, band, icnt, pwbd, pscale, wpool, onorm, wgla, wo, wr, br):
    b, s, d = x.shape
    tm = TOK_TILE
    nj = s // tm
    t = b * s
    const2 = lambda i, j: (0, 0)
    const3 = lambda i, j: (0, 0, 0)
    tok = lambda w: pl.BlockSpec((1, tm, w), lambda i, j: (i, j, 0))
    lane = pl.BlockSpec((V7X_SUBLANES, tm), lambda i, j: (0, i * nj + j))
    return pl.pallas_call(
        _mix_kernel,
        out_shape=(jax.ShapeDtypeStruct((b, s, d), f32), jax.ShapeDtypeStruct((t * ROW_TILE, V7X_LANES), f32),
                   jax.ShapeDtypeStruct((V7X_SUBLANES, t), i32), jax.ShapeDtypeStruct((V7X_SUBLANES, t), f32)),
        grid=(b, nj),
        in_specs=[tok(d), tok(GLA_DV),
                  pl.BlockSpec((1, MOD_CHUNKS, d), lambda i, j: (i, 0, 0)),
                  pl.BlockSpec((1, d), const2), pl.BlockSpec((1, d), const2),
                  pl.BlockSpec(wugg.shape, const2),
                  pl.BlockSpec(band.shape, const3),
                  pl.BlockSpec(icnt.shape, const2),
                  pl.BlockSpec(pwbd.shape, const2),
                  pl.BlockSpec(pscale.shape, const2),
                  pl.BlockSpec(wpool.shape, const2),
                  pl.BlockSpec(onorm.shape, const2),
                  pl.BlockSpec(wgla.shape, const2),
                  pl.BlockSpec(wo.shape, const2),
                  pl.BlockSpec(wr.shape, const2),
                  pl.BlockSpec(br.shape, const2)],
        out_specs=(tok(d), pl.BlockSpec((tm * ROW_TILE, V7X_LANES), lambda i, j: (i * nj + j, 0)), lane, lane),
        compiler_params=pltpu.CompilerParams(dimension_semantics=("arbitrary", "arbitrary"),
                                             vmem_limit_bytes=V7X_SCOPED_VMEM_CAP),
        name="mix",
    )(x, o, mod3, n1, n2, wugg, band, icnt, pwbd, pscale, wpool, onorm, wgla, wo, wr, br)


def _moe_kernel(ta_ref, tb_ref, g0_ref, xs_ref, xg_ref, xf_ref, pa_ref, pb_ref,
                idx0_ref, idx1_ref, idxn0_ref, idxn1_ref, idxp0_ref, idxp1_ref, idxl_ref,
                wgt0_ref, wgt1_ref, h2_ref, w1_any, w3_any, w2_any,
                w1a0_ref, w3a0_ref, w2a0_ref, w1b0_ref, w3b0_ref, w2b0_ref,
                w1a1_ref, w3a1_ref, w2a1_ref, w1b1_ref, w3b1_ref, w2b1_ref, y_ref,
                xbuf, ybuf, gsem, ssem, xt_sc, yacc, wx1, wx3, wx2, wsem):
    del tb_ref
    j = pl.program_id(0)
    last = pl.num_programs(0) - 1
    tm = xbuf.shape[1] // ROW_TILE

    def token_tile(ref, off):
        return ref.at[pl.ds(pl.multiple_of(off, ROW_TILE), ROW_TILE)]

    def gather_row(off, s, r8):
        return pltpu.make_async_copy(token_tile(h2_ref, off), xbuf.at[s, pl.ds(r8, ROW_TILE)], gsem.at[s])

    def scatter_row(off, s, r8):
        return pltpu.make_async_copy(ybuf.at[s, pl.ds(r8, ROW_TILE)], token_tile(y_ref, off), ssem.at[s])

    def start_rows(make, idx_ref):
        def body(r, carry):
            make(idx_ref[0, 0, r], pl.multiple_of(r * ROW_TILE, ROW_TILE)).start()
            return carry
        lax.fori_loop(0, tm, body, 0, unroll=8)

    def gather_tile(s):
        return pltpu.make_async_copy(h2_ref.at[pl.ds(0, tm * ROW_TILE)], xbuf.at[s], gsem.at[s])

    def scatter_tile(s):
        return pltpu.make_async_copy(ybuf.at[s], y_ref.at[pl.ds(0, tm * ROW_TILE)], ssem.at[s])

    @pl.when(j == 0)
    def _():
        ybuf[...] = jnp.zeros(ybuf.shape, ybuf.dtype)
        start_rows(lambda off, r8: gather_row(off, 0, r8), idx0_ref)
        start_rows(lambda off, r8: gather_row(off, 1, r8), idx1_ref)

    n_pieces = 12
    per_piece = -(-tm // n_pieces)
    half = D_EXPERT // 2

    def expert_pass(xt, w_col, weights, between):
        w1, w3, w2 = weights
        acc = None
        for c in range(2):
            h1 = _dot(xt, w1(c * half, (c + 1) * half))
            between()
            h3 = _dot(xt, w3(c * half, (c + 1) * half))
            between()
            part = _dot((_silu(h1) * h3).astype(bf16), w2(c * half, (c + 1) * half))
            between()
            acc = part if acc is None else acc + part
        return w_col * acc

    def lower_weight(wgt_ref, expert):
        w = wgt_ref[...]
        return jnp.where(w[:, 3:4] == expert.astype(f32), w[:, 0:1], 0.0)

    def higher_weight(wgt_ref, pair):
        w = wgt_ref[...]
        return jnp.where(w[:, 2:3] == pair.astype(f32), w[:, 1:2], 0.0)

    def slices(w1, w3, w2, *lead):
        return (lambda c0, c1: w1[(*lead, slice(None), slice(c0, c1))],
                lambda c0, c1: w3[(*lead, slice(None), slice(c0, c1))],
                lambda r0, r1: w2[(*lead, slice(r0, r1), slice(None))])

    def tile_body(s, tile, idxn_ref, idxp_ref, wgt_ref, first_refs, second_refs):
        first_extra = xs_ref[tile]
        n_extra = xs_ref[tile + 1] - first_extra

        def extra_weight_copies(k):
            pair = xg_ref[first_extra + k]
            return [pltpu.make_async_copy(src.at[e], dst.at[slot], wsem.at[n])
                    for n, (src, dst, e, slot) in enumerate(((w1_any, wx1, pa_ref[pair], 0),
                                                             (w3_any, wx3, pa_ref[pair], 0),
                                                             (w2_any, wx2, pa_ref[pair], 0),
                                                             (w1_any, wx1, pb_ref[pair], 1),
                                                             (w3_any, wx3, pb_ref[pair], 1),
                                                             (w2_any, wx2, pb_ref[pair], 1)))]

        @pl.when(n_extra > 0)
        def _():
            for cp in extra_weight_copies(0):
                cp.start()

        gather_tile(s).wait()
        xt = _tiles_to_rows(xbuf.at[s]).astype(bf16)
        xt_sc[...] = xt
        piece = [0]

        def issue_rows():
            lo = piece[0] * per_piece
            rows = range(lo, min(lo + per_piece, tm))
            src_rows = [idxn_ref[0, 0, r] for r in rows]
            dst_rows = [idxp_ref[0, 0, r] for r in rows]
            for r, sr, dr in zip(rows, src_rows, dst_rows):
                gather_row(sr, s, r * ROW_TILE).start(priority=0)
                scatter_row(dr, 1 - s, r * ROW_TILE).start(priority=1)
            piece[0] += 1

        yacc[...] = (expert_pass(xt, lower_weight(wgt_ref, ta_ref[tile]), slices(*first_refs, 0), issue_rows)
                     + expert_pass(xt, higher_weight(wgt_ref, g0_ref[tile]), slices(*second_refs, 0), issue_rows))
        assert piece[0] == n_pieces

        def extra_pass(k, carry):
            for cp in extra_weight_copies(k):
                cp.wait()
            pair = xg_ref[first_extra + k]
            yacc[...] += expert_pass(xt_sc[...], higher_weight(wgt_ref, pair), slices(wx1, wx3, wx2, 1), lambda: None)

            @pl.when(xf_ref[first_extra + k] == 1)
            def _():
                yacc[...] += expert_pass(xt_sc[...], lower_weight(wgt_ref, pa_ref[pair]), slices(wx1, wx3, wx2, 0),
                                         lambda: None)

            @pl.when(k + 1 < n_extra)
            def _():
                for cp in extra_weight_copies(k + 1):
                    cp.start()

            return carry

        lax.fori_loop(0, n_extra, extra_pass, 0)

        @pl.when(tile >= 1)
        def _():
            scatter_tile(s).wait()

        _rows_to_tiles(ybuf.at[s], yacc[...])

    tile_body(0, 2 * j, idxn0_ref, idxp0_ref, wgt0_ref, (w1a0_ref, w3a0_ref, w2a0_ref), (w1b0_ref, w3b0_ref, w2b0_ref))
    tile_body(1, 2 * j + 1, idxn1_ref, idxp1_ref, wgt1_ref, (w1a1_ref, w3a1_ref, w2a1_ref),
              (w1b1_ref, w3b1_ref, w2b1_ref))

    @pl.when(j == last)
    def _():
        scatter_tile(0).wait()
        start_rows(lambda off, r8: scatter_row(off, 1, r8), idxl_ref)
        scatter_tile(1).wait()
        gather_tile(0).wait()
        gather_tile(1).wait()


def _moe(tile_a, tile_b, first_pair, extra_start, extra_pair, extra_lower, pair_a, pair_b, idx, wgt, h2, w1, w3, w2):
    nt, tm = idx.shape[0] - 1, idx.shape[2]
    assert nt % 2 == 0
    rows, lanes = h2.shape
    d = ROW_TILE * lanes
    idx_blk = (1, 1, tm)
    smem_idx = lambda fn: pl.BlockSpec(idx_blk, lambda j, *_: fn(j), memory_space=pltpu.SMEM)
    w13_blk = (1, d, D_EXPERT)
    w2_blk = (1, D_EXPERT, d)

    def weights(s):
        lower = lambda j, ta, tb, *_: (ta[2 * j + s], 0, 0)
        higher = lambda j, ta, tb, *_: (tb[2 * j + s], 0, 0)
        return [pl.BlockSpec(blk, fn) for fn in (lower, higher) for blk in (w13_blk, w13_blk, w2_blk)]

    any_space = pl.BlockSpec(memory_space=pl.ANY)
    grid_spec = pltpu.PrefetchScalarGridSpec(
        num_scalar_prefetch=8,
        grid=(nt // 2,),
        in_specs=[smem_idx(lambda j: (0, 0, 0)),
                  smem_idx(lambda j: (1, 0, 0)),
                  smem_idx(lambda j: (jnp.minimum(2 * j + 2, nt - 1), 0, 0)),
                  smem_idx(lambda j: (jnp.minimum(2 * j + 3, nt - 1), 0, 0)),
                  smem_idx(lambda j: (jnp.where(j == 0, nt, 2 * j - 1), 0, 0)),
                  smem_idx(lambda j: (2 * j, 0, 0)),
                  smem_idx(lambda j: (nt - 1, 0, 0)),
                  pl.BlockSpec((tm, wgt.shape[1]), lambda j, *_: (2 * j, 0)),
                  pl.BlockSpec((tm, wgt.shape[1]), lambda j, *_: (2 * j + 1, 0)),
                  any_space, any_space, any_space, any_space] + weights(0) + weights(1),
        out_specs=any_space,
        scratch_shapes=[pltpu.VMEM((2, tm * ROW_TILE, lanes), f32), pltpu.VMEM((2, tm * ROW_TILE, lanes), f32),
                        pltpu.SemaphoreType.DMA((2,)), pltpu.SemaphoreType.DMA((2,)),
                        pltpu.VMEM((tm, d), bf16), pltpu.VMEM((tm, d), f32),
                        pltpu.VMEM((2, d, D_EXPERT), bf16), pltpu.VMEM((2, d, D_EXPERT), bf16),
                        pltpu.VMEM((2, D_EXPERT, d), bf16), pltpu.SemaphoreType.DMA((6,))],
    )
    return pl.pallas_call(
        _moe_kernel,
        out_shape=jax.ShapeDtypeStruct((rows + tm * ROW_TILE, lanes), f32),
        grid_spec=grid_spec,
        compiler_params=pltpu.CompilerParams(dimension_semantics=("arbitrary",),
                                             vmem_limit_bytes=_vmem_limit(48 * 1024 * 1024)),
        name="moe",
    )(tile_a, tile_b, first_pair, extra_start, extra_pair, extra_lower, pair_a, pair_b,
      idx, idx, idx, idx, idx, idx, idx, wgt, wgt, h2, w1, w3, w2, *([w1, w3, w2] * 4))


def _pair_tables():
    a, b = [], []
    for g in range(N_GROUPS):
        for i in range(EXPERTS_PER_GROUP):
            for j in range(i + 1, EXPERTS_PER_GROUP):
                a.append(g * EXPERTS_PER_GROUP + i)
                b.append(g * EXPERTS_PER_GROUP + j)
    return np.asarray(a, np.int32), np.asarray(b, np.int32)


def _dispatch_plan(pid, lower, wlo, whi, tm):
    t = pid.shape[0]
    nt = t // tm
    assert t % (2 * tm) == 0
    pid_s, tok_s, low_s, wlo_s, whi_s = lax.sort((pid, jnp.arange(t, dtype=i32), lower, wlo, whi), num_keys=1,
                                                 is_stable=True)
    pairs = jnp.arange(N_PAIRS, dtype=i32)
    counts = jnp.sum((pid[:, None] == pairs[None, :]).astype(i32), axis=0)
    starts = jnp.cumsum(counts) - counts
    first_pair = pid_s.reshape(nt, tm)[:, 0]
    is_extra = jnp.logical_and(counts > 0, starts % tm != 0)
    extra_tile = starts // tm
    tile = jnp.arange(nt + 1, dtype=i32)
    extra_start = jnp.sum(jnp.logical_and(is_extra[None, :], extra_tile[None, :] < tile[:, None]).astype(i32), axis=1)
    extra_rank = jnp.cumsum(is_extra.astype(i32)) - is_extra.astype(i32)
    hit = jnp.logical_and(is_extra[None, :], extra_rank[None, :] == pairs[:, None])
    extra_pair = jnp.sum(jnp.where(hit, pairs[None, :], 0), axis=1)
    pa, pb = (jnp.asarray(tab) for tab in _pair_tables())
    before = jnp.logical_and(counts[None, :] > 0, pairs[None, :] < pairs[:, None])
    prev_pair = jnp.max(jnp.where(before, pairs[None, :], -1), axis=1)
    prev_lower = jnp.sum(jnp.where(prev_pair[:, None] == pairs[None, :], pa[None, :], 0), axis=1)
    new_lower = jnp.logical_or(prev_pair < 0, prev_lower != pa).astype(i32)
    extra_lower = jnp.sum(jnp.where(hit, new_lower[None, :], 0), axis=1)
    is_first = (first_pair[:, None] == pairs[None, :]).astype(i32)
    row = jnp.arange(tm, dtype=i32)[None, :]
    idx = jnp.concatenate([tok_s.reshape(nt, tm), t + row], axis=0).reshape(nt + 1, 1, tm) * ROW_TILE
    wgt = jnp.stack([wlo_s, whi_s, pid_s.astype(f32), low_s.astype(f32)], axis=-1)
    return (jnp.sum(is_first * pa[None, :], axis=1), jnp.sum(is_first * pb[None, :], axis=1), first_pair,
            extra_start, extra_pair, extra_lower, pa, pb, idx, wgt)


def _final_kernel(x1_ref, y_ref, mod_ref, g_ref, out_ref):
    m = mod_ref[0]
    x2 = x1_ref[0] + m[5:6] * _tiles_to_rows(y_ref)
    ms = jnp.mean(x2 * x2, axis=-1, keepdims=True)
    out_ref[0] = x2 * lax.rsqrt(ms + EPS) * g_ref[...]


def _final(x1, y, mod3, g):
    b, s, d = x1.shape
    tm = FINAL_TILE
    nj = s // tm
    tok = pl.BlockSpec((1, tm, d), lambda i, j: (i, j, 0))
    return pl.pallas_call(
        _final_kernel,
        out_shape=jax.ShapeDtypeStruct((b, s, d), f32),
        grid=(b, nj),
        in_specs=[tok,
                  pl.BlockSpec((tm * ROW_TILE, y.shape[1]), lambda i, j: (i * nj + j, 0)),
                  pl.BlockSpec((1, MOD_CHUNKS, d), lambda i, j: (i, 0, 0)),
                  pl.BlockSpec((1, d), lambda i, j: (0, 0))],
        out_specs=tok,
        compiler_params=pltpu.CompilerParams(dimension_semantics=("arbitrary", "arbitrary"),
                                             vmem_limit_bytes=_vmem_limit(48 * 1024 * 1024)),
        name="final",
    )(x1, y, mod3, g)


def _pool_constants(tm):
    slab = V7X_MXU_DIM
    band = np.zeros((len(POOL_WINDOWS), slab, slab), np.float32)
    icnt = np.zeros((tm, POOL_WIDTH), np.float32)
    for gi, w in enumerate(POOL_WINDOWS):
        for t in range(slab):
            base, pos = (t // GRID_W) * GRID_W, t % GRID_W
            lo, hi = max(pos - w // 2, 0), min(pos + w // 2, GRID_W)
            band[gi, t, base + lo:base + hi] = 1.0
        for t in range(tm):
            pos = t % GRID_W
            cnt = min(pos + w // 2, GRID_W) - max(pos - w // 2, 0)
            icnt[t, gi * POOL_GROUP_DIM:(gi + 1) * POOL_GROUP_DIM] = 1.0 / cnt
    return jnp.asarray(band, bf16), jnp.asarray(icnt, f32)


def _block_diag(blocks):
    n = len(blocks)
    r, c = blocks[0].shape
    out = jnp.zeros((n * r, n * c), blocks[0].dtype)
    for i, blk in enumerate(blocks):
        out = out.at[i * r:(i + 1) * r, i * c:(i + 1) * c].set(blk)
    return out


def kernel(x, c, ctx, c_ctx, w_mod, b_mod, norm1_g, norm2_g, w_in, gla_a2_f, gla_ab_f, gla_a2_b, gla_ab_b,
           gla_onorm_g, pool_w, pool_scale, w_pool_br, w_gla_br, w_o, router_grp_w, router_grp_b,
           router_exp_w, router_exp_b, moe_w1, moe_w3, moe_w2, final_norm_g):
    b, s, d = x.shape
    depth = w_mod.shape[0]
    assert depth == 1 and d == D_MODEL and s % TOK_TILE == 0 and s % FINAL_TILE == 0 and s % (4 * GLA_CHUNK) == 0
    assert TOK_TILE % V7X_MXU_DIM == 0 and V7X_MXU_DIM % GRID_W == 0
    t = b * s

    cond = jnp.concatenate([c, c_ctx[None, :], jnp.zeros((COND_PAD - 1, d), f32)], axis=0)
    mod3 = _adaln(cond, w_mod[0], b_mod[0][None, :]).reshape(b + COND_PAD, MOD_CHUNKS, d)

    w_in_b = w_in[0].astype(bf16)
    wqkv = w_in_b[:, _Q0:_G0]
    wkv = w_in_b[:, _K0:_G0]
    wugg = jnp.concatenate([w_in_b[:, _U0:_Q0], w_in_b[:, _G0:_LR0]], axis=1)
    wlr = jnp.pad(w_in_b[:, _LR0:_IN_END], ((0, 0), (0, V7X_LANES - 2 * GLA_GATE_RANK)))
    a2bd = jnp.pad(_block_diag([gla_a2_f[0], gla_a2_b[0]]).astype(bf16),
                   ((0, V7X_LANES - 2 * GLA_GATE_RANK), (0, 0)))
    ab = jnp.concatenate([gla_ab_f[0], gla_ab_b[0]])[None, :]
    n1 = norm1_g[0][None, :]
    n2 = norm2_g[0][None, :]

    s0f, s0b = _ctx_states(ctx, mod3, b, n1, wkv, wlr, a2bd, ab)
    q, k, v, la = _qkv(x, mod3, n1, jnp.concatenate([wqkv, wlr], axis=1), a2bd, ab)
    o = _gla(q, k, v, la, s0f, s0b)

    band, icnt = _pool_constants(TOK_TILE)
    pwbd = _block_diag([pool_w[0, gi] for gi in range(len(POOL_WINDOWS))]).astype(bf16)
    onorm = jnp.tile(gla_onorm_g[0], GLA_HEADS)[None, :]
    wr = jnp.zeros((d, V7X_LANES), f32)
    wr = wr.at[:, :N_GROUPS].set(router_grp_w[0]).at[:, V7X_SUBLANES:V7X_SUBLANES + N_EXPERTS].set(router_exp_w[0])
    br = jnp.zeros((1, V7X_LANES), f32)
    br = br.at[0, :N_GROUPS].set(router_grp_b[0]).at[0, V7X_SUBLANES:V7X_SUBLANES + N_EXPERTS].set(router_exp_b[0])
    x1, h2, ri, rw = _mix(x, o, mod3, n1, n2, wugg, band, icnt, pwbd, pool_scale[0][None, :],
                           w_pool_br[0].astype(bf16), onorm, w_gla_br[0].astype(bf16), w_o[0].astype(bf16),
                           wr.astype(bf16), br)

    plan = _dispatch_plan(ri[0], ri[1], rw[0], rw[1], MOE_TILE)
    y = _moe(*plan, h2, moe_w1[0].astype(bf16), moe_w3[0].astype(bf16), moe_w2[0].astype(bf16))
    return _final(x1, y, mod3, final_norm_g[None, :])
```

```python
import jax
import jax.numpy as jnp
import numpy as np
from jax import lax
from jax.experimental import pallas as pl
from jax.experimental.pallas import tpu as pltpu

f32 = jnp.float32
bf16 = jnp.bfloat16
i32 = jnp.int32

D_MODEL = 1024
GRID_W = 64
POOL_WINDOWS = (2, 4, 8, 16)
POOL_GROUP_DIM = 128
POOL_WIDTH = len(POOL_WINDOWS) * POOL_GROUP_DIM
GLA_HEADS = 4
GLA_DK = 512
GLA_DV = 1024
GLA_HK = GLA_DK // GLA_HEADS
GLA_HV = GLA_DV // GLA_HEADS
GLA_GATE_RANK = 16
GLA_GATE_NORMALIZER = 16.0
N_GROUPS = 4
EXPERTS_PER_GROUP = 8
N_EXPERTS = N_GROUPS * EXPERTS_PER_GROUP
D_EXPERT = 512
MOD_CHUNKS = 6
EPS = 1e-6

_U0, _Q0, _K0, _G0, _LR0, _IN_END = 0, 512, 1024, 2560, 5632, 5664

V7X_LANES = 128
V7X_SUBLANES = 8
V7X_MXU_DIM = 256
V7X_SCOPED_VMEM_CAP = 60000 * 1024

GLA_CHUNK = 128
GLA_HALF = GLA_CHUNK // 2
TOK_TILE = 512
QKV_TILE = 1024
FINAL_TILE = 1024
MOE_TILE = 256
COND_PAD = 8
N_PAIRS = N_GROUPS * (EXPERTS_PER_GROUP * (EXPERTS_PER_GROUP - 1) // 2)


def _vmem_limit(nbytes):
    return int(min(max(nbytes, 16 * 1024 * 1024), V7X_SCOPED_VMEM_CAP))


def _dot(a, b):
    return jnp.dot(a, b, preferred_element_type=f32)


def _dot_nt(a, b):
    return lax.dot_general(a, b, (((1,), (1,)), ((), ())), preferred_element_type=f32)


def _dot_tn(a, b):
    return lax.dot_general(a, b, (((0,), (0,)), ((), ())), preferred_element_type=f32)


def _sigmoid(x):
    return 1.0 / (1.0 + jnp.exp(-x))


def _silu(x):
    return x * _sigmoid(x)


def _log_sigmoid(z):
    return jnp.minimum(z, 0.0) - jnp.log1p(jnp.exp(-jnp.abs(z)))


def _norm_mod(x, g, scale, shift):
    ms = jnp.mean(x * x, axis=-1, keepdims=True)
    return (x * lax.rsqrt(ms + EPS) * g) * (1.0 + scale) + shift


def _tri_cumsum(tri, la):
    hi = la.astype(bf16)
    lo = (la - hi.astype(f32)).astype(bf16)
    return _dot(tri, hi) + _dot(tri, lo)


ROW_TILE = V7X_SUBLANES


def _rows_to_tiles(ref, val):
    n = val.shape[0]
    for a in range(ROW_TILE):
        ref[pl.ds(a, n, stride=ROW_TILE), :] = val[:, a * V7X_LANES:(a + 1) * V7X_LANES]


def _tiles_to_rows(ref):
    n = ref.shape[0] // ROW_TILE
    return jnp.concatenate([ref[pl.ds(a, n, stride=ROW_TILE), :] for a in range(ROW_TILE)], axis=1)


def _log_decays(lr, a2_ref, ab_ref):
    z = _dot(lr.astype(bf16), a2_ref[...]) + ab_ref[...]
    return _log_sigmoid(z) * (1.0 / GLA_GATE_NORMALIZER)


def _adaln_kernel(c_ref, w_ref, b_ref, o_ref):
    c = c_ref[...]
    o_ref[...] = _dot(_silu(c).astype(bf16), w_ref[...].astype(bf16)) + b_ref[...]


def _adaln(cond, w_mod, b_mod):
    rows, d = cond.shape
    n = w_mod.shape[1]
    tn = 512
    return pl.pallas_call(
        _adaln_kernel,
        out_shape=jax.ShapeDtypeStruct((rows, n), f32),
        grid=(n // tn,),
        in_specs=[pl.BlockSpec((rows, d), lambda j: (0, 0)),
                  pl.BlockSpec((d, tn), lambda j: (0, j)),
                  pl.BlockSpec((1, tn), lambda j: (0, j))],
        out_specs=pl.BlockSpec((rows, tn), lambda j: (0, j)),
        compiler_params=pltpu.CompilerParams(dimension_semantics=("arbitrary",)),
        name="adaln",
    )(cond, w_mod, b_mod)


def _ctx_kernel(ctx_ref, mod_ref, g_ref, wkv_ref, wlr_ref, a2_ref, ab_ref, sf_ref, sb_ref):
    x = ctx_ref[0]
    n = x.shape[0]
    m = mod_ref[0]
    hc = _norm_mod(x, g_ref[...], m[1:2], m[0:1]).astype(bf16)
    kv = _dot(hc, wkv_ref[...])
    la = _log_decays(_dot(hc, wlr_ref[...]), a2_ref, ab_ref)
    row = lax.broadcasted_iota(i32, (n, n), 0)
    col = lax.broadcasted_iota(i32, (n, n), 1)
    lower = jnp.where(row >= col, 1.0, 0.0).astype(bf16)
    upper = jnp.where(col >= row, 1.0, 0.0).astype(bf16)
    cum_f = _tri_cumsum(lower, la[:, :GLA_DK])
    cum_b = _tri_cumsum(upper, la[:, GLA_DK:])
    k = kv[:, :GLA_DK]
    v = kv[:, GLA_DK:].astype(bf16)
    kr_f = (k * jnp.exp(cum_f[n - 1:n] - cum_f)).astype(bf16)
    kr_b = (k * jnp.exp(cum_b[0:1] - cum_b)).astype(bf16)
    for h in range(GLA_HEADS):
        ks = slice(h * GLA_HK, (h + 1) * GLA_HK)
        vs = slice(h * GLA_HV, (h + 1) * GLA_HV)
        sf_ref[0, h] = _dot_tn(kr_f[:, ks], v[:, vs])
        sb_ref[0, h] = _dot_tn(kr_b[:, ks], v[:, vs])


def _ctx_states(ctx, mod3, ctx_row, g, wkv, wlr, a2bd, ab):
    b, n, d = ctx.shape
    st = jax.ShapeDtypeStruct((b, GLA_HEADS, GLA_HK, GLA_HV), f32)
    const = lambda i: (0, 0)
    st_spec = pl.BlockSpec((1, GLA_HEADS, GLA_HK, GLA_HV), lambda i: (i, 0, 0, 0))
    return pl.pallas_call(
        _ctx_kernel,
        out_shape=(st, st),
        grid=(b,),
        in_specs=[pl.BlockSpec((1, n, d), lambda i: (i, 0, 0)),
                  pl.BlockSpec((1, MOD_CHUNKS, d), lambda i: (ctx_row, 0, 0)),
                  pl.BlockSpec((1, d), const),
                  pl.BlockSpec(wkv.shape, const),
                  pl.BlockSpec(wlr.shape, const),
                  pl.BlockSpec(a2bd.shape, const),
                  pl.BlockSpec(ab.shape, const)],
        out_specs=(st_spec, st_spec),
        compiler_params=pltpu.CompilerParams(dimension_semantics=("arbitrary",),
                                             vmem_limit_bytes=_vmem_limit(32 * 1024 * 1024)),
        name="ctx_states",
    )(ctx, mod3, g, wkv, wlr, a2bd, ab)


def _qkv_kernel(x_ref, mod_ref, g_ref, w_ref, a2_ref, ab_ref, q_ref, k_ref, v_ref, la_ref):
    m = mod_ref[0]
    hx = _norm_mod(x_ref[0], g_ref[...], m[1:2], m[0:1]).astype(bf16)
    p = _dot(hx, w_ref[...])
    q_ref[0] = p[:, :GLA_DK].astype(bf16)
    k_ref[0] = p[:, GLA_DK:2 * GLA_DK].astype(bf16)
    v_ref[0] = p[:, 2 * GLA_DK:2 * GLA_DK + GLA_DV].astype(bf16)
    la_ref[0] = _log_decays(p[:, 2 * GLA_DK + GLA_DV:], a2_ref, ab_ref)


def _qkv(x, mod3, g, wqkv, a2bd, ab):
    b, s, d = x.shape
    tm = QKV_TILE
    const = lambda i, j: (0, 0)
    tok = lambda w: pl.BlockSpec((1, tm, w), lambda i, j: (i, j, 0))
    return pl.pallas_call(
        _qkv_kernel,
        out_shape=(jax.ShapeDtypeStruct((b, s, GLA_DK), bf16), jax.ShapeDtypeStruct((b, s, GLA_DK), bf16),
                   jax.ShapeDtypeStruct((b, s, GLA_DV), bf16), jax.ShapeDtypeStruct((b, s, 2 * GLA_DK), f32)),
        grid=(b, s // tm),
        in_specs=[tok(d),
                  pl.BlockSpec((1, MOD_CHUNKS, d), lambda i, j: (i, 0, 0)),
                  pl.BlockSpec((1, d), const),
                  pl.BlockSpec(wqkv.shape, const),
                  pl.BlockSpec(a2bd.shape, const),
                  pl.BlockSpec(ab.shape, const)],
        out_specs=(tok(GLA_DK), tok(GLA_DK), tok(GLA_DV), tok(2 * GLA_DK)),
        compiler_params=pltpu.CompilerParams(dimension_semantics=("arbitrary", "arbitrary"),
                                             vmem_limit_bytes=V7X_SCOPED_VMEM_CAP),
        name="qkv",
    )(x, mod3, g, wqkv, a2bd, ab)


def _gla_kernel(q_ref, k_ref, v_ref, la_ref, s0f_ref, s0b_ref, o_ref, s_sc, oacc, a_sc, ops_a, ops_b, dec_a, dec_b):
    seq = q_ref.shape[1]
    nc = seq // GLA_CHUNK
    c_len = GLA_CHUNK
    s_sc[0:GLA_HEADS] = s0f_ref[0]
    s_sc[GLA_HEADS:2 * GLA_HEADS] = s0b_ref[0]
    row = lax.broadcasted_iota(i32, (c_len, c_len), 0)
    col = lax.broadcasted_iota(i32, (c_len, c_len), 1)
    lower = row >= col
    upper = col >= row
    tri_l = jnp.where(lower, 1.0, 0.0).astype(bf16)
    tri_u = jnp.where(upper, 1.0, 0.0).astype(bf16)
    scale = GLA_HK ** -0.5

    n_hd = 2 * GLA_HEADS
    QD, QS, KI, KR = range(4)

    def chunk_rows(c):
        return pl.ds(pl.multiple_of(c * c_len, c_len), c_len)

    def chunks_of(step):
        return jnp.minimum(step, nc - 1), jnp.maximum(nc - 1 - step, 0)

    def prepare(step, ops, dec):
        tots = []
        for d, c, tri, r_row, t_row in zip((0, 1), chunks_of(step), (tri_l, tri_u), (GLA_HALF - 1, GLA_HALF),
                                           (c_len - 1, 0)):
            rows = chunk_rows(c)
            cum = _tri_cumsum(tri, la_ref[0, rows, d * GLA_DK:(d + 1) * GLA_DK])
            r = cum[r_row:r_row + 1]
            tot = cum[t_row:t_row + 1]
            qd = q_ref[0, rows, :].astype(f32) * (jnp.exp(cum - r) * scale)
            ki = k_ref[0, rows, :].astype(f32) * jnp.exp(r - cum)
            ops[4 * d + QD] = qd.astype(bf16)
            ops[4 * d + QS] = (qd * jnp.exp(r)).astype(bf16)
            ops[4 * d + KI] = ki.astype(bf16)
            ops[4 * d + KR] = (ki * jnp.exp(tot - r)).astype(bf16)
            tots.extend(tot[:, h * GLA_HK:(h + 1) * GLA_HK] for h in range(GLA_HEADS))
        pad = jnp.zeros((c_len - n_hd, GLA_HK), f32)
        dec[...] = jnp.exp(jnp.concatenate(tots + [pad], axis=0).T)

    def matmuls(step, ops, dec, second_visit):
        rows = [chunk_rows(c) for c in chunks_of(step)]
        for d, mask in ((0, lower), (1, upper)):
            for h in range(GLA_HEADS):
                ks = slice(h * GLA_HK, (h + 1) * GLA_HK)
                a = _dot_nt(ops[4 * d + QD, :, ks], ops[4 * d + KI, :, ks])
                a_sc[d * GLA_HEADS + h] = jnp.where(mask, a, 0.0).astype(bf16)
        for d in (0, 1):
            for h in range(GLA_HEADS):
                j = d * GLA_HEADS + h
                ks = slice(h * GLA_HK, (h + 1) * GLA_HK)
                vs = slice(h * GLA_HV, (h + 1) * GLA_HV)
                o = _dot(jnp.concatenate([a_sc[j], ops[4 * d + QS, :, ks]], axis=1),
                         jnp.concatenate([v_ref[0, rows[d], vs], s_sc[j].astype(bf16)], axis=0))
                if second_visit:
                    o_ref[0, rows[d], vs] = (oacc[rows[d], vs] + o).astype(o_ref.dtype)
                else:
                    oacc[rows[d], vs] = o
        for d in (0, 1):
            for h in range(GLA_HEADS):
                j = d * GLA_HEADS + h
                ks = slice(h * GLA_HK, (h + 1) * GLA_HK)
                vs = slice(h * GLA_HV, (h + 1) * GLA_HV)
                s_sc[j] = dec[:, j:j + 1] * s_sc[j] + _dot_tn(ops[4 * d + KR, :, ks], v_ref[0, rows[d], vs])

    def step_pair(second_visit):
        def body(m, carry):
            prepare(2 * m + 1, ops_b, dec_b)
            matmuls(2 * m, ops_a, dec_a, second_visit)
            prepare(2 * m + 2, ops_a, dec_a)
            matmuls(2 * m + 1, ops_b, dec_b, second_visit)
            return carry
        return body

    prepare(0, ops_a, dec_a)
    lax.fori_loop(0, nc // 4, step_pair(False), 0)
    lax.fori_loop(nc // 4, nc // 2, step_pair(True), 0)


def _gla(q, k, v, la, s0f, s0b):
    b, s, _ = q.shape
    tok = lambda w: pl.BlockSpec((1, s, w), lambda i: (i, 0, 0))
    st_spec = pl.BlockSpec((1, GLA_HEADS, GLA_HK, GLA_HV), lambda i: (i, 0, 0, 0))
    return pl.pallas_call(
        _gla_kernel,
        out_shape=jax.ShapeDtypeStruct((b, s, GLA_DV), bf16),
        grid=(b,),
        in_specs=[tok(GLA_DK), tok(GLA_DK), tok(GLA_DV), tok(2 * GLA_DK), st_spec, st_spec],
        out_specs=tok(GLA_DV),
        scratch_shapes=[pltpu.VMEM((2 * GLA_HEADS, GLA_HK, GLA_HV), f32),
                        pltpu.VMEM((s, GLA_DV), f32),
                        pltpu.VMEM((2 * GLA_HEADS, GLA_CHUNK, GLA_CHUNK), bf16),
                        pltpu.VMEM((8, GLA_CHUNK, GLA_DK), bf16), pltpu.VMEM((8, GLA_CHUNK, GLA_DK), bf16),
                        pltpu.VMEM((GLA_CHUNK, GLA_HK), f32), pltpu.VMEM((GLA_CHUNK, GLA_HK), f32)],
        compiler_params=pltpu.CompilerParams(dimension_semantics=("arbitrary",),
                                             vmem_limit_bytes=V7X_SCOPED_VMEM_CAP),
        name="gla",
    )(q, k, v, la, s0f, s0b)


def _mix_kernel(x_ref, o_ref, mod_ref, n1_ref, n2_ref, wugg_ref, band_ref, icnt_ref, pwbd_ref, pscale_ref,
                wpool_ref, onorm_ref, wgla_ref, wo_ref, wr_ref, br_ref,
                x1_ref, h2_ref, ri_ref, rw_ref):
    x = x_ref[0]
    tm = x.shape[0]
    m = mod_ref[0]
    hx = _norm_mod(x, n1_ref[...], m[1:2], m[0:1]).astype(bf16)
    p = _dot(hx, wugg_ref[...])
    u = p[:, :POOL_WIDTH]
    g = p[:, POOL_WIDTH:POOL_WIDTH + GLA_DV]
    gates = p[:, POOL_WIDTH + GLA_DV:]

    ub = u.astype(bf16)
    slab = band_ref.shape[1]
    sums = []
    for s in range(tm // slab):
        rs = slice(s * slab, (s + 1) * slab)
        sums.append(jnp.concatenate(
            [_dot(band_ref[gi], ub[rs, gi * POOL_GROUP_DIM:(gi + 1) * POOL_GROUP_DIM])
             for gi in range(len(POOL_WINDOWS))], axis=1))
    pooled = jnp.concatenate(sums, axis=0) * icnt_ref[...] - u
    y_pool = _dot(pooled.astype(bf16), pwbd_ref[...]) * pscale_ref[...]
    y_pool = _dot(y_pool.astype(bf16), wpool_ref[...])

    o = o_ref[0].astype(f32)
    normed = []
    for h in range(GLA_HEADS):
        oh = o[:, h * GLA_HV:(h + 1) * GLA_HV]
        normed.append(oh * lax.rsqrt(jnp.mean(oh * oh, axis=-1, keepdims=True) + EPS))
    on = jnp.concatenate(normed, axis=1) * onorm_ref[...]
    y_gla = _dot((on * _silu(g)).astype(bf16), wgla_ref[...])

    mixed = _sigmoid(gates[:, :D_MODEL]) * y_pool + _sigmoid(gates[:, D_MODEL:]) * y_gla
    x1 = x + m[2:3] * _dot(mixed.astype(bf16), wo_ref[...])
    x1_ref[0] = x1

    h2f = _norm_mod(x1, n2_ref[...], m[4:5], m[3:4])
    _rows_to_tiles(h2_ref, h2f)
    h2 = h2f.astype(bf16)

    lt = (_dot(h2, wr_ref[...]) + br_ref[...]).T
    g0, g1, g2, g3 = lt[0:1], lt[1:2], lt[2:3], lt[3:4]
    gmax = jnp.maximum(jnp.maximum(g0, g1), jnp.maximum(g2, g3))
    gidx = jnp.where(g0 == gmax, 0, jnp.where(g1 == gmax, 1, jnp.where(g2 == gmax, 2, 3))).astype(i32)
    p_grp = 1.0 / (jnp.exp(g0 - gmax) + jnp.exp(g1 - gmax) + jnp.exp(g2 - gmax) + jnp.exp(g3 - gmax))
    e0 = V7X_SUBLANES
    ing = lt[e0 + 3 * EXPERTS_PER_GROUP:e0 + 4 * EXPERTS_PER_GROUP]
    for gi in (2, 1, 0):
        ing = jnp.where(gidx == gi, lt[e0 + gi * EXPERTS_PER_GROUP:e0 + (gi + 1) * EXPERTS_PER_GROUP], ing)
    rid = lax.broadcasted_iota(i32, ing.shape, 0)
    v1 = jnp.max(ing, axis=0, keepdims=True)
    i1 = jnp.min(jnp.where(ing == v1, rid, EXPERTS_PER_GROUP), axis=0, keepdims=True)
    rest = jnp.where(rid == i1, -jnp.inf, ing)
    v2 = jnp.max(rest, axis=0, keepdims=True)
    i2 = jnp.min(jnp.where(rest == v2, rid, EXPERTS_PER_GROUP), axis=0, keepdims=True)
    t = jnp.exp(v2 - v1)
    w1 = p_grp / (1.0 + t)
    w2 = p_grp * t / (1.0 + t)
    first_low = i1 < i2
    ilo = jnp.minimum(i1, i2)
    ihi = jnp.maximum(i1, i2)
    pair = ilo * (EXPERTS_PER_GROUP - 1) - ((ilo * (ilo - 1)) >> 1) + (ihi - ilo - 1)
    pid = gidx * (N_PAIRS // N_GROUPS) + pair
    ri_ref[...] = jnp.where(rid == 0, pid, jnp.where(rid == 1, gidx * EXPERTS_PER_GROUP + ilo, 0))
    rw_ref[...] = jnp.where(rid == 0, jnp.where(first_low, w1, w2),
                            jnp.where(rid == 1, jnp.where(first_low, w2, w1), 0.0))


def _mix(x, o, mod3, n1, n2, wugg, band, icnt, pwbd, pscale, wpool, onorm, wgla, wo, wr, br):
    b, s, d = x.shape
    tm = TOK_TILE
    nj = s // tm
    t = b * s
    const2 = lambda i, j: (0, 0)
    const3 = lambda i, j: (0, 0, 0)
    tok = lambda w: pl.BlockSpec((1, tm, w), lambda i, j: (i, j, 0))
    lane = pl.BlockSpec((V7X_SUBLANES, tm), lambda i, j: (0, i * nj + j))
    return pl.pallas_call(
        _mix_kernel,
        out_shape=(jax.ShapeDtypeStruct((b, s, d), f32), jax.ShapeDtypeStruct((t * ROW_TILE, V7X_LANES), f32),
                   jax.ShapeDtypeStruct((V7X_SUBLANES, t), i32), jax.ShapeDtypeStruct((V7X_SUBLANES, t), f32)),
        grid=(b, nj),
        in_specs=[tok(d), tok(GLA_DV),
                  pl.BlockSpec((1, MOD_CHUNKS, d), lambda i, j: (i, 0, 0)),
                  pl.BlockSpec((1, d), const2), pl.BlockSpec((1, d), const2),
                  pl.BlockSpec(wugg.shape, const2),
                  pl.BlockSpec(band.shape, const3),
                  pl.BlockSpec(icnt.shape, const2),
                  pl.BlockSpec(pwbd.shape, const2),
                  pl.BlockSpec(pscale.shape, const2),
                  pl.BlockSpec(wpool.shape, const2),
                  pl.BlockSpec(onorm.shape, const2),
                  pl.BlockSpec(wgla.shape, const2),
                  pl.BlockSpec(wo.shape, const2),
                  pl.BlockSpec(wr.shape, const2),
                  pl.BlockSpec(br.shape, const2)],
        out_specs=(tok(d), pl.BlockSpec((tm * ROW_TILE, V7X_LANES), lambda i, j: (i * nj + j, 0)), lane, lane),
        compiler_params=pltpu.CompilerParams(dimension_semantics=("arbitrary", "arbitrary"),
                                             vmem_limit_bytes=V7X_SCOPED_VMEM_CAP),
        name="mix",
    )(x, o, mod3, n1, n2, wugg, band, icnt, pwbd, pscale, wpool, onorm, wgla, wo, wr, br)


def _moe_kernel(ta_ref, tb_ref, g0_ref, xs_ref, xg_ref, xf_ref, pa_ref, pb_ref,
                idx0_ref, idx1_ref, idxn0_ref, idxn1_ref, idxp0_ref, idxp1_ref, idxl_ref,
                wgt0_ref, wgt1_ref, h2_ref, w1_any, w3_any, w2_any,
                w1a0_ref, w3a0_ref, w2a0_ref, w1b0_ref, w3b0_ref, w2b0_ref,
                w1a1_ref, w3a1_ref, w2a1_ref, w1b1_ref, w3b1_ref, w2b1_ref, y_ref,
                xbuf, ybuf, gsem, ssem, xt_sc, yacc, wx1, wx3, wx2, wsem):
    del tb_ref
    j = pl.program_id(0)
    last = pl.num_programs(0) - 1
    tm = xbuf.shape[1] // ROW_TILE

    def token_tile(ref, off):
        return ref.at[pl.ds(pl.multiple_of(off, ROW_TILE), ROW_TILE)]

    def gather_row(off, s, r8):
        return pltpu.make_async_copy(token_tile(h2_ref, off), xbuf.at[s, pl.ds(r8, ROW_TILE)], gsem.at[s])

    def scatter_row(off, s, r8):
        return pltpu.make_async_copy(ybuf.at[s, pl.ds(r8, ROW_TILE)], token_tile(y_ref, off), ssem.at[s])

    def start_rows(make, idx_ref):
        def body(r, carry):
            make(idx_ref[0, 0, r], pl.multiple_of(r * ROW_TILE, ROW_TILE)).start()
            return carry
        lax.fori_loop(0, tm, body, 0, unroll=8)

    def gather_tile(s):
        return pltpu.make_async_copy(h2_ref.at[pl.ds(0, tm * ROW_TILE)], xbuf.at[s], gsem.at[s])

    def scatter_tile(s):
        return pltpu.make_async_copy(ybuf.at[s], y_ref.at[pl.ds(0, tm * ROW_TILE)], ssem.at[s])

    @pl.when(j == 0)
    def _():
        ybuf[...] = jnp.zeros(ybuf.shape, ybuf.dtype)
        start_rows(lambda off, r8: gather_row(off, 0, r8), idx0_ref)
        start_rows(lambda off, r8: gather_row(off, 1, r8), idx1_ref)

    n_pieces = 12
    per_piece = -(-tm // n_pieces)
    half = D_EXPERT // 2

    def expert_pass(xt, w_col, weights, between):
        w1, w3, w2 = weights
        acc = None
        for c in range(2):
            h1 = _dot(xt, w1(c * half, (c + 1) * half))
            between()
            h3 = _dot(xt, w3(c * half, (c + 1) * half))
            between()
            part = _dot((_silu(h1) * h3).astype(bf16), w2(c * half, (c + 1) * half))
            between()
            acc = part if acc is None else acc + part
        return w_col * acc

    def lower_weight(wgt_ref, expert):
        w = wgt_ref[...]
        return jnp.where(w[:, 3:4] == expert.astype(f32), w[:, 0:1], 0.0)

    def higher_weight(wgt_ref, pair):
        w = wgt_ref[...]
        return jnp.where(w[:, 2:3] == pair.astype(f32), w[:, 1:2], 0.0)

    def slices(w1, w3, w2, *lead):
        return (lambda c0, c1: w1[(*lead, slice(None), slice(c0, c1))],
                lambda c0, c1: w3[(*lead, slice(None), slice(c0, c1))],
                lambda r0, r1: w2[(*lead, slice(r0, r1), slice(None))])

    def tile_body(s, tile, idxn_ref, idxp_ref, wgt_ref, first_refs, second_refs):
        first_extra = xs_ref[tile]
        n_extra = xs_ref[tile + 1] - first_extra

        def extra_weight_copies(k):
            pair = xg_ref[first_extra + k]
            return [pltpu.make_async_copy(src.at[e], dst.at[slot], wsem.at[n])
                    for n, (src, dst, e, slot) in enumerate(((w1_any, wx1, pa_ref[pair], 0),
                                                             (w3_any, wx3, pa_ref[pair], 0),
                                                             (w2_any, wx2, pa_ref[pair], 0),
                                                             (w1_any, wx1, pb_ref[pair], 1),
                                                             (w3_any, wx3, pb_ref[pair], 1),
                                                             (w2_any, wx2, pb_ref[pair], 1)))]

        @pl.when(n_extra > 0)
        def _():
            for cp in extra_weight_copies(0):
                cp.start()

        gather_tile(s).wait()
        xt = _tiles_to_rows(xbuf.at[s]).astype(bf16)
        xt_sc[...] = xt
        piece = [0]

        def issue_rows():
            lo = piece[0] * per_piece
            rows = range(lo, min(lo + per_piece, tm))
            src_rows = [idxn_ref[0, 0, r] for r in rows]
            dst_rows = [idxp_ref[0, 0, r] for r in rows]
            for r, sr, dr in zip(rows, src_rows, dst_rows):
                gather_row(sr, s, r * ROW_TILE).start(priority=0)
                scatter_row(dr, 1 - s, r * ROW_TILE).start(priority=1)
            piece[0] += 1

        yacc[...] = (expert_pass(xt, lower_weight(wgt_ref, ta_ref[tile]), slices(*first_refs, 0), issue_rows)
                     + expert_pass(xt, higher_weight(wgt_ref, g0_ref[tile]), slices(*second_refs, 0), issue_rows))
        assert piece[0] == n_pieces

        def extra_pass(k, carry):
            for cp in extra_weight_copies(k):
                cp.wait()
            pair = xg_ref[first_extra + k]
            yacc[...] += expert_pass(xt_sc[...], higher_weight(wgt_ref, pair), slices(wx1, wx3, wx2, 1), lambda: None)

            @pl.when(xf_ref[first_extra + k] == 1)
            def _():
                yacc[...] += expert_pass(xt_sc[...], lower_weight(wgt_ref, pa_ref[pair]), slices(wx1, wx3, wx2, 0),
                                         lambda: None)

            @pl.when(k + 1 < n_extra)
            def _():
                for cp in extra_weight_copies(k + 1):
                    cp.start()

            return carry

        lax.fori_loop(0, n_extra, extra_pass, 0)

        @pl.when(tile >= 1)
        def _():
            scatter_tile(s).wait()

        _rows_to_tiles(ybuf.at[s], yacc[...])

    tile_body(0, 2 * j, idxn0_ref, idxp0_ref, wgt0_ref, (w1a0_ref, w3a0_ref, w2a0_ref), (w1b0_ref, w3b0_ref, w2b0_ref))
    tile_body(1, 2 * j + 1, idxn1_ref, idxp1_ref, wgt1_ref, (w1a1_ref, w3a1_ref, w2a1_ref),
              (w1b1_ref, w3b1_ref, w2b1_ref))

    @pl.when(j == last)
    def _():
        scatter_tile(0).wait()
        start_rows(lambda off, r8: scatter_row(off, 1, r8), idxl_ref)
        scatter_tile(1).wait()
        gather_tile(0).wait()
        gather_tile(1).wait()


def _moe(tile_a, tile_b, first_pair, extra_start, extra_pair, extra_lower, pair_a, pair_b, idx, wgt, h2, w1, w3, w2):
    nt, tm = idx.shape[0] - 1, idx.shape[2]
    assert nt % 2 == 0
    rows, lanes = h2.shape
    d = ROW_TILE * lanes
    idx_blk = (1, 1, tm)
    smem_idx = lambda fn: pl.BlockSpec(idx_blk, lambda j, *_: fn(j), memory_space=pltpu.SMEM)
    w13_blk = (1, d, D_EXPERT)
    w2_blk = (1, D_EXPERT, d)

    def weights(s):
        lower = lambda j, ta, tb, *_: (ta[2 * j + s], 0, 0)
        higher = lambda j, ta, tb, *_: (tb[2 * j + s], 0, 0)
        return [pl.BlockSpec(blk, fn) for fn in (lower, higher) for blk in (w13_blk, w13_blk, w2_blk)]

    any_space = pl.BlockSpec(memory_space=pl.ANY)
    grid_spec = pltpu.PrefetchScalarGridSpec(
        num_scalar_prefetch=8,
        grid=(nt // 2,),
        in_specs=[smem_idx(lambda j: (0, 0, 0)),
                  smem_idx(lambda j: (1, 0, 0)),
                  smem_idx(lambda j: (jnp.minimum(2 * j + 2, nt - 1), 0, 0)),
                  smem_idx(lambda j: (jnp.minimum(2 * j + 3, nt - 1), 0, 0)),
                  smem_idx(lambda j: (jnp.where(j == 0, nt, 2 * j - 1), 0, 0)),
                  smem_idx(lambda j: (2 * j, 0, 0)),
                  smem_idx(lambda j: (nt - 1, 0, 0)),
                  pl.BlockSpec((tm, wgt.shape[1]), lambda j, *_: (2 * j, 0)),
                  pl.BlockSpec((tm, wgt.shape[1]), lambda j, *_: (2 * j + 1, 0)),
                  any_space, any_space, any_space, any_space] + weights(0) + weights(1),
        out_specs=any_space,
        scratch_shapes=[pltpu.VMEM((2, tm * ROW_TILE, lanes), f32), pltpu.VMEM((2, tm * ROW_TILE, lanes), f32),
                        pltpu.SemaphoreType.DMA((2,)), pltpu.SemaphoreType.DMA((2,)),
                        pltpu.VMEM((tm, d), bf16), pltpu.VMEM((tm, d), f32),
                        pltpu.VMEM((2, d, D_EXPERT), bf16), pltpu.VMEM((2, d, D_EXPERT), bf16),
                        pltpu.VMEM((2, D_EXPERT, d), bf16), pltpu.SemaphoreType.DMA((6,))],
    )
    return pl.pallas_call(
        _moe_kernel,
        out_shape=jax.ShapeDtypeStruct((rows + tm * ROW_TILE, lanes), f32),
        grid_spec=grid_spec,
        compiler_params=pltpu.CompilerParams(dimension_semantics=("arbitrary",),
                                             vmem_limit_bytes=_vmem_limit(48 * 1024 * 1024)),
        name="moe",
    )(tile_a, tile_b, first_pair, extra_start, extra_pair, extra_lower, pair_a, pair_b,
      idx, idx, idx, idx, idx, idx, idx, wgt, wgt, h2, w1, w3, w2, *([w1, w3, w2] * 4))


def _pair_tables():
    a, b = [], []
    for g in range(N_GROUPS):
        for i in range(EXPERTS_PER_GROUP):
            for j in range(i + 1, EXPERTS_PER_GROUP):
                a.append(g * EXPERTS_PER_GROUP + i)
                b.append(g * EXPERTS_PER_GROUP + j)
    return np.asarray(a, np.int32), np.asarray(b, np.int32)


def _dispatch_plan(pid, lower, wlo, whi, tm):
    t = pid.shape[0]
    nt = t // tm
    assert t % (2 * tm) == 0
    pid_s, tok_s, low_s, wlo_s, whi_s = lax.sort((pid, jnp.arange(t, dtype=i32), lower, wlo, whi), num_keys=1,
                                                 is_stable=True)
    pairs = jnp.arange(N_PAIRS, dtype=i32)
    counts = jnp.sum((pid[:, None] == pairs[None, :]).astype(i32), axis=0)
    starts = jnp.cumsum(counts) - counts
    first_pair = pid_s.reshape(nt, tm)[:, 0]
    is_extra = jnp.logical_and(counts > 0, starts % tm != 0)
    extra_tile = starts // tm
    tile = jnp.arange(nt + 1, dtype=i32)
    extra_start = jnp.sum(jnp.logical_and(is_extra[None, :], extra_tile[None, :] < tile[:, None]).astype(i32), axis=1)
    extra_rank = jnp.cumsum(is_extra.astype(i32)) - is_extra.astype(i32)
    hit = jnp.logical_and(is_extra[None, :], extra_rank[None, :] == pairs[:, None])
    extra_pair = jnp.sum(jnp.where(hit, pairs[None, :], 0), axis=1)
    pa, pb = (jnp.asarray(tab) for tab in _pair_tables())
    before = jnp.logical_and(counts[None, :] > 0, pairs[None, :] < pairs[:, None])
    prev_pair = jnp.max(jnp.where(before, pairs[None, :], -1), axis=1)
    prev_lower = jnp.sum(jnp.where(prev_pair[:, None] == pairs[None, :], pa[None, :], 0), axis=1)
    new_lower = jnp.logical_or(prev_pair < 0, prev_lower != pa).astype(i32)
    extra_lower = jnp.sum(jnp.where(hit, new_lower[None, :], 0), axis=1)
    is_first = (first_pair[:, None] == pairs[None, :]).astype(i32)
    row = jnp.arange(tm, dtype=i32)[None, :]
    idx = jnp.concatenate([tok_s.reshape(nt, tm), t + row], axis=0).reshape(nt + 1, 1, tm) * ROW_TILE
    wgt = jnp.stack([wlo_s, whi_s, pid_s.astype(f32), low_s.astype(f32)], axis=-1)
    return (jnp.sum(is_first * pa[None, :], axis=1), jnp.sum(is_first * pb[None, :], axis=1), first_pair,
            extra_start, extra_pair, extra_lower, pa, pb, idx, wgt)


def _final_kernel(x1_ref, y_ref, mod_ref, g_ref, out_ref):
    m = mod_ref[0]
    x2 = x1_ref[0] + m[5:6] * _tiles_to_rows(y_ref)
    ms = jnp.mean(x2 * x2, axis=-1, keepdims=True)
    out_ref[0] = x2 * lax.rsqrt(ms + EPS) * g_ref[...]


def _final(x1, y, mod3, g):
    b, s, d = x1.shape
    tm = FINAL_TILE
    nj = s // tm
    tok = pl.BlockSpec((1, tm, d), lambda i, j: (i, j, 0))
    return pl.pallas_call(
        _final_kernel,
        out_shape=jax.ShapeDtypeStruct((b, s, d), f32),
        grid=(b, nj),
        in_specs=[tok,
                  pl.BlockSpec((tm * ROW_TILE, y.shape[1]), lambda i, j: (i * nj + j, 0)),
                  pl.BlockSpec((1, MOD_CHUNKS, d), lambda i, j: (i, 0, 0)),
                  pl.BlockSpec((1, d), lambda i, j: (0, 0))],
        out_specs=tok,
        compiler_params=pltpu.CompilerParams(dimension_semantics=("arbitrary", "arbitrary"),
                                             vmem_limit_bytes=_vmem_limit(48 * 1024 * 1024)),
        name="final",
    )(x1, y, mod3, g)


def _pool_constants(tm):
    slab = V7X_MXU_DIM
    band = np.zeros((len(POOL_WINDOWS), slab, slab), np.float32)
    icnt = np.zeros((tm, POOL_WIDTH), np.float32)
    for gi, w in enumerate(POOL_WINDOWS):
        for t in range(slab):
            base, pos = (t // GRID_W) * GRID_W, t % GRID_W
            lo, hi = max(pos - w // 2, 0), min(pos + w // 2, GRID_W)
            band[gi, t, base + lo:base + hi] = 1.0
        for t in range(tm):
            pos = t % GRID_W
            cnt = min(pos + w // 2, GRID_W) - max(pos - w // 2, 0)
            icnt[t, gi * POOL_GROUP_DIM:(gi + 1) * POOL_GROUP_DIM] = 1.0 / cnt
    return jnp.asarray(band, bf16), jnp.asarray(icnt, f32)


def _block_diag(blocks):
    n = len(blocks)
    r, c = blocks[0].shape
    out = jnp.zeros((n * r, n * c), blocks[0].dtype)
    for i, blk in enumerate(blocks):
        out = out.at[i * r:(i + 1) * r, i * c:(i + 1) * c].set(blk)
    return out


def kernel(x, c, ctx, c_ctx, w_mod, b_mod, norm1_g, norm2_g, w_in, gla_a2_f, gla_ab_f, gla_a2_b, gla_ab_b,
           gla_onorm_g, pool_w, pool_scale, w_pool_br, w_gla_br, w_o, router_grp_w, router_grp_b,
           router_exp_w, router_exp_b, moe_w1, moe_w3, moe_w2, final_norm_g):
    b, s, d = x.shape
    depth = w_mod.shape[0]
    assert depth == 1 and d == D_MODEL and s % TOK_TILE == 0 and s % FINAL_TILE == 0 and s % (4 * GLA_CHUNK) == 0
    assert TOK_TILE % V7X_MXU_DIM == 0 and V7X_MXU_DIM % GRID_W == 0
    t = b * s

    cond = jnp.concatenate([c, c_ctx[None, :], jnp.zeros((COND_PAD - 1, d), f32)], axis=0)
    mod3 = _adaln(cond, w_mod[0], b_mod[0][None, :]).reshape(b + COND_PAD, MOD_CHUNKS, d)

    w_in_b = w_in[0].astype(bf16)
    wqkv = w_in_b[:, _Q0:_G0]
    wkv = w_in_b[:, _K0:_G0]
    wugg = jnp.concatenate([w_in_b[:, _U0:_Q0], w_in_b[:, _G0:_LR0]], axis=1)
    wlr = jnp.pad(w_in_b[:, _LR0:_IN_END], ((0, 0), (0, V7X_LANES - 2 * GLA_GATE_RANK)))
    a2bd = jnp.pad(_block_diag([gla_a2_f[0], gla_a2_b[0]]).astype(bf16),
                   ((0, V7X_LANES - 2 * GLA_GATE_RANK), (0, 0)))
    ab = jnp.concatenate([gla_ab_f[0], gla_ab_b[0]])[None, :]
    n1 = norm1_g[0][None, :]
    n2 = norm2_g[0][None, :]

    s0f, s0b = _ctx_states(ctx, mod3, b, n1, wkv, wlr, a2bd, ab)
    q, k, v, la = _qkv(x, mod3, n1, jnp.concatenate([wqkv, wlr], axis=1), a2bd, ab)
    o = _gla(q, k, v, la, s0f, s0b)

    band, icnt = _pool_constants(TOK_TILE)
    pwbd = _block_diag([pool_w[0, gi] for gi in range(len(POOL_WINDOWS))]).astype(bf16)
    onorm = jnp.tile(gla_onorm_g[0], GLA_HEADS)[None, :]
    wr = jnp.zeros((d, V7X_LANES), f32)
    wr = wr.at[:, :N_GROUPS].set(router_grp_w[0]).at[:, V7X_SUBLANES:V7X_SUBLANES + N_EXPERTS].set(router_exp_w[0])
    br = jnp.zeros((1, V7X_LANES), f32)
    br = br.at[0, :N_GROUPS].set(router_grp_b[0]).at[0, V7X_SUBLANES:V7X_SUBLANES + N_EXPERTS].set(router_exp_b[0])
    x1, h2, ri, rw = _mix(x, o, mod3, n1, n2, wugg, band, icnt, pwbd, pool_scale[0][None, :],
                           w_pool_br[0].astype(bf16), onorm, w_gla_br[0].astype(bf16), w_o[0].astype(bf16),
                           wr.astype(bf16), br)

    plan = _dispatch_plan(ri[0], ri[1], rw[0], rw[1], MOE_TILE)
    y = _moe(*plan, h2, moe_w1[0].astype(bf16), moe_w3[0].astype(bf16), moe_w2[0].astype(bf16))
    return _final(x1, y, mod3, final_norm_g[None, :])
```

```python
import jax
import jax.numpy as jnp
import numpy as np
from jax import lax
from jax.experimental import pallas as pl
from jax.experimental.pallas import tpu as pltpu

f32 = jnp.float32
bf16 = jnp.bfloat16
i32 = jnp.int32

D_MODEL = 1024
GRID_W = 64
POOL_WINDOWS = (2, 4, 8, 16)
POOL_GROUP_DIM = 128
POOL_WIDTH = len(POOL_WINDOWS) * POOL_GROUP_DIM
GLA_HEADS = 4
GLA_DK = 512
GLA_DV = 1024
GLA_HK = GLA_DK // GLA_HEADS
GLA_HV = GLA_DV // GLA_HEADS
GLA_GATE_RANK = 16
GLA_GATE_NORMALIZER = 16.0
N_GROUPS = 4
EXPERTS_PER_GROUP = 8
N_EXPERTS = N_GROUPS * EXPERTS_PER_GROUP
D_EXPERT = 512
MOD_CHUNKS = 6
EPS = 1e-6

_U0, _Q0, _K0, _G0, _LR0, _IN_END = 0, 512, 1024, 2560, 5632, 5664

V7X_LANES = 128
V7X_SUBLANES = 8
V7X_MXU_DIM = 256
V7X_SCOPED_VMEM_CAP = 60000 * 1024

GLA_CHUNK = 128
GLA_HALF = GLA_CHUNK // 2
TOK_TILE = 512
QKV_TILE = 1024
FINAL_TILE = 1024
MOE_TILE = 256
CTX_GROUP = 2
COND_PAD = 8
N_PAIRS = N_GROUPS * (EXPERTS_PER_GROUP * (EXPERTS_PER_GROUP - 1) // 2)


def _vmem_limit(nbytes):
    return int(min(max(nbytes, 16 * 1024 * 1024), V7X_SCOPED_VMEM_CAP))


def _dot(a, b):
    return jnp.dot(a, b, preferred_element_type=f32)


def _dot_nt(a, b):
    return lax.dot_general(a, b, (((1,), (1,)), ((), ())), preferred_element_type=f32)


def _dot_tn(a, b):
    return lax.dot_general(a, b, (((0,), (0,)), ((), ())), preferred_element_type=f32)


def _sigmoid(x):
    return 1.0 / (1.0 + jnp.exp(-x))


def _silu(x):
    return x * _sigmoid(x)


def _log_sigmoid(z):
    return jnp.minimum(z, 0.0) - jnp.log1p(jnp.exp(-jnp.abs(z)))


def _norm_mod(x, g, scale, shift):
    ms = jnp.mean(x * x, axis=-1, keepdims=True)
    return (x * lax.rsqrt(ms + EPS) * g) * (1.0 + scale) + shift


def _tri_cumsum(tri, la):
    hi = la.astype(bf16)
    lo = (la - hi.astype(f32)).astype(bf16)
    return _dot(tri, hi) + _dot(tri, lo)


ROW_TILE = V7X_SUBLANES


def _rows_to_tiles(ref, val):
    n = val.shape[0]
    for a in range(ROW_TILE):
        ref[pl.ds(a, n, stride=ROW_TILE), :] = val[:, a * V7X_LANES:(a + 1) * V7X_LANES]


def _tiles_to_rows(ref):
    n = ref.shape[0] // ROW_TILE
    return jnp.concatenate([ref[pl.ds(a, n, stride=ROW_TILE), :] for a in range(ROW_TILE)], axis=1)


def _log_decays(lr, a2_ref, ab_ref):
    z = _dot(lr.astype(bf16), a2_ref[...]) + ab_ref[...]
    return _log_sigmoid(z) * (1.0 / GLA_GATE_NORMALIZER)


def _adaln_kernel(c_ref, w_ref, b_ref, o_ref):
    c = c_ref[...]
    o_ref[...] = _dot(_silu(c).astype(bf16), w_ref[...].astype(bf16)) + b_ref[...]


def _adaln(cond, w_mod, b_mod):
    rows, d = cond.shape
    n = w_mod.shape[1]
    tn = 512
    return pl.pallas_call(
        _adaln_kernel,
        out_shape=jax.ShapeDtypeStruct((rows, n), f32),
        grid=(n // tn,),
        in_specs=[pl.BlockSpec((rows, d), lambda j: (0, 0)),
                  pl.BlockSpec((d, tn), lambda j: (0, j)),
                  pl.BlockSpec((1, tn), lambda j: (0, j))],
        out_specs=pl.BlockSpec((rows, tn), lambda j: (0, j)),
        compiler_params=pltpu.CompilerParams(dimension_semantics=("arbitrary",)),
        name="adaln",
    )(cond, w_mod, b_mod)


def _ctx_kernel(ctx_ref, mod_ref, g_ref, wkv_ref, wlr_ref, a2_ref, ab_ref, sf_ref, sb_ref):
    group, n, d = ctx_ref.shape
    m = mod_ref[0]
    hc = _norm_mod(ctx_ref[...].reshape(group * n, d), g_ref[...], m[1:2], m[0:1]).astype(bf16)
    kv_all = _dot(hc, wkv_ref[...])
    la_all = _log_decays(_dot(hc, wlr_ref[...]), a2_ref, ab_ref)
    row = lax.broadcasted_iota(i32, (n, n), 0)
    col = lax.broadcasted_iota(i32, (n, n), 1)
    lower = jnp.where(row >= col, 1.0, 0.0).astype(bf16)
    upper = jnp.where(col >= row, 1.0, 0.0).astype(bf16)
    for bi in range(group):
        kv = kv_all[bi * n:(bi + 1) * n]
        la = la_all[bi * n:(bi + 1) * n]
        cum_f = _tri_cumsum(lower, la[:, :GLA_DK])
        cum_b = _tri_cumsum(upper, la[:, GLA_DK:])
        k = kv[:, :GLA_DK]
        v = kv[:, GLA_DK:].astype(bf16)
        kr_f = (k * jnp.exp(cum_f[n - 1:n] - cum_f)).astype(bf16)
        kr_b = (k * jnp.exp(cum_b[0:1] - cum_b)).astype(bf16)
        for h in range(GLA_HEADS):
            ks = slice(h * GLA_HK, (h + 1) * GLA_HK)
            vs = slice(h * GLA_HV, (h + 1) * GLA_HV)
            sf_ref[bi, h] = _dot_tn(kr_f[:, ks], v[:, vs])
            sb_ref[bi, h] = _dot_tn(kr_b[:, ks], v[:, vs])


def _ctx_states(ctx, mod3, ctx_row, g, wkv, wlr, a2bd, ab):
    b, n, d = ctx.shape
    group = CTX_GROUP
    assert b % group == 0
    st = jax.ShapeDtypeStruct((b, GLA_HEADS, GLA_HK, GLA_HV), f32)
    const = lambda i: (0, 0)
    st_spec = pl.BlockSpec((group, GLA_HEADS, GLA_HK, GLA_HV), lambda i: (i, 0, 0, 0))
    return pl.pallas_call(
        _ctx_kernel,
        out_shape=(st, st),
        grid=(b // group,),
        in_specs=[pl.BlockSpec((group, n, d), lambda i: (i, 0, 0)),
                  pl.BlockSpec((1, MOD_CHUNKS, d), lambda i: (ctx_row, 0, 0)),
                  pl.BlockSpec((1, d), const),
                  pl.BlockSpec(wkv.shape, const),
                  pl.BlockSpec(wlr.shape, const),
                  pl.BlockSpec(a2bd.shape, const),
                  pl.BlockSpec(ab.shape, const)],
        out_specs=(st_spec, st_spec),
        compiler_params=pltpu.CompilerParams(dimension_semantics=("arbitrary",),
                                             vmem_limit_bytes=_vmem_limit(32 * 1024 * 1024)),
        name="ctx_states",
    )(ctx, mod3, g, wkv, wlr, a2bd, ab)


def _qkv_kernel(x_ref, mod_ref, g_ref, w_ref, a2_ref, ab_ref, q_ref, k_ref, v_ref, la_ref):
    m = mod_ref[0]
    hx = _norm_mod(x_ref[0], g_ref[...], m[1:2], m[0:1]).astype(bf16)
    p = _dot(hx, w_ref[...])
    q_ref[0] = p[:, :GLA_DK].astype(bf16)
    k_ref[0] = p[:, GLA_DK:2 * GLA_DK].astype(bf16)
    v_ref[0] = p[:, 2 * GLA_DK:2 * GLA_DK + GLA_DV].astype(bf16)
    la_ref[0] = _log_decays(p[:, 2 * GLA_DK + GLA_DV:], a2_ref, ab_ref)


def _qkv(x, mod3, g, wqkv, a2bd, ab):
    b, s, d = x.shape
    tm = QKV_TILE
    const = lambda i, j: (0, 0)
    tok = lambda w: pl.BlockSpec((1, tm, w), lambda i, j: (i, j, 0))
    return pl.pallas_call(
        _qkv_kernel,
        out_shape=(jax.ShapeDtypeStruct((b, s, GLA_DK), bf16), jax.ShapeDtypeStruct((b, s, GLA_DK), bf16),
                   jax.ShapeDtypeStruct((b, s, GLA_DV), bf16), jax.ShapeDtypeStruct((b, s, 2 * GLA_DK), f32)),
        grid=(b, s // tm),
        in_specs=[tok(d),
                  pl.BlockSpec((1, MOD_CHUNKS, d), lambda i, j: (i, 0, 0)),
                  pl.BlockSpec((1, d), const),
                  pl.BlockSpec(wqkv.shape, const),
                  pl.BlockSpec(a2bd.shape, const),
                  pl.BlockSpec(ab.shape, const)],
        out_specs=(tok(GLA_DK), tok(GLA_DK), tok(GLA_DV), tok(2 * GLA_DK)),
        compiler_params=pltpu.CompilerParams(dimension_semantics=("arbitrary", "arbitrary"),
                                             vmem_limit_bytes=V7X_SCOPED_VMEM_CAP),
        name="qkv",
    )(x, mod3, g, wqkv, a2bd, ab)


def _gla_kernel(q_ref, k_ref, v_ref, la_ref, s0f_ref, s0b_ref, o_ref, s_sc, oacc, a_sc, ops_a, ops_b, dec_a, dec_b):
    seq = q_ref.shape[1]
    nc = seq // GLA_CHUNK
    c_len = GLA_CHUNK
    s_sc[0:GLA_HEADS] = s0f_ref[0]
    s_sc[GLA_HEADS:2 * GLA_HEADS] = s0b_ref[0]
    row = lax.broadcasted_iota(i32, (c_len, c_len), 0)
    col = lax.broadcasted_iota(i32, (c_len, c_len), 1)
    lower = row >= col
    upper = col >= row
    tri_l = jnp.where(lower, 1.0, 0.0).astype(bf16)
    tri_u = jnp.where(upper, 1.0, 0.0).astype(bf16)
    scale = GLA_HK ** -0.5

    n_hd = 2 * GLA_HEADS
    QD, QS, KI, KR = range(4)

    def chunk_rows(c):
        return pl.ds(pl.multiple_of(c * c_len, c_len), c_len)

    def chunks_of(step):
        return jnp.minimum(step, nc - 1), jnp.maximum(nc - 1 - step, 0)

    def prepare(step, ops, dec):
        tots = []
        for d, c, tri, r_row, t_row in zip((0, 1), chunks_of(step), (tri_l, tri_u), (GLA_HALF - 1, GLA_HALF),
                                           (c_len - 1, 0)):
            rows = chunk_rows(c)
            cum = _tri_cumsum(tri, la_ref[0, rows, d * GLA_DK:(d + 1) * GLA_DK])
            r = cum[r_row:r_row + 1]
            tot = cum[t_row:t_row + 1]
            qd = q_ref[0, rows, :].astype(f32) * (jnp.exp(cum - r) * scale)
            ki = k_ref[0, rows, :].astype(f32) * jnp.exp(r - cum)
            ops[4 * d + QD] = qd.astype(bf16)
            ops[4 * d + QS] = (qd * jnp.exp(r)).astype(bf16)
            ops[4 * d + KI] = ki.astype(bf16)
            ops[4 * d + KR] = (ki * jnp.exp(tot - r)).astype(bf16)
            tots.extend(tot[:, h * GLA_HK:(h + 1) * GLA_HK] for h in range(GLA_HEADS))
        pad = jnp.zeros((c_len - n_hd, GLA_HK), f32)
        dec[...] = jnp.exp(jnp.concatenate(tots + [pad], axis=0).T)

    def matmuls(step, ops, dec, second_visit):
        rows = [chunk_rows(c) for c in chunks_of(step)]
        for d, mask in ((0, lower), (1, upper)):
            for h in range(GLA_HEADS):
                ks = slice(h * GLA_HK, (h + 1) * GLA_HK)
                a = _dot_nt(ops[4 * d + QD, :, ks], ops[4 * d + KI, :, ks])
                a_sc[d * GLA_HEADS + h] = jnp.where(mask, a, 0.0).astype(bf16)
        for d in (0, 1):
            for h in range(GLA_HEADS):
                j = d * GLA_HEADS + h
                ks = slice(h * GLA_HK, (h + 1) * GLA_HK)
                vs = slice(h * GLA_HV, (h + 1) * GLA_HV)
                o = _dot(jnp.concatenate([a_sc[j], ops[4 * d + QS, :, ks]], axis=1),
                         jnp.concatenate([v_ref[0, rows[d], vs], s_sc[j].astype(bf16)], axis=0))
                if second_visit:
                    o_ref[0, rows[d], vs] = (oacc[rows[d], vs] + o).astype(o_ref.dtype)
                else:
                    oacc[rows[d], vs] = o
        for d in (0, 1):
            for h in range(GLA_HEADS):
                j = d * GLA_HEADS + h
                ks = slice(h * GLA_HK, (h + 1) * GLA_HK)
                vs = slice(h * GLA_HV, (h + 1) * GLA_HV)
                s_sc[j] = dec[:, j:j + 1] * s_sc[j] + _dot_tn(ops[4 * d + KR, :, ks], v_ref[0, rows[d], vs])

    def step_pair(second_visit):
        def body(m, carry):
            prepare(2 * m + 1, ops_b, dec_b)
            matmuls(2 * m, ops_a, dec_a, second_visit)
            prepare(2 * m + 2, ops_a, dec_a)
            matmuls(2 * m + 1, ops_b, dec_b, second_visit)
            return carry
        return body

    prepare(0, ops_a, dec_a)
    lax.fori_loop(0, nc // 4, step_pair(False), 0)
    lax.fori_loop(nc // 4, nc // 2, step_pair(True), 0)


def _gla(q, k, v, la, s0f, s0b):
    b, s, _ = q.shape
    tok = lambda w: pl.BlockSpec((1, s, w), lambda i: (i, 0, 0))
    st_spec = pl.BlockSpec((1, GLA_HEADS, GLA_HK, GLA_HV), lambda i: (i, 0, 0, 0))
    return pl.pallas_call(
        _gla_kernel,
        out_shape=jax.ShapeDtypeStruct((b, s, GLA_DV), bf16),
        grid=(b,),
        in_specs=[tok(GLA_DK), tok(GLA_DK), tok(GLA_DV), tok(2 * GLA_DK), st_spec, st_spec],
        out_specs=tok(GLA_DV),
        scratch_shapes=[pltpu.VMEM((2 * GLA_HEADS, GLA_HK, GLA_HV), f32),
                        pltpu.VMEM((s, GLA_DV), f32),
                        pltpu.VMEM((2 * GLA_HEADS, GLA_CHUNK, GLA_CHUNK), bf16),
                        pltpu.VMEM((8, GLA_CHUNK, GLA_DK), bf16), pltpu.VMEM((8, GLA_CHUNK, GLA_DK), bf16),
                        pltpu.VMEM((GLA_CHUNK, GLA_HK), f32), pltpu.VMEM((GLA_CHUNK, GLA_HK), f32)],
        compiler_params=pltpu.CompilerParams(dimension_semantics=("arbitrary",),
                                             vmem_limit_bytes=V7X_SCOPED_VMEM_CAP),
        name="gla",
    )(q, k, v, la, s0f, s0b)


def _mix_kernel(x_ref, o_ref, mod_ref, n1_ref, n2_ref, wugg_ref, band_ref, icnt_ref, pwbd_ref, pscale_ref,
                wpool_ref, onorm_ref, wgla_ref, wo_ref, wr_ref, br_ref,
                x1_ref, h2_ref, ri_ref, rw_ref):
    x = x_ref[0]
    tm = x.shape[0]
    m = mod_ref[0]
    hx = _norm_mod(x, n1_ref[...], m[1:2], m[0:1]).astype(bf16)
    p = _dot(hx, wugg_ref[...])
    u = p[:, :POOL_WIDTH]
    g = p[:, POOL_WIDTH:POOL_WIDTH + GLA_DV]
    gates = p[:, POOL_WIDTH + GLA_DV:]

    ub = u.astype(bf16)
    slab = band_ref.shape[1]
    sums = []
    for s in range(tm // slab):
        rs = slice(s * slab, (s + 1) * slab)
        sums.append(jnp.concatenate(
            [_dot(band_ref[gi], ub[rs, gi * POOL_GROUP_DIM:(gi + 1) * POOL_GROUP_DIM])
             for gi in range(len(POOL_WINDOWS))], axis=1))
    pooled = jnp.concatenate(sums, axis=0) * icnt_ref[...] - u
    y_pool = _dot(pooled.astype(bf16), pwbd_ref[...]) * pscale_ref[...]
    y_pool = _dot(y_pool.astype(bf16), wpool_ref[...])

    o = o_ref[0].astype(f32)
    normed = []
    for h in range(GLA_HEADS):
        oh = o[:, h * GLA_HV:(h + 1) * GLA_HV]
        normed.append(oh * lax.rsqrt(jnp.mean(oh * oh, axis=-1, keepdims=True) + EPS))
    on = jnp.concatenate(normed, axis=1) * onorm_ref[...]
    y_gla = _dot((on * _silu(g)).astype(bf16), wgla_ref[...])

    mixed = _sigmoid(gates[:, :D_MODEL]) * y_pool + _sigmoid(gates[:, D_MODEL:]) * y_gla
    x1 = x + m[2:3] * _dot(mixed.astype(bf16), wo_ref[...])
    x1_ref[0] = x1

    h2f = _norm_mod(x1, n2_ref[...], m[4:5], m[3:4])
    _rows_to_tiles(h2_ref, h2f)
    h2 = h2f.astype(bf16)

    lt = (_dot(h2, wr_ref[...]) + br_ref[...]).T
    g0, g1, g2, g3 = lt[0:1], lt[1:2], lt[2:3], lt[3:4]
    gmax = jnp.maximum(jnp.maximum(g0, g1), jnp.maximum(g2, g3))
    gidx = jnp.where(g0 == gmax, 0, jnp.where(g1 == gmax, 1, jnp.where(g2 == gmax, 2, 3))).astype(i32)
    p_grp = 1.0 / (jnp.exp(g0 - gmax) + jnp.exp(g1 - gmax) + jnp.exp(g2 - gmax) + jnp.exp(g3 - gmax))
    e0 = V7X_SUBLANES
    ing = lt[e0 + 3 * EXPERTS_PER_GROUP:e0 + 4 * EXPERTS_PER_GROUP]
    for gi in (2, 1, 0):
        ing = jnp.where(gidx == gi, lt[e0 + gi * EXPERTS_PER_GROUP:e0 + (gi + 1) * EXPERTS_PER_GROUP], ing)
    rid = lax.broadcasted_iota(i32, ing.shape, 0)
    v1 = jnp.max(ing, axis=0, keepdims=True)
    i1 = jnp.min(jnp.where(ing == v1, rid, EXPERTS_PER_GROUP), axis=0, keepdims=True)
    rest = jnp.where(rid == i1, -jnp.inf, ing)
    v2 = jnp.max(rest, axis=0, keepdims=True)
    i2 = jnp.min(jnp.where(rest == v2, rid, EXPERTS_PER_GROUP), axis=0, keepdims=True)
    t = jnp.exp(v2 - v1)
    w1 = p_grp / (1.0 + t)
    w2 = p_grp * t / (1.0 + t)
    first_low = i1 < i2
    ilo = jnp.minimum(i1, i2)
    ihi = jnp.maximum(i1, i2)
    pair = ilo * (EXPERTS_PER_GROUP - 1) - ((ilo * (ilo - 1)) >> 1) + (ihi - ilo - 1)
    pid = gidx * (N_PAIRS // N_GROUPS) + pair
    ri_ref[...] = jnp.where(rid == 0, pid, jnp.where(rid == 1, gidx * EXPERTS_PER_GROUP + ilo, 0))
    rw_ref[...] = jnp.where(rid == 0, jnp.where(first_low, w1, w2),
                            jnp.where(rid == 1, jnp.where(first_low, w2, w1), 0.0))


def _mix(x, o, mod3, n1, n2, wugg, band, icnt, pwbd, pscale, wpool, onorm, wgla, wo, wr, br):
    b, s, d = x.shape
    tm = TOK_TILE
    nj = s // tm
    t = b * s
    const2 = lambda i, j: (0, 0)
    const3 = lambda i, j: (0, 0, 0)
    tok = lambda w: pl.BlockSpec((1, tm, w), lambda i, j: (i, j, 0))
    lane = pl.BlockSpec((V7X_SUBLANES, tm), lambda i, j: (0, i * nj + j))
    return pl.pallas_call(
        _mix_kernel,
        out_shape=(jax.ShapeDtypeStruct((b, s, d), f32), jax.ShapeDtypeStruct((t * ROW_TILE, V7X_LANES), f32),
                   jax.ShapeDtypeStruct((V7X_SUBLANES, t), i32), jax.ShapeDtypeStruct((V7X_SUBLANES, t), f32)),
        grid=(b, nj),
        in_specs=[tok(d), tok(GLA_DV),
                  pl.BlockSpec((1, MOD_CHUNKS, d), lambda i, j: (i, 0, 0)),
                  pl.BlockSpec((1, d), const2), pl.BlockSpec((1, d), const2),
                  pl.BlockSpec(wugg.shape, const2),
                  pl.BlockSpec(band.shape, const3),
                  pl.BlockSpec(icnt.shape, const2),
                  pl.BlockSpec(pwbd.shape, const2),
                  pl.BlockSpec(pscale.shape, const2),
                  pl.BlockSpec(wpool.shape, const2),
                  pl.BlockSpec(onorm.shape, const2),
                  pl.BlockSpec(wgla.shape, const2),
                  pl.BlockSpec(wo.shape, const2),
                  pl.BlockSpec(wr.shape, const2),
                  pl.BlockSpec(br.shape, const2)],
        out_specs=(tok(d), pl.BlockSpec((tm * ROW_TILE, V7X_LANES), lambda i, j: (i * nj + j, 0)), lane, lane),
        compiler_params=pltpu.CompilerParams(dimension_semantics=("arbitrary", "arbitrary"),
                                             vmem_limit_bytes=V7X_SCOPED_VMEM_CAP),
        name="mix",
    )(x, o, mod3, n1, n2, wugg, band, icnt, pwbd, pscale, wpool, onorm, wgla, wo, wr, br)


def _moe_kernel(ta_ref, tb_ref, g0_ref, xs_ref, xg_ref, xf_ref, pa_ref, pb_ref,
                idx0_ref, idx1_ref, idxn0_ref, idxn1_ref, idxp0_ref, idxp1_ref, idxl_ref,
                wgt0_ref, wgt1_ref, h2_ref, w1_any, w3_any, w2_any,
                w1a0_ref, w3a0_ref, w2a0_ref, w1b0_ref, w3b0_ref, w2b0_ref,
                w1a1_ref, w3a1_ref, w2a1_ref, w1b1_ref, w3b1_ref, w2b1_ref, y_ref,
                xbuf, ybuf, gsem, ssem, xt_sc, yacc, wx1, wx3, wx2, wsem):
    del tb_ref
    j = pl.program_id(0)
    last = pl.num_programs(0) - 1
    tm = xbuf.shape[1] // ROW_TILE

    def token_tile(ref, off):
        return ref.at[pl.ds(pl.multiple_of(off, ROW_TILE), ROW_TILE)]

    def gather_row(off, s, r8):
        return pltpu.make_async_copy(token_tile(h2_ref, off), xbuf.at[s, pl.ds(r8, ROW_TILE)], gsem.at[s])

    def scatter_row(off, s, r8):
        return pltpu.make_async_copy(ybuf.at[s, pl.ds(r8, ROW_TILE)], token_tile(y_ref, off), ssem.at[s])

    def start_rows(make, idx_ref):
        def body(r, carry):
            make(idx_ref[0, 0, r], pl.multiple_of(r * ROW_TILE, ROW_TILE)).start()
            return carry
        lax.fori_loop(0, tm, body, 0, unroll=8)

    def gather_tile(s):
        return pltpu.make_async_copy(h2_ref.at[pl.ds(0, tm * ROW_TILE)], xbuf.at[s], gsem.at[s])

    def scatter_tile(s):
        return pltpu.make_async_copy(ybuf.at[s], y_ref.at[pl.ds(0, tm * ROW_TILE)], ssem.at[s])

    @pl.when(j == 0)
    def _():
        ybuf[...] = jnp.zeros(ybuf.shape, ybuf.dtype)
        start_rows(lambda off, r8: gather_row(off, 0, r8), idx0_ref)
        start_rows(lambda off, r8: gather_row(off, 1, r8), idx1_ref)

    n_pieces = 12
    per_piece = -(-tm // n_pieces)
    half = D_EXPERT // 2

    def expert_pass(xt, w_col, weights, between):
        w1, w3, w2 = weights
        acc = None
        for c in range(2):
            h1 = _dot(xt, w1(c * half, (c + 1) * half))
            between()
            h3 = _dot(xt, w3(c * half, (c + 1) * half))
            between()
            part = _dot((_silu(h1) * h3).astype(bf16), w2(c * half, (c + 1) * half))
            between()
            acc = part if acc is None else acc + part
        return w_col * acc

    def lower_weight(wgt_ref, expert):
        w = wgt_ref[...]
        return jnp.where(w[:, 3:4] == expert.astype(f32), w[:, 0:1], 0.0)

    def higher_weight(wgt_ref, pair):
        w = wgt_ref[...]
        return jnp.where(w[:, 2:3] == pair.astype(f32), w[:, 1:2], 0.0)

    def slices(w1, w3, w2, *lead):
        return (lambda c0, c1: w1[(*lead, slice(None), slice(c0, c1))],
                lambda c0, c1: w3[(*lead, slice(None), slice(c0, c1))],
                lambda r0, r1: w2[(*lead, slice(r0, r1), slice(None))])

    def tile_body(s, tile, idxn_ref, idxp_ref, wgt_ref, first_refs, second_refs):
        first_extra = xs_ref[tile]
        n_extra = xs_ref[tile + 1] - first_extra

        def extra_weight_copies(k):
            pair = xg_ref[first_extra + k]
            return [pltpu.make_async_copy(src.at[e], dst.at[slot], wsem.at[n])
                    for n, (src, dst, e, slot) in enumerate(((w1_any, wx1, pa_ref[pair], 0),
                                                             (w3_any, wx3, pa_ref[pair], 0),
                                                             (w2_any, wx2, pa_ref[pair], 0),
                                                             (w1_any, wx1, pb_ref[pair], 1),
                                                             (w3_any, wx3, pb_ref[pair], 1),
                                                             (w2_any, wx2, pb_ref[pair], 1)))]

        @pl.when(n_extra > 0)
        def _():
            for cp in extra_weight_copies(0):
                cp.start()

        gather_tile(s).wait()
        xt = _tiles_to_rows(xbuf.at[s]).astype(bf16)
        xt_sc[...] = xt
        piece = [0]

        def issue_rows():
            lo = piece[0] * per_piece
            rows = range(lo, min(lo + per_piece, tm))
            src_rows = [idxn_ref[0, 0, r] for r in rows]
            dst_rows = [idxp_ref[0, 0, r] for r in rows]
            for r, sr, dr in zip(rows, src_rows, dst_rows):
                gather_row(sr, s, r * ROW_TILE).start(priority=0)
                scatter_row(dr, 1 - s, r * ROW_TILE).start(priority=1)
            piece[0] += 1

        yacc[...] = (expert_pass(xt, lower_weight(wgt_ref, ta_ref[tile]), slices(*first_refs, 0), issue_rows)
                     + expert_pass(xt, higher_weight(wgt_ref, g0_ref[tile]), slices(*second_refs, 0), issue_rows))
        assert piece[0] == n_pieces

        def extra_pass(k, carry):
            for cp in extra_weight_copies(k):
                cp.wait()
            pair = xg_ref[first_extra + k]
            yacc[...] += expert_pass(xt_sc[...], higher_weight(wgt_ref, pair), slices(wx1, wx3, wx2, 1), lambda: None)

            @pl.when(xf_ref[first_extra + k] == 1)
            def _():
                yacc[...] += expert_pass(xt_sc[...], lower_weight(wgt_ref, pa_ref[pair]), slices(wx1, wx3, wx2, 0),
                                         lambda: None)

            @pl.when(k + 1 < n_extra)
            def _():
                for cp in extra_weight_copies(k + 1):
                    cp.start()

            return carry

        lax.fori_loop(0, n_extra, extra_pass, 0)

        @pl.when(tile >= 1)
        def _():
            scatter_tile(s).wait()

        _rows_to_tiles(ybuf.at[s], yacc[...])

    tile_body(0, 2 * j, idxn0_ref, idxp0_ref, wgt0_ref, (w1a0_ref, w3a0_ref, w2a0_ref), (w1b0_ref, w3b0_ref, w2b0_ref))
    tile_body(1, 2 * j + 1, idxn1_ref, idxp1_ref, wgt1_ref, (w1a1_ref, w3a1_ref, w2a1_ref),
              (w1b1_ref, w3b1_ref, w2b1_ref))

    @pl.when(j == last)
    def _():
        scatter_tile(0).wait()
        start_rows(lambda off, r8: scatter_row(off, 1, r8), idxl_ref)
        scatter_tile(1).wait()
        gather_tile(0).wait()
        gather_tile(1).wait()


def _moe(tile_a, tile_b, first_pair, extra_start, extra_pair, extra_lower, pair_a, pair_b, idx, wgt, h2, w1, w3, w2):
    nt, tm = idx.shape[0] - 1, idx.shape[2]
    assert nt % 2 == 0
    rows, lanes = h2.shape
    d = ROW_TILE * lanes
    idx_blk = (1, 1, tm)
    smem_idx = lambda fn: pl.BlockSpec(idx_blk, lambda j, *_: fn(j), memory_space=pltpu.SMEM)
    w13_blk = (1, d, D_EXPERT)
    w2_blk = (1, D_EXPERT, d)

    def weights(s):
        lower = lambda j, ta, tb, *_: (ta[2 * j + s], 0, 0)
        higher = lambda j, ta, tb, *_: (tb[2 * j + s], 0, 0)
        return [pl.BlockSpec(blk, fn) for fn in (lower, higher) for blk in (w13_blk, w13_blk, w2_blk)]

    any_space = pl.BlockSpec(memory_space=pl.ANY)
    grid_spec = pltpu.PrefetchScalarGridSpec(
        num_scalar_prefetch=8,
        grid=(nt // 2,),
        in_specs=[smem_idx(lambda j: (0, 0, 0)),
                  smem_idx(lambda j: (1, 0, 0)),
                  smem_idx(lambda j: (jnp.minimum(2 * j + 2, nt - 1), 0, 0)),
                  smem_idx(lambda j: (jnp.minimum(2 * j + 3, nt - 1), 0, 0)),
                  smem_idx(lambda j: (jnp.where(j == 0, nt, 2 * j - 1), 0, 0)),
                  smem_idx(lambda j: (2 * j, 0, 0)),
                  smem_idx(lambda j: (nt - 1, 0, 0)),
                  pl.BlockSpec((tm, wgt.shape[1]), lambda j, *_: (2 * j, 0)),
                  pl.BlockSpec((tm, wgt.shape[1]), lambda j, *_: (2 * j + 1, 0)),
                  any_space, any_space, any_space, any_space] + weights(0) + weights(1),
        out_specs=any_space,
        scratch_shapes=[pltpu.VMEM((2, tm * ROW_TILE, lanes), f32), pltpu.VMEM((2, tm * ROW_TILE, lanes), f32),
                        pltpu.SemaphoreType.DMA((2,)), pltpu.SemaphoreType.DMA((2,)),
                        pltpu.VMEM((tm, d), bf16), pltpu.VMEM((tm, d), f32),
                        pltpu.VMEM((2, d, D_EXPERT), bf16), pltpu.VMEM((2, d, D_EXPERT), bf16),
                        pltpu.VMEM((2, D_EXPERT, d), bf16), pltpu.SemaphoreType.DMA((6,))],
    )
    return pl.pallas_call(
        _moe_kernel,
        out_shape=jax.ShapeDtypeStruct((rows + tm * ROW_TILE, lanes), f32),
        grid_spec=grid_spec,
        compiler_params=pltpu.CompilerParams(dimension_semantics=("arbitrary",),
                                             vmem_limit_bytes=_vmem_limit(48 * 1024 * 1024)),
        name="moe",
    )(tile_a, tile_b, first_pair, extra_start, extra_pair, extra_lower, pair_a, pair_b,
      idx, idx, idx, idx, idx, idx, idx, wgt, wgt, h2, w1, w3, w2, *([w1, w3, w2] * 4))


def _pair_tables():
    a, b = [], []
    for g in range(N_GROUPS):
        for i in range(EXPERTS_PER_GROUP):
            for j in range(i + 1, EXPERTS_PER_GROUP):
                a.append(g * EXPERTS_PER_GROUP + i)
                b.append(g * EXPERTS_PER_GROUP + j)
    return np.asarray(a, np.int32), np.asarray(b, np.int32)


def _dispatch_plan(pid, lower, wlo, whi, tm):
    t = pid.shape[0]
    nt = t // tm
    assert t % (2 * tm) == 0
    pid_s, tok_s, low_s, wlo_s, whi_s = lax.sort((pid, jnp.arange(t, dtype=i32), lower, wlo, whi), num_keys=1,
                                                 is_stable=True)
    pairs = jnp.arange(N_PAIRS, dtype=i32)
    counts = jnp.sum((pid[:, None] == pairs[None, :]).astype(i32), axis=0)
    starts = jnp.cumsum(counts) - counts
    first_pair = pid_s.reshape(nt, tm)[:, 0]
    is_extra = jnp.logical_and(counts > 0, starts % tm != 0)
    extra_tile = starts // tm
    tile = jnp.arange(nt + 1, dtype=i32)
    extra_start = jnp.sum(jnp.logical_and(is_extra[None, :], extra_tile[None, :] < tile[:, None]).astype(i32), axis=1)
    extra_rank = jnp.cumsum(is_extra.astype(i32)) - is_extra.astype(i32)
    hit = jnp.logical_and(is_extra[None, :], extra_rank[None, :] == pairs[:, None])
    extra_pair = jnp.sum(jnp.where(hit, pairs[None, :], 0), axis=1)
    pa, pb = (jnp.asarray(tab) for tab in _pair_tables())
    before = jnp.logical_and(counts[None, :] > 0, pairs[None, :] < pairs[:, None])
    prev_pair = jnp.max(jnp.where(before, pairs[None, :], -1), axis=1)
    prev_lower = jnp.sum(jnp.where(prev_pair[:, None] == pairs[None, :], pa[None, :], 0), axis=1)
    new_lower = jnp.logical_or(prev_pair < 0, prev_lower != pa).astype(i32)
    extra_lower = jnp.sum(jnp.where(hit, new_lower[None, :], 0), axis=1)
    is_first = (first_pair[:, None] == pairs[None, :]).astype(i32)
    row = jnp.arange(tm, dtype=i32)[None, :]
    idx = jnp.concatenate([tok_s.reshape(nt, tm), t + row], axis=0).reshape(nt + 1, 1, tm) * ROW_TILE
    wgt = jnp.stack([wlo_s, whi_s, pid_s.astype(f32), low_s.astype(f32)], axis=-1)
    return (jnp.sum(is_first * pa[None, :], axis=1), jnp.sum(is_first * pb[None, :], axis=1), first_pair,
            extra_start, extra_pair, extra_lower, pa, pb, idx, wgt)


def _final_kernel(x1_ref, y_ref, mod_ref, g_ref, out_ref):
    m = mod_ref[0]
    x2 = x1_ref[0] + m[5:6] * _tiles_to_rows(y_ref)
    ms = jnp.mean(x2 * x2, axis=-1, keepdims=True)
    out_ref[0] = x2 * lax.rsqrt(ms + EPS) * g_ref[...]


def _final(x1, y, mod3, g):
    b, s, d = x1.shape
    tm = FINAL_TILE
    nj = s // tm
    tok = pl.BlockSpec((1, tm, d), lambda i, j: (i, j, 0))
    return pl.pallas_call(
        _final_kernel,
        out_shape=jax.ShapeDtypeStruct((b, s, d), f32),
        grid=(b, nj),
        in_specs=[tok,
                  pl.BlockSpec((tm * ROW_TILE, y.shape[1]), lambda i, j: (i * nj + j, 0)),
                  pl.BlockSpec((1, MOD_CHUNKS, d), lambda i, j: (i, 0, 0)),
                  pl.BlockSpec((1, d), lambda i, j: (0, 0))],
        out_specs=tok,
        compiler_params=pltpu.CompilerParams(dimension_semantics=("arbitrary", "arbitrary"),
                                             vmem_limit_bytes=_vmem_limit(48 * 1024 * 1024)),
        name="final",
    )(x1, y, mod3, g)


def _pool_constants(tm):
    slab = V7X_MXU_DIM
    band = np.zeros((len(POOL_WINDOWS), slab, slab), np.float32)
    icnt = np.zeros((tm, POOL_WIDTH), np.float32)
    for gi, w in enumerate(POOL_WINDOWS):
        for t in range(slab):
            base, pos = (t // GRID_W) * GRID_W, t % GRID_W
            lo, hi = max(pos - w // 2, 0), min(pos + w // 2, GRID_W)
            band[gi, t, base + lo:base + hi] = 1.0
        for t in range(tm):
            pos = t % GRID_W
            cnt = min(pos + w // 2, GRID_W) - max(pos - w // 2, 0)
            icnt[t, gi * POOL_GROUP_DIM:(gi + 1) * POOL_GROUP_DIM] = 1.0 / cnt
    return jnp.asarray(band, bf16), jnp.asarray(icnt, f32)


def _block_diag(blocks):
    n = len(blocks)
    r, c = blocks[0].shape
    out = jnp.zeros((n * r, n * c), blocks[0].dtype)
    for i, blk in enumerate(blocks):
        out = out.at[i * r:(i + 1) * r, i * c:(i + 1) * c].set(blk)
    return out


def kernel(x, c, ctx, c_ctx, w_mod, b_mod, norm1_g, norm2_g, w_in, gla_a2_f, gla_ab_f, gla_a2_b, gla_ab_b,
           gla_onorm_g, pool_w, pool_scale, w_pool_br, w_gla_br, w_o, router_grp_w, router_grp_b,
           router_exp_w, router_exp_b, moe_w1, moe_w3, moe_w2, final_norm_g):
    b, s, d = x.shape
    depth = w_mod.shape[0]
    assert depth == 1 and d == D_MODEL and s % TOK_TILE == 0 and s % FINAL_TILE == 0 and s % (4 * GLA_CHUNK) == 0
    assert TOK_TILE % V7X_MXU_DIM == 0 and V7X_MXU_DIM % GRID_W == 0
    t = b * s

    cond = jnp.concatenate([c, c_ctx[None, :], jnp.zeros((COND_PAD - 1, d), f32)], axis=0)
    mod3 = _adaln(cond, w_mod[0], b_mod[0][None, :]).reshape(b + COND_PAD, MOD_CHUNKS, d)

    w_in_b = w_in[0].astype(bf16)
    wqkv = w_in_b[:, _Q0:_G0]
    wkv = w_in_b[:, _K0:_G0]
    wugg = jnp.concatenate([w_in_b[:, _U0:_Q0], w_in_b[:, _G0:_LR0]], axis=1)
    wlr = jnp.pad(w_in_b[:, _LR0:_IN_END], ((0, 0), (0, V7X_LANES - 2 * GLA_GATE_RANK)))
    a2bd = jnp.pad(_block_diag([gla_a2_f[0], gla_a2_b[0]]).astype(bf16),
                   ((0, V7X_LANES - 2 * GLA_GATE_RANK), (0, 0)))
    ab = jnp.concatenate([gla_ab_f[0], gla_ab_b[0]])[None, :]
    n1 = norm1_g[0][None, :]
    n2 = norm2_g[0][None, :]

    s0f, s0b = _ctx_states(ctx, mod3, b, n1, wkv, wlr, a2bd, ab)
    q, k, v, la = _qkv(x, mod3, n1, jnp.concatenate([wqkv, wlr], axis=1), a2bd, ab)
    o = _gla(q, k, v, la, s0f, s0b)

    band, icnt = _pool_constants(TOK_TILE)
    pwbd = _block_diag([pool_w[0, gi] for gi in range(len(POOL_WINDOWS))]).astype(bf16)
    onorm = jnp.tile(gla_onorm_g[0], GLA_HEADS)[None, :]
    wr = jnp.zeros((d, V7X_LANES), f32)
    wr = wr.at[:, :N_GROUPS].set(router_grp_w[0]).at[:, V7X_SUBLANES:V7X_SUBLANES + N_EXPERTS].set(router_exp_w[0])
    br = jnp.zeros((1, V7X_LANES), f32)
    br = br.at[0, :N_GROUPS].set(router_grp_b[0]).at[0, V7X_SUBLANES:V7X_SUBLANES + N_EXPERTS].set(router_exp_b[0])
    x1, h2, ri, rw = _mix(x, o, mod3, n1, n2, wugg, band, icnt, pwbd, pool_scale[0][None, :],
                           w_pool_br[0].astype(bf16), onorm, w_gla_br[0].astype(bf16), w_o[0].astype(bf16),
                           wr.astype(bf16), br)

    plan = _dispatch_plan(ri[0], ri[1], rw[0], rw[1], MOE_TILE)
    y = _moe(*plan, h2, moe_w1[0].astype(bf16), moe_w3[0].astype(bf16), moe_w2[0].astype(bf16))
    return _final(x1, y, mod3, final_norm_g[None, :])
```
